```python
import math
import jax
import jax.numpy as jnp
from jax import lax
import numpy as np

D_MODEL = 4096
BATCH = 2
SEQ = 4096
DEPTH = 2

MOBA_HEADS = 8
MOBA_HEAD_DIM = 128
MOBA_BLOCK = 256
MOBA_TOPK = 3
MOBA_Q_CHUNK = 32
SSM_D_INNER = 1024
SSM_HEAD_DIM = 64
SSM_HEADS = SSM_D_INNER // SSM_HEAD_DIM
SSM_GROUPS = 2
SSM_STATE = 128
SSM_CONV = 4
SSM_CHUNK = 128
SSM_CONV_CH = SSM_D_INNER + 2 * SSM_GROUPS * SSM_STATE
MLA_HEADS = 8
MLA_Q_LORA = 768
MLA_KV_LORA = 512
MLA_NOPE = 128
MLA_ROPE = 64
MLA_V = 128
MLA_Q_BLOCK = 128
ROPE_THETA = 10000.0
SWA_HEADS = 16
SWA_KV_HEADS = 2
SWA_HEAD_DIM = 64
SWA_WINDOW = 128
N_BRANCH = 4
BRANCH_W = 1024
N_GROUPS = 4
EXPERTS_PER_GROUP = 8
N_EXPERTS = N_GROUPS * EXPERTS_PER_GROUP
EXPERT_TOPK = 2
EXPERT_HIDDEN = 512
MOE_BLOCK = 128
NORM_EPS = 1e-6
N_ALIBI = MOBA_HEADS + SWA_HEADS

IN_SIZES = (
    3 * MOBA_HEADS * MOBA_HEAD_DIM,
    SSM_D_INNER,
    SSM_CONV_CH,
    SSM_HEADS,
    MLA_Q_LORA,
    MLA_KV_LORA,
    MLA_ROPE,
    SWA_HEADS * SWA_HEAD_DIM,
    SWA_KV_HEADS * SWA_HEAD_DIM,
    SWA_KV_HEADS * SWA_HEAD_DIM,
    N_BRANCH * D_MODEL,
)
N_IN = sum(IN_SIZES)

kernel_name = "hybrid_gated_mixers_hier_moe_block"


def rmsnorm(x, w):
    xf = x.astype(jnp.float32)
    y = xf * lax.rsqrt(jnp.mean(xf * xf, axis=-1, keepdims=True) + NORM_EPS)
    return (y * w.astype(jnp.float32)).astype(x.dtype)


def alibi_slopes():
    i = jnp.arange(1, N_ALIBI + 1, dtype=jnp.float32)
    s = jnp.exp2(-8.0 * i / N_ALIBI)
    return s[:SWA_HEADS], s[SWA_HEADS:]


def moba_attention(q, k, v, positions, slopes):
    B, S, H, Dh = q.shape
    BLK, QC = MOBA_BLOCK, MOBA_Q_CHUNK
    nblk = -(-S // BLK)
    pad = nblk * BLK - S
    kp = jnp.pad(k, ((0, 0), (0, pad), (0, 0), (0, 0)))
    vp = jnp.pad(v, ((0, 0), (0, pad), (0, 0), (0, 0)))
    pp = jnp.pad(positions, ((0, 0), (0, pad)))
    kb = kp.reshape(B, nblk, BLK, H, Dh).transpose(0, 3, 1, 2, 4)
    vb = vp.reshape(B, nblk, BLK, H, Dh).transpose(0, 3, 1, 2, 4)
    pb = pp.reshape(B, nblk, BLK)
    k_mean = jnp.mean(kb.astype(jnp.float32), axis=3)
    gate = jnp.einsum('bshd,bhnd->bhsn', q.astype(jnp.float32), k_mean)
    q_blk = jnp.arange(S) // BLK
    past = jnp.arange(nblk)[None, :] < q_blk[:, None]
    gate = jnp.where(past, gate, -jnp.inf)
    ksel = min(MOBA_TOPK, nblk)
    _, sel = lax.top_k(gate, ksel)
    sel_valid = jnp.arange(ksel)[None, :] < q_blk[:, None]
    scale = Dh ** -0.5
    bi = jnp.arange(B)[:, None, None, None]
    hi = jnp.arange(H)[None, :, None, None]

    def chunk(ci):
        t0 = ci * QC
        qc = lax.dynamic_slice_in_dim(q, t0, QC, axis=1)
        pq = lax.dynamic_slice_in_dim(positions, t0, QC, axis=1)
        sc = lax.dynamic_slice_in_dim(sel, t0, QC, axis=2)
        valid = lax.dynamic_slice_in_dim(sel_valid, t0, QC, axis=0)
        kg = kb[bi, hi, sc]
        vg = vb[bi, hi, sc]
        pk = pb[bi, sc]
        s_past = jnp.einsum('bqhd,bhqrkd->bhqrk', qc, kg).astype(jnp.float32) * scale
        dist = (pq[:, None, :, None, None] - pk).astype(jnp.float32)
        s_past = s_past - slopes[:, None, None, None] * jnp.abs(dist)
        s_past = jnp.where(valid[None, None, :, :, None], s_past, -jnp.inf)
        s_past = s_past.reshape(B, H, QC, ksel * BLK)
        j = t0 // BLK
        k_own = lax.dynamic_index_in_dim(kb, j, axis=2, keepdims=False)
        v_own = lax.dynamic_index_in_dim(vb, j, axis=2, keepdims=False)
        pos_own = lax.dynamic_index_in_dim(pb, j, axis=1, keepdims=False)
        s_own = jnp.einsum('bqhd,bhkd->bhqk', qc, k_own).astype(jnp.float32) * scale
        d_own = (pq[:, None, :, None] - pos_own[:, None, None, :]).astype(jnp.float32)
        s_own = s_own - slopes[:, None, None] * jnp.abs(d_own)
        q_loc = t0 - j * BLK + jnp.arange(QC)
        causal = jnp.arange(BLK)[None, :] <= q_loc[:, None]
        s_own = jnp.where(causal, s_own, -jnp.inf)
        p = jax.nn.softmax(jnp.concatenate([s_past, s_own], axis=-1), axis=-1).astype(v.dtype)
        o = jnp.einsum('bhqk,bhqkd->bqhd', p[..., :ksel * BLK], vg.reshape(B, H, QC, ksel * BLK, Dh))
        return o + jnp.einsum('bhqk,bhkd->bqhd', p[..., ksel * BLK:], v_own)

    out = lax.map(chunk, jnp.arange(S // QC))
    return out.transpose(1, 0, 2, 3, 4).reshape(B, S, H * Dh)


def causal_depthwise_conv(u, w, b):
    K, C = w.shape
    y = lax.conv_general_dilated(u, w[:, None, :], window_strides=(1,), padding=[(K - 1, 0)],
                                 dimension_numbers=('NWC', 'WIO', 'NWC'), feature_group_count=C)
    return y + b


def ssd_chunked(xs, dt, A, Bm, Cm):
    B, S, H, P = xs.shape
    N = Bm.shape[-1]
    L = SSM_CHUNK
    nc = S // L
    X = (xs.astype(jnp.float32) * dt[..., None]).reshape(B, nc, L, H, P)
    a = (dt * A).reshape(B, nc, L, H).transpose(0, 3, 1, 2)
    a_cs = jnp.cumsum(a, axis=-1)
    Bc = Bm.astype(jnp.float32).reshape(B, nc, L, H, N)
    Cc = Cm.astype(jnp.float32).reshape(B, nc, L, H, N)
    tri = jnp.tril(jnp.ones((L, L), dtype=bool))
    seg = jnp.exp(jnp.where(tri, a_cs[..., :, None] - a_cs[..., None, :], -jnp.inf))
    y_diag = jnp.einsum('bclhn,bcshn,bhcls,bcshp->bclhp', Cc, Bc, seg, X)
    decay_states = jnp.exp(a_cs[..., -1:] - a_cs)
    states = jnp.einsum('bclhn,bhcl,bclhp->bchpn', Bc, decay_states, X)
    chunk_decay = jnp.exp(a_cs[..., -1])

    def step(hstate, inp):
        st, dec = inp
        return hstate * dec[..., None, None] + st, hstate

    h0 = jnp.zeros((B, H, P, N), jnp.float32)
    _, prev = lax.scan(step, h0, (states.transpose(1, 0, 2, 3, 4), chunk_decay.transpose(2, 0, 1)))
    prev = prev.transpose(1, 0, 2, 3, 4)
    y_off = jnp.einsum('bclhn,bchpn,bhcl->bclhp', Cc, prev, jnp.exp(a_cs))
    return (y_diag + y_off).reshape(B, S, H, P)


def mamba2_mixer(z, xbc, dt_raw, conv_w, conv_b, dt_bias, a_log, d_skip, norm_w):
    B, S, _ = z.shape
    xbc = jax.nn.silu(causal_depthwise_conv(xbc, conv_w, conv_b))
    xs, Bm, Cm = jnp.split(xbc, [SSM_D_INNER, SSM_D_INNER + SSM_GROUPS * SSM_STATE], axis=-1)
    xs = xs.reshape(B, S, SSM_HEADS, SSM_HEAD_DIM)
    rep = SSM_HEADS // SSM_GROUPS
    Bm = jnp.repeat(Bm.reshape(B, S, SSM_GROUPS, SSM_STATE), rep, axis=2)
    Cm = jnp.repeat(Cm.reshape(B, S, SSM_GROUPS, SSM_STATE), rep, axis=2)
    dt = jax.nn.softplus(dt_raw.astype(jnp.float32) + dt_bias.astype(jnp.float32))
    A = -jnp.exp(a_log.astype(jnp.float32))
    y = ssd_chunked(xs, dt, A, Bm, Cm) + d_skip.astype(jnp.float32)[:, None] * xs.astype(jnp.float32)
    g = (y.reshape(B, S, SSM_D_INNER) * jax.nn.silu(z.astype(jnp.float32)))
    g = g.reshape(B, S, SSM_GROUPS, SSM_D_INNER // SSM_GROUPS)
    g = g * lax.rsqrt(jnp.mean(g * g, axis=-1, keepdims=True) + NORM_EPS)
    return (g.reshape(B, S, SSM_D_INNER) * norm_w.astype(jnp.float32)).astype(z.dtype)


def apply_rope(x, cos, sin):
    half = x.shape[-1] // 2
    x1, x2 = x[..., :half], x[..., half:]
    return jnp.concatenate([x1 * cos - x2 * sin, x2 * cos + x1 * sin], axis=-1)


def mla_attention(q_lat, kv_lat, k_rope_raw, positions, q_norm, wq_b, kv_norm, wkv_b):
    B, S, _ = q_lat.shape
    H, QB = MLA_HEADS, MLA_Q_BLOCK
    q = (rmsnorm(q_lat, q_norm) @ wq_b).reshape(B, S, H, MLA_NOPE + MLA_ROPE)
    kv = (rmsnorm(kv_lat, kv_norm) @ wkv_b).reshape(B, S, H, MLA_NOPE + MLA_V)
    q_nope, q_rope = q[..., :MLA_NOPE], q[..., MLA_NOPE:]
    k_nope, v = kv[..., :MLA_NOPE], kv[..., MLA_NOPE:]
    half = MLA_ROPE // 2
    inv_freq = ROPE_THETA ** (-jnp.arange(half, dtype=jnp.float32) / half)
    ang = positions.astype(jnp.float32)[..., None] * inv_freq
    cos, sin = jnp.cos(ang).astype(q.dtype), jnp.sin(ang).astype(q.dtype)
    q_rope = apply_rope(q_rope, cos[:, :, None, :], sin[:, :, None, :])
    k_rope = apply_rope(k_rope_raw, cos, sin)
    scale = (MLA_NOPE + MLA_ROPE) ** -0.5
    key_idx = jnp.arange(S)

    def qblock(bi):
        t0 = bi * QB
        qn = lax.dynamic_slice_in_dim(q_nope, t0, QB, axis=1)
        qr = lax.dynamic_slice_in_dim(q_rope, t0, QB, axis=1)
        s = jnp.einsum('bqhd,bkhd->bhqk', qn, k_nope) + jnp.einsum('bqhd,bkd->bhqk', qr, k_rope)
        s = s.astype(jnp.float32) * scale
        causal = key_idx[None, :] <= (t0 + jnp.arange(QB))[:, None]
        p = jax.nn.softmax(jnp.where(causal, s, -jnp.inf), axis=-1).astype(v.dtype)
        return jnp.einsum('bhqk,bkhd->bqhd', p, v)

    out = lax.map(qblock, jnp.arange(S // QB))
    return out.transpose(1, 0, 2, 3, 4).reshape(B, S, H * MLA_V)


def swa_attention(q, k, v, positions, sinks, slopes):
    B, S, H, Dh = q.shape
    KVH = k.shape[2]
    R = H // KVH
    W = SWA_WINDOW
    nb = S // W

    def with_prev(t):
        tb = t.reshape((B, nb, W) + t.shape[2:])
        prev = jnp.pad(tb[:, :-1], ((0, 0), (1, 0)) + ((0, 0),) * (tb.ndim - 2))
        return jnp.concatenate([prev, tb], axis=2)

    qb = q.reshape(B, nb, W, KVH, R, Dh)
    kb, vb, pk = with_prev(k), with_prev(v), with_prev(positions)
    pq = positions.reshape(B, nb, W)
    s = jnp.einsum('bnqgrd,bnkgd->bgrnqk', qb, kb).astype(jnp.float32) * Dh ** -0.5
    dist = jnp.abs(pq[..., :, None] - pk[..., None, :]).astype(jnp.float32)
    s = s - slopes.reshape(KVH, R)[None, :, :, None, None, None] * dist[:, None, None]
    rel = jnp.arange(W)[:, None] + W - jnp.arange(2 * W)[None, :]
    key_glob = (jnp.arange(nb)[:, None] - 1) * W + jnp.arange(2 * W)[None, :]
    allowed = ((rel >= 0) & (rel < W))[None] & (key_glob >= 0)[:, None, :]
    s = jnp.where(allowed, s, -jnp.inf)
    sink = jnp.broadcast_to(sinks.astype(jnp.float32).reshape(KVH, R)[None, :, :, None, None, None],
                            s.shape[:-1] + (1,))
    p = jax.nn.softmax(jnp.concatenate([s, sink], axis=-1), axis=-1)[..., :-1].astype(v.dtype)
    o = jnp.einsum('bgrnqk,bnkgd->bnqgrd', p, vb)
    return o.reshape(B, S, H * Dh)


def hybrid_mixer(h, positions, w_in, conv_w, conv_b, dt_bias, a_log, d_skip, ssm_norm,
                 mla_q_norm, mla_wq_b, mla_kv_norm, mla_wkv_b, swa_sinks, w_branch, w_out,
                 moba_slopes, swa_slopes):
    B, S, _ = h.shape
    proj = h @ w_in
    (m_qkv, z, xbc, dt_raw, q_lat, kv_lat, k_rope, sq, sk, sv, gate_logits) = jnp.split(
        proj, np.cumsum(IN_SIZES)[:-1].tolist(), axis=-1)
    m_qkv = m_qkv.reshape(B, S, 3, MOBA_HEADS, MOBA_HEAD_DIM)
    o_moba = moba_attention(m_qkv[:, :, 0], m_qkv[:, :, 1], m_qkv[:, :, 2], positions, moba_slopes)
    o_ssm = mamba2_mixer(z, xbc, dt_raw, conv_w, conv_b, dt_bias, a_log, d_skip, ssm_norm)
    o_mla = mla_attention(q_lat, kv_lat, k_rope, positions, mla_q_norm, mla_wq_b, mla_kv_norm, mla_wkv_b)
    o_swa = swa_attention(sq.reshape(B, S, SWA_HEADS, SWA_HEAD_DIM),
                          sk.reshape(B, S, SWA_KV_HEADS, SWA_HEAD_DIM),
                          sv.reshape(B, S, SWA_KV_HEADS, SWA_HEAD_DIM),
                          positions, swa_sinks, swa_slopes)
    o_cat = jnp.stack([o_moba, o_ssm, o_mla, o_swa], axis=2)
    y = jnp.einsum('bsrk,rkd->bsrd', o_cat, w_branch)
    gates = jax.nn.sigmoid(gate_logits.reshape(B, S, N_BRANCH, D_MODEL))
    merged = jnp.sum(gates * y, axis=2)
    return merged @ w_out


def hier_moe(h, wg, bg, wr, br, w_gate, w_up, w_down):
    B, S, D = h.shape
    T = B * S
    hf = h.reshape(T, D)
    g_prob = jax.nn.softmax((hf @ wg).astype(jnp.float32) + bg.astype(jnp.float32), axis=-1)
    g_w, g_idx = lax.top_k(g_prob, 1)
    e_logits = ((hf @ wr).astype(jnp.float32) + br.astype(jnp.float32)).reshape(T, N_GROUPS, EXPERTS_PER_GROUP)
    e_logits = jnp.take_along_axis(e_logits, g_idx[:, :, None], axis=1)[:, 0]
    e_w, e_loc = lax.top_k(jax.nn.softmax(e_logits, axis=-1), EXPERT_TOPK)
    e_w = e_w / jnp.sum(e_w, axis=-1, keepdims=True) * g_w
    e_idx = g_idx * EXPERTS_PER_GROUP + e_loc
    M = T * EXPERT_TOPK
    flat_e = e_idx.reshape(M)
    flat_tok = jnp.repeat(jnp.arange(T, dtype=jnp.int32), EXPERT_TOPK)
    flat_w = e_w.reshape(M)
    order = jnp.argsort(flat_e)
    sorted_e = flat_e[order]
    counts = jnp.bincount(flat_e, length=N_EXPERTS)
    start = jnp.cumsum(counts) - counts
    padded = (counts + MOE_BLOCK - 1) // MOE_BLOCK * MOE_BLOCK
    pad_end = jnp.cumsum(padded)
    pad_start = pad_end - padded
    dest = pad_start[sorted_e] + jnp.arange(M, dtype=jnp.int32) - start[sorted_e]
    n_blocks = -(-M // MOE_BLOCK) + N_EXPERTS
    cap = n_blocks * MOE_BLOCK
    slot_tok = jnp.full((cap,), T, jnp.int32).at[dest].set(flat_tok[order])
    slot_w = jnp.zeros((cap,), h.dtype).at[dest].set(flat_w[order].astype(h.dtype))
    block_expert = jnp.minimum(
        jnp.searchsorted(pad_end, jnp.arange(n_blocks, dtype=jnp.int32) * MOE_BLOCK, side='right'),
        N_EXPERTS - 1)
    h_pad = jnp.concatenate([hf, jnp.zeros((1, D), h.dtype)], axis=0)

    def expert_block(args):
        tok, e = args
        xb = h_pad[tok]
        return (jax.nn.silu(xb @ w_gate[e]) * (xb @ w_up[e])) @ w_down[e]

    ys = lax.map(expert_block, (slot_tok.reshape(n_blocks, MOE_BLOCK), block_expert))
    out = jnp.zeros((T + 1, D), h.dtype).at[slot_tok].add(ys.reshape(cap, D) * slot_w[:, None])
    return out[:T].reshape(B, S, D)


def setup_inputs(seed: int = 0) -> dict:
    key = jax.random.key(seed)
    ks = jax.random.split(key, 32)
    f32 = jnp.float32
    D, L = D_MODEL, DEPTH

    def nrm(k, shape, scale):
        return jax.random.normal(k, shape, f32) * scale

    x = nrm(ks[0], (BATCH, SEQ, D), 1.0)
    c = nrm(ks[1], (BATCH, D), 1.0)
    offs = jax.random.randint(ks[2], (BATCH, 1), 0, 1024, dtype=jnp.int32)
    positions = offs + jnp.arange(SEQ, dtype=jnp.int32)[None, :]
    dt0 = jnp.exp(jax.random.uniform(ks[10], (L, SSM_HEADS), f32, math.log(1e-3), math.log(1e-1)))
    return {
        "x": x,
        "c": c,
        "positions": positions,
        "ada_w": nrm(ks[3], (L, D, 6 * D), 0.2 * D ** -0.5),
        "ada_b": nrm(ks[4], (L, 6 * D), 0.02),
        "norm_mix": 1.0 + nrm(ks[5], (L, D), 0.02),
        "norm_ffn": 1.0 + nrm(ks[6], (L, D), 0.02),
        "w_in": nrm(ks[7], (L, D, N_IN), D ** -0.5),
        "conv_w": nrm(ks[8], (L, SSM_CONV, SSM_CONV_CH), SSM_CONV ** -0.5),
        "conv_b": nrm(ks[9], (L, SSM_CONV_CH), 0.02),
        "dt_bias": dt0 + jnp.log(-jnp.expm1(-dt0)),
        "a_log": jnp.log(jax.random.uniform(ks[11], (L, SSM_HEADS), f32, 1.0, 16.0)),
        "d_skip": 1.0 + nrm(ks[12], (L, SSM_HEADS), 0.02),
        "ssm_norm": 1.0 + nrm(ks[13], (L, SSM_D_INNER), 0.02),
        "mla_q_norm": 1.0 + nrm(ks[14], (L, MLA_Q_LORA), 0.02),
        "mla_wq_b": nrm(ks[15], (L, MLA_Q_LORA, MLA_HEADS * (MLA_NOPE + MLA_ROPE)), MLA_Q_LORA ** -0.5),
        "mla_kv_norm": 1.0 + nrm(ks[16], (L, MLA_KV_LORA), 0.02),
        "mla_wkv_b": nrm(ks[17], (L, MLA_KV_LORA, MLA_HEADS * (MLA_NOPE + MLA_V)), MLA_KV_LORA ** -0.5),
        "swa_sinks": nrm(ks[18], (L, SWA_HEADS), 0.5),
        "w_branch": nrm(ks[19], (L, N_BRANCH, BRANCH_W, D), BRANCH_W ** -0.5),
        "w_out": nrm(ks[20], (L, D, D), D ** -0.5),
        "router_group_w": nrm(ks[21], (L, D, N_GROUPS), D ** -0.5),
        "router_group_b": nrm(ks[22], (L, N_GROUPS), 0.01),
        "router_w": nrm(ks[23], (L, D, N_EXPERTS), D ** -0.5),
        "router_b": nrm(ks[24], (L, N_EXPERTS), 0.01),
        "exp_w_gate": nrm(ks[25], (L, N_EXPERTS, D, EXPERT_HIDDEN), D ** -0.5),
        "exp_w_up": nrm(ks[26], (L, N_EXPERTS, D, EXPERT_HIDDEN), D ** -0.5),
        "exp_w_down": nrm(ks[27], (L, N_EXPERTS, EXPERT_HIDDEN, D), EXPERT_HIDDEN ** -0.5),
        "final_norm": 1.0 + nrm(ks[28], (D,), 0.02),
    }


def reference(x, c, positions, ada_w, ada_b, norm_mix, norm_ffn, w_in, conv_w, conv_b, dt_bias,
              a_log, d_skip, ssm_norm, mla_q_norm, mla_wq_b, mla_kv_norm, mla_wkv_b, swa_sinks,
              w_branch, w_out, router_group_w, router_group_b, router_w, router_b,
              exp_w_gate, exp_w_up, exp_w_down, final_norm):
    swa_slopes, moba_slopes = alibi_slopes()
    for l in range(DEPTH):
        mod = (c @ ada_w[l] + ada_b[l])[:, None, :]
        sh1, sc1, g1, sh2, sc2, g2 = jnp.split(mod, 6, axis=-1)
        h = rmsnorm(x, norm_mix[l]) * (1.0 + sc1) + sh1
        x = x + g1 * hybrid_mixer(h, positions, w_in[l], conv_w[l], conv_b[l], dt_bias[l], a_log[l],
                                  d_skip[l], ssm_norm[l], mla_q_norm[l], mla_wq_b[l], mla_kv_norm[l],
                                  mla_wkv_b[l], swa_sinks[l], w_branch[l], w_out[l],
                                  moba_slopes, swa_slopes)
        h = rmsnorm(x, norm_ffn[l]) * (1.0 + sc2) + sh2
        x = x + g2 * hier_moe(h, router_group_w[l], router_group_b[l], router_w[l], router_b[l],
                              exp_w_gate[l], exp_w_up[l], exp_w_down[l])
    return rmsnorm(x, final_norm)
```

```python
import functools
import math

import numpy as np
import jax
import jax.numpy as jnp
from jax import lax
from jax.experimental import pallas as pl
from jax.experimental.pallas import tpu as pltpu

F32 = jnp.float32
BF16 = jnp.bfloat16
HIGHEST = lax.Precision.HIGHEST

MOBA_HEADS = 8
MOBA_HEAD_DIM = 128
MOBA_BLOCK = 256
MOBA_TOPK = 3
SSM_D_INNER = 1024
SSM_HEAD_DIM = 64
SSM_HEADS = SSM_D_INNER // SSM_HEAD_DIM
SSM_GROUPS = 2
SSM_STATE = 128
SSM_CONV = 4
SSM_CHUNK = 128
SSM_BC = 2 * SSM_GROUPS * SSM_STATE
MLA_HEADS = 8
MLA_Q_LORA = 768
MLA_KV_LORA = 512
MLA_NOPE = 128
MLA_ROPE = 64
MLA_V = 128
ROPE_THETA = 10000.0
SWA_HEADS = 16
SWA_KV_HEADS = 2
SWA_HEAD_DIM = 64
SWA_WINDOW = 128
N_BRANCH = 4
BRANCH_W = 1024
N_GROUPS = 4
EXPERTS_PER_GROUP = 8
N_EXPERTS = N_GROUPS * EXPERTS_PER_GROUP
EXPERT_HIDDEN = 512
MOE_BLOCK = 128
NORM_EPS = 1e-6
N_ALIBI = MOBA_HEADS + SWA_HEADS

LANE = 128
VMEM_CAP = 60 * 1024 * 1024
ATTN_BLOCK = 256
MLA_QK = 256


def _alibi_slopes():
    i = np.arange(1, N_ALIBI + 1, dtype=np.float64)
    s = np.exp2(-8.0 * i / N_ALIBI).astype(np.float32)
    return s[:SWA_HEADS], s[SWA_HEADS:]


def _proj_layout(d_model):
    g = N_BRANCH * d_model
    off = dict(gate=0, z=g, xs=g + 1024, sq=g + 2048, mq=g + 3072, mk=g + 4096, mv=g + 5120,
               bc=g + 6144, skx=g + 6656, svx=g + 7168, kvlat=g + 7680, qlat=g + 8192)
    return off, g + 8960


def _params(sem, est_bytes):
    limit = int(min(VMEM_CAP, max(est_bytes, 16 * 1024 * 1024)))
    return pltpu.CompilerParams(dimension_semantics=sem, vmem_limit_bytes=limit)


def _silu(v):
    return v * (1.0 / (1.0 + jnp.exp(-v)))


def _ada_kernel(c_ref, w_ref, b_ref, o_ref):
    w = w_ref[0].astype(BF16)
    acc = jnp.dot(c_ref[...].astype(BF16), w, preferred_element_type=F32)
    o_ref[0] = acc + b_ref[0]


def _ada_mod(c, ada_w, ada_b):
    depth, d, n = ada_w.shape
    b = c.shape[0]
    c8 = jnp.zeros((8, d), F32).at[:b].set(c)
    tn = 512
    out = pl.pallas_call(
        _ada_kernel,
        grid=(depth, n // tn),
        in_specs=[pl.BlockSpec((8, d), lambda l, j: (0, 0)),
                  pl.BlockSpec((1, d, tn), lambda l, j: (l, 0, j)),
                  pl.BlockSpec((1, 1, tn), lambda l, j: (l, 0, j))],
        out_specs=pl.BlockSpec((1, 8, tn), lambda l, j: (l, 0, j)),
        out_shape=jax.ShapeDtypeStruct((depth, 8, n), F32),
        compiler_params=_params(("parallel", "parallel"), 3 * d * tn * 4 + (4 << 20)),
        name="ada_mod",
    )(c8, ada_w, ada_b.reshape(depth, 1, n))
    return out[:, :b].reshape(depth, b, 6, d)


def _norm_mod_kernel(x_ref, w_ref, mod_ref, o_ref, *, sh_idx, sc_idx):
    x = x_ref[0]
    y = x * lax.rsqrt(jnp.mean(x * x, axis=-1, keepdims=True) + NORM_EPS) * w_ref[...]
    m = mod_ref[0]
    o_ref[0] = (y * (1.0 + m[sc_idx:sc_idx + 1]) + m[sh_idx:sh_idx + 1]).astype(o_ref.dtype)


def _norm_mod(x, w, mod, sh_idx, sc_idx, tm=256):
    b, s, d = x.shape
    return pl.pallas_call(
        functools.partial(_norm_mod_kernel, sh_idx=sh_idx, sc_idx=sc_idx),
        grid=(b, s // tm),
        in_specs=[pl.BlockSpec((1, tm, d), lambda i, j: (i, j, 0)),
                  pl.BlockSpec((1, d), lambda i, j: (0, 0)),
                  pl.BlockSpec((1, 6, d), lambda i, j: (i, 0, 0))],
        out_specs=pl.BlockSpec((1, tm, d), lambda i, j: (i, j, 0)),
        out_shape=jax.ShapeDtypeStruct((b, s, d), BF16),
        compiler_params=_params(("parallel", "parallel"), 6 * tm * d * 4 + (4 << 20)),
        name="norm_mod",
    )(x, w.reshape(1, d), mod)


def _mm_kernel(a_ref, w_ref, o_ref):
    o_ref[...] = jnp.dot(a_ref[...], w_ref[...], preferred_element_type=F32).astype(o_ref.dtype)


def _matmul(a, w, out_dtype, tm, tn, name):
    m, k = a.shape
    n = w.shape[1]
    est = 2 * (tm * k * 2 + k * tn * 2 + tm * tn * 4) + tm * tn * 4 + (4 << 20)
    return pl.pallas_call(
        _mm_kernel,
        grid=(m // tm, n // tn),
        in_specs=[pl.BlockSpec((tm, k), lambda i, j: (i, 0)),
                  pl.BlockSpec((k, tn), lambda i, j: (0, j))],
        out_specs=pl.BlockSpec((tm, tn), lambda i, j: (i, j)),
        out_shape=jax.ShapeDtypeStruct((m, n), out_dtype),
        compiler_params=_params(("parallel", "parallel"), est),
        name=name,
    )(a, w)


def _rope_kernel(pos_ref, freq_ref, cos_ref, sin_ref):
    ang = pos_ref[...] * freq_ref[...]
    lane = lax.broadcasted_iota(jnp.int32, ang.shape, 1)
    live = lane < MLA_ROPE
    cos_ref[...] = jnp.where(live, jnp.cos(ang), 0.0)
    sin_ref[...] = jnp.where(live, jnp.sin(ang), 0.0)


def _rope_tables(pos_col):
    t = pos_col.shape[0]
    half = MLA_ROPE // 2
    inv = ROPE_THETA ** (-np.arange(half, dtype=np.float32) / half)
    freq = np.zeros((1, LANE), np.float32)
    freq[0, :half] = inv
    freq[0, half:2 * half] = inv
    tm = 512
    return pl.pallas_call(
        _rope_kernel,
        grid=(t // tm,),
        in_specs=[pl.BlockSpec((tm, 1), lambda i: (i, 0)),
                  pl.BlockSpec((1, LANE), lambda i: (0, 0))],
        out_specs=[pl.BlockSpec((tm, LANE), lambda i: (i, 0))] * 2,
        out_shape=[jax.ShapeDtypeStruct((t, LANE), F32)] * 2,
        compiler_params=_params(("parallel",), 16 << 20),
        name="rope_tables",
    )(pos_col, jnp.asarray(freq))


def _mla_q_kernel(x_ref, nw_ref, w_ref, cos_ref, sin_ref, o_ref, *, scale):
    x = x_ref[...].astype(F32)
    y = x * lax.rsqrt(jnp.mean(x * x, axis=-1, keepdims=True) + NORM_EPS) * nw_ref[...]
    r = jnp.dot(y.astype(BF16), w_ref[...], preferred_element_type=F32)
    cos = cos_ref[...]
    sin = sin_ref[...]
    for h in range(MLA_HEADS):
        base = h * 3 * LANE
        nope = r[:, base:base + LANE]
        rope = r[:, base + LANE:base + 2 * LANE]
        rot = r[:, base + 2 * LANE:base + 3 * LANE]
        o_ref[:, h * MLA_QK:h * MLA_QK + LANE] = (nope * scale).astype(o_ref.dtype)
        o_ref[:, h * MLA_QK + LANE:(h + 1) * MLA_QK] = (
            (rope * cos + rot * sin) * scale).astype(o_ref.dtype)


def _mla_kv_kernel(x_ref, nw_ref, w_ref, kr_ref, krot_ref, cos_ref, sin_ref, k_ref, v_ref):
    x = x_ref[...].astype(F32)
    y = x * lax.rsqrt(jnp.mean(x * x, axis=-1, keepdims=True) + NORM_EPS) * nw_ref[...]
    r = jnp.dot(y.astype(BF16), w_ref[...], preferred_element_type=F32)
    kr = (kr_ref[...] * cos_ref[...] + krot_ref[...] * sin_ref[...]).astype(k_ref.dtype)
    for h in range(MLA_HEADS):
        base = h * (MLA_NOPE + MLA_V)
        k_ref[:, h * MLA_QK:h * MLA_QK + LANE] = r[:, base:base + MLA_NOPE].astype(k_ref.dtype)
        k_ref[:, h * MLA_QK + LANE:(h + 1) * MLA_QK] = kr
        v_ref[:, h * MLA_V:(h + 1) * MLA_V] = r[:, base + MLA_NOPE:base + MLA_NOPE + MLA_V].astype(
            v_ref.dtype)


def _mla_project(proj, tail, off, q_norm, wq_ext, kv_norm, wkv, cos_t, sin_t, tm=512):
    t = proj.shape[0]
    scale = float((MLA_NOPE + MLA_ROPE) ** -0.5)
    q = pl.pallas_call(
        functools.partial(_mla_q_kernel, scale=scale),
        grid=(t // tm,),
        in_specs=[pl.BlockSpec((tm, MLA_Q_LORA), lambda i: (i, off["qlat"] // MLA_Q_LORA)),
                  pl.BlockSpec((1, MLA_Q_LORA), lambda i: (0, 0)),
                  pl.BlockSpec(wq_ext.shape, lambda i: (0, 0)),
                  pl.BlockSpec((tm, LANE), lambda i: (i, 0)),
                  pl.BlockSpec((tm, LANE), lambda i: (i, 0))],
        out_specs=pl.BlockSpec((tm, MLA_HEADS * MLA_QK), lambda i: (i, 0)),
        out_shape=jax.ShapeDtypeStruct((t, MLA_HEADS * MLA_QK), BF16),
        compiler_params=_params(("parallel",), 40 << 20),
        name="mla_q_proj",
    )(proj, q_norm.reshape(1, -1), wq_ext, cos_t, sin_t)
    k, v = pl.pallas_call(
        _mla_kv_kernel,
        grid=(t // tm,),
        in_specs=[pl.BlockSpec((tm, MLA_KV_LORA), lambda i: (i, off["kvlat"] // MLA_KV_LORA)),
                  pl.BlockSpec((1, MLA_KV_LORA), lambda i: (0, 0)),
                  pl.BlockSpec(wkv.shape, lambda i: (0, 0)),
                  pl.BlockSpec((tm, LANE), lambda i: (i, 0)),
                  pl.BlockSpec((tm, LANE), lambda i: (i, 1)),
                  pl.BlockSpec((tm, LANE), lambda i: (i, 0)),
                  pl.BlockSpec((tm, LANE), lambda i: (i, 0))],
        out_specs=[pl.BlockSpec((tm, MLA_HEADS * MLA_QK), lambda i: (i, 0)),
                   pl.BlockSpec((tm, MLA_HEADS * MLA_V), lambda i: (i, 0))],
        out_shape=[jax.ShapeDtypeStruct((t, MLA_HEADS * MLA_QK), BF16),
                   jax.ShapeDtypeStruct((t, MLA_HEADS * MLA_V), BF16)],
        compiler_params=_params(("parallel",), 40 << 20),
        name="mla_kv_proj",
    )(proj, kv_norm.reshape(1, -1), wkv, tail, tail, cos_t, sin_t)
    return q, k, v


def _flash_kernel(*refs, moba, q_scale):
    if moba:
        slope_ref, q_ref, k_ref, v_ref, posk_ref, posq_ref, o_ref, vt_scr, kmean_scr, sel_scr = refs
    else:
        q_ref, k_ref, v_ref, o_ref, vt_scr = refs
    blk = ATTN_BLOCK
    nblk = k_ref.shape[1] // blk
    tq = q_ref.shape[1]
    qi = pl.program_id(2)

    @pl.when(qi == 0)
    def _():
        for c in range(nblk):
            vblk = v_ref[0, c * blk:(c + 1) * blk, :].astype(F32)
            vt_scr[c] = vblk.T.astype(BF16)
            if moba:
                kblk = k_ref[0, c * blk:(c + 1) * blk, :].astype(F32)
                kmean_scr[c:c + 1, :] = jnp.mean(kblk, axis=0, keepdims=True)

    q_raw = q_ref[0]
    if q_scale is None:
        q = q_raw
    else:
        q = (q_raw.astype(F32) * q_scale).astype(BF16)

    if moba:
        slope = slope_ref[pl.program_id(1)]
        pq = posq_ref[0]
        gate = lax.dot_general(kmean_scr[...], q_raw.astype(F32), (((1,), (1,)), ((), ())),
                               precision=HIGHEST, preferred_element_type=F32)
        n_iota = lax.broadcasted_iota(jnp.int32, gate.shape, 0)
        beaten = jnp.zeros(gate.shape, F32)
        for m in range(nblk):
            gm = gate[m:m + 1, :]
            wins = jnp.where(gm > gate, 1.0, jnp.where((gm == gate), jnp.where(m < n_iota, 1.0, 0.0), 0.0))
            beaten = beaten + jnp.where(m < qi, wins, 0.0)
        sel_scr[...] = jnp.where(beaten < MOBA_TOPK, jnp.where(n_iota < qi, 1.0, 0.0), 0.0)

    def scores(n):
        start = pl.multiple_of(n * blk, blk)
        kb = k_ref[0, pl.ds(start, blk), :]
        st = lax.dot_general(kb, q, (((1,), (1,)), ((), ())), preferred_element_type=F32)
        if moba:
            pk = posk_ref[0, pl.ds(start, blk), :]
            st = st - slope * jnp.abs(pq - pk)
        return st

    st = scores(qi)
    key_i = lax.broadcasted_iota(jnp.int32, st.shape, 0)
    qry_i = lax.broadcasted_iota(jnp.int32, st.shape, 1)
    st = jnp.where(key_i <= qry_i, st, -jnp.inf)
    m0 = jnp.max(st, axis=0, keepdims=True)
    p = jnp.exp(st - m0)
    l0 = jnp.sum(p, axis=0, keepdims=True)
    acc0 = jnp.dot(vt_scr[qi], p.astype(BF16), preferred_element_type=F32)

    def body(n, carry):
        m_prev, l_prev, acc = carry
        s_n = scores(n)
        if moba:
            s_n = jnp.where(sel_scr[pl.ds(n, 1), :] > 0.0, s_n, -jnp.inf)
        m_new = jnp.maximum(m_prev, jnp.max(s_n, axis=0, keepdims=True))
        alpha = jnp.exp(m_prev - m_new)
        p_n = jnp.exp(s_n - m_new)
        l_new = alpha * l_prev + jnp.sum(p_n, axis=0, keepdims=True)
        acc_new = alpha * acc + jnp.dot(vt_scr[n], p_n.astype(BF16), preferred_element_type=F32)
        return m_new, l_new, acc_new

    _, l_fin, acc = lax.fori_loop(0, qi, body, (m0, l0, acc0))
    o_ref[0] = (acc * (1.0 / l_fin)).T.astype(o_ref.dtype)


def _flash(q, k, v, *, heads, dq, dv, q_off, k_off, v_off, moba=False, q_scale=None,
           slopes=None, pos_col=None, pos_row=None, name="flash"):
    b, s, _ = q.shape
    blk = ATTN_BLOCK
    nblk = s // blk
    in_specs = [pl.BlockSpec((1, blk, dq), lambda bi, h, i: (bi, i, q_off + h)),
                pl.BlockSpec((1, s, dq), lambda bi, h, i: (bi, 0, k_off + h)),
                pl.BlockSpec((1, s, dv), lambda bi, h, i: (bi, 0, v_off + h))]
    args = [q, k, v]
    scratch = [pltpu.VMEM((nblk, dv, blk), BF16)]
    if moba:
        in_specs = [pl.BlockSpec(memory_space=pltpu.SMEM)] + in_specs + [
            pl.BlockSpec((1, s, 1), lambda bi, h, i: (bi, 0, 0)),
            pl.BlockSpec((1, 1, blk), lambda bi, h, i: (bi, 0, i))]
        args = [slopes] + args + [pos_col, pos_row]
        scratch += [pltpu.VMEM((nblk, dq), F32), pltpu.VMEM((nblk, blk), F32)]
    est = 2 * (s * dq * 2 + s * dv * 2) + s * dv * 2 + s * LANE * 4 * 2 + (12 << 20)
    return pl.pallas_call(
        functools.partial(_flash_kernel, moba=moba, q_scale=q_scale),
        grid=(b, heads, nblk),
        in_specs=in_specs,
        out_specs=pl.BlockSpec((1, blk, dv), lambda bi, h, i: (bi, i, h)),
        out_shape=jax.ShapeDtypeStruct((b, s, heads * dv), BF16),
        scratch_shapes=scratch,
        compiler_params=_params(("parallel", "parallel", "arbitrary"), est),
        name=name,
    )(*args)


def _swa_kernel(sink_ref, q_ref, kp_ref, kc_ref, vp_ref, vc_ref, pq_ref, pkp_ref, pkc_ref, o_ref,
                *, slopes):
    w = SWA_WINDOW
    n = pl.program_id(1)
    kx = jnp.concatenate([kp_ref[0], kc_ref[0]], axis=0)
    vx = jnp.concatenate([vp_ref[0], vc_ref[0]], axis=0)
    pk = jnp.concatenate([pkp_ref[0], pkc_ref[0]], axis=1)
    dist = jnp.abs(pq_ref[0] - pk)
    qi = lax.broadcasted_iota(jnp.int32, dist.shape, 0)
    kk = lax.broadcasted_iota(jnp.int32, dist.shape, 1)
    allowed = jnp.where(kk > qi, jnp.where(kk <= qi + w, 1.0, 0.0), 0.0)
    allowed = jnp.where(kk >= w, allowed, jnp.where(n > 0, allowed, 0.0)) > 0.0
    scale = float(SWA_HEAD_DIM ** -0.5)
    rep = SWA_HEADS // SWA_KV_HEADS
    for pair in range(SWA_HEADS // 2):
        qp = q_ref[0, :, pair * LANE:(pair + 1) * LANE]
        acc = jnp.zeros((w, LANE), F32)
        for half in range(2):
            h = 2 * pair + half
            col = (2 * (h // rep) + half) * LANE
            s = lax.dot_general(qp, kx[:, col:col + LANE], (((1,), (1,)), ((), ())),
                                preferred_element_type=F32)
            s = s * scale - float(slopes[h]) * dist
            s = jnp.where(allowed, s, -jnp.inf)
            sink = sink_ref[h]
            m = jnp.maximum(jnp.max(s, axis=1, keepdims=True), sink)
            p = jnp.exp(s - m)
            denom = jnp.sum(p, axis=1, keepdims=True) + jnp.exp(sink - m)
            p = p * (1.0 / denom)
            acc = acc + jnp.dot(p.astype(BF16), vx[:, col:col + LANE], preferred_element_type=F32)
        o_ref[0, :, pair * LANE:(pair + 1) * LANE] = acc.astype(o_ref.dtype)


def _swa(proj3, off, sinks, pos_col, pos_row, slopes):
    b, s, _ = proj3.shape
    w = SWA_WINDOW
    qw = SWA_HEADS * SWA_HEAD_DIM
    kw = 4 * LANE
    prev = lambda j: jnp.maximum(j - 1, 0)
    return pl.pallas_call(
        functools.partial(_swa_kernel, slopes=tuple(float(v) for v in slopes)),
        grid=(b, s // w),
        in_specs=[pl.BlockSpec(memory_space=pltpu.SMEM),
                  pl.BlockSpec((1, w, qw), lambda i, j: (i, j, off["sq"] // qw)),
                  pl.BlockSpec((1, w, kw), lambda i, j: (i, prev(j), off["skx"] // kw)),
                  pl.BlockSpec((1, w, kw), lambda i, j: (i, j, off["skx"] // kw)),
                  pl.BlockSpec((1, w, kw), lambda i, j: (i, prev(j), off["svx"] // kw)),
                  pl.BlockSpec((1, w, kw), lambda i, j: (i, j, off["svx"] // kw)),
                  pl.BlockSpec((1, w, 1), lambda i, j: (i, j, 0)),
                  pl.BlockSpec((1, 1, w), lambda i, j: (i, 0, prev(j))),
                  pl.BlockSpec((1, 1, w), lambda i, j: (i, 0, j))],
        out_specs=pl.BlockSpec((1, w, qw), lambda i, j: (i, j, 0)),
        out_shape=jax.ShapeDtypeStruct((b, s, qw), BF16),
        compiler_params=_params(("parallel", "parallel"), 24 << 20),
        name="swa_attention",
    )(sinks, proj3, proj3, proj3, proj3, proj3, pos_col, pos_row, pos_row)


def _ssd_kernel(xs_ref, bc_ref, z_ref, dt_ref, cwx_ref, cwb_ref, cbx_ref, cbb_ref, dtb_ref,
                alog_ref, dsk_ref, nw_ref, exp_ref, o_ref, padx_scr, padb_scr, st_scr):
    ch = SSM_CHUNK
    c = pl.program_id(1)

    @pl.when(c == 0)
    def _():
        padx_scr[0:8, :] = jnp.zeros((8, padx_scr.shape[1]), F32)
        padb_scr[0:8, :] = jnp.zeros((8, padb_scr.shape[1]), F32)
        st_scr[...] = jnp.zeros(st_scr.shape, F32)

    padx_scr[8:8 + ch, :] = xs_ref[0].astype(F32)
    padb_scr[8:8 + ch, :] = bc_ref[0].astype(F32)

    def conv(pad_scr, w_ref, b_ref):
        acc = b_ref[...] + w_ref[0:1, :] * pad_scr[5:5 + ch, :]
        for k in range(1, SSM_CONV):
            acc = acc + w_ref[k:k + 1, :] * pad_scr[5 + k:5 + k + ch, :]
        return _silu(acc)

    xs = conv(padx_scr, cwx_ref, cbx_ref)
    bcv = conv(padb_scr, cwb_ref, cbb_ref)
    padx_scr[0:8, :] = padx_scr[ch:ch + 8, :]
    padb_scr[0:8, :] = padb_scr[ch:ch + 8, :]

    gn = SSM_GROUPS * SSM_STATE
    dtr = dt_ref[0] + dtb_ref[...]
    dt = jnp.maximum(dtr, 0.0) + jnp.log(1.0 + jnp.exp(-jnp.abs(dtr)))
    a = dt * (-jnp.exp(alog_ref[...]))
    expand = exp_ref[...]
    row = lax.broadcasted_iota(jnp.int32, (ch, ch), 0)
    colm = lax.broadcasted_iota(jnp.int32, (ch, ch), 1)
    tril = row >= colm
    tri = jnp.where(tril, 1.0, 0.0)
    dt_e = jnp.dot(dt, expand, precision=HIGHEST, preferred_element_type=F32)
    acs = jnp.dot(tri, a, precision=HIGHEST, preferred_element_type=F32)
    acs_e = jnp.dot(acs, expand, precision=HIGHEST, preferred_element_type=F32)
    acs_t = acs.T
    x_dt = xs * dt_e
    last = acs_e[ch - 1:ch, :]
    x_dec = (x_dt * jnp.exp(last - acs_e)).astype(BF16)
    lane = lax.broadcasted_iota(jnp.int32, (ch, LANE), 1)
    lo = lane < SSM_HEAD_DIM
    half_w = SSM_D_INNER // SSM_GROUPS
    heads_per_group = SSM_HEADS // SSM_GROUPS
    ydiag = []
    yoff = []
    for g in range(SSM_GROUPS):
        bg = bcv[:, g * SSM_STATE:(g + 1) * SSM_STATE]
        cg = bcv[:, gn + g * SSM_STATE:gn + (g + 1) * SSM_STATE].astype(BF16)
        gmat = lax.dot_general(cg, bg.astype(BF16), (((1,), (1,)), ((), ())),
                               preferred_element_type=F32)
        st_g = st_scr[:, g * half_w:(g + 1) * half_w]
        yoff.append(jnp.dot(cg, st_g.astype(BF16), preferred_element_type=F32))
        for j in range(heads_per_group // 2):
            pair = g * (heads_per_group // 2) + j
            xp = x_dt[:, pair * LANE:(pair + 1) * LANE]
            acc = jnp.zeros((ch, LANE), F32)
            for half in range(2):
                h = 2 * pair + half
                seg = jnp.exp(jnp.where(tril, acs[:, h:h + 1] - acs_t[h:h + 1, :], -jnp.inf))
                mmat = (gmat * seg).astype(BF16)
                xh = (jnp.where(lo, xp, 0.0) if half == 0 else jnp.where(lo, 0.0, xp)).astype(BF16)
                acc = acc + jnp.dot(mmat, xh, preferred_element_type=F32)
            ydiag.append(acc)
        upd = jnp.dot(bg.T.astype(BF16), x_dec[:, g * half_w:(g + 1) * half_w],
                      preferred_element_type=F32)
        st_scr[:, g * half_w:(g + 1) * half_w] = st_g * jnp.exp(last[:, g * half_w:(g + 1) * half_w]) + upd
    y = (jnp.concatenate(ydiag, axis=1) + jnp.concatenate(yoff, axis=1) * jnp.exp(acs_e)
         + dsk_ref[...] * xs)
    gz = y * _silu(z_ref[0].astype(F32))
    outs = []
    for g in range(SSM_GROUPS):
        gg = gz[:, g * half_w:(g + 1) * half_w]
        outs.append(gg * lax.rsqrt(jnp.mean(gg * gg, axis=-1, keepdims=True) + NORM_EPS))
    o_ref[0] = (jnp.concatenate(outs, axis=1) * nw_ref[...]).astype(o_ref.dtype)


def _ssd(proj3, tail3, off, conv_w, conv_b, dt_bias, a_log, d_skip, norm_w):
    b, s, _ = proj3.shape
    ch = SSM_CHUNK
    di = SSM_D_INNER
    pad16 = lambda v: jnp.zeros((1, LANE), F32).at[0, :SSM_HEADS].set(v)
    expand = np.zeros((LANE, di), np.float32)
    for h in range(SSM_HEADS):
        expand[h, h * SSM_HEAD_DIM:(h + 1) * SSM_HEAD_DIM] = 1.0
    full = lambda shape: pl.BlockSpec(shape, lambda i, j: (0,) * len(shape))
    return pl.pallas_call(
        _ssd_kernel,
        grid=(b, s // ch),
        in_specs=[pl.BlockSpec((1, ch, di), lambda i, j: (i, j, off["xs"] // di)),
                  pl.BlockSpec((1, ch, SSM_BC), lambda i, j: (i, j, off["bc"] // SSM_BC)),
                  pl.BlockSpec((1, ch, di), lambda i, j: (i, j, off["z"] // di)),
                  pl.BlockSpec((1, ch, LANE), lambda i, j: (i, j, 2)),
                  full((SSM_CONV, di)), full((SSM_CONV, SSM_BC)), full((1, di)), full((1, SSM_BC)),
                  full((1, LANE)), full((1, LANE)), full((1, di)), full((1, di)), full((LANE, di))],
        out_specs=pl.BlockSpec((1, ch, di), lambda i, j: (i, j, 0)),
        out_shape=jax.ShapeDtypeStruct((b, s, di), BF16),
        scratch_shapes=[pltpu.VMEM((ch + 8, di), F32), pltpu.VMEM((ch + 8, SSM_BC), F32),
                        pltpu.VMEM((SSM_STATE, di), F32)],
        compiler_params=_params(("parallel", "arbitrary"), 32 << 20),
        name="ssd_mixer",
    )(proj3, proj3, proj3, tail3, conv_w[:, :di], conv_w[:, di:], conv_b[:di].reshape(1, di),
      conv_b[di:].reshape(1, SSM_BC), pad16(dt_bias), pad16(a_log),
      jnp.repeat(d_skip, SSM_HEAD_DIM).reshape(1, di), norm_w.reshape(1, di), jnp.asarray(expand))


def _merge_kernel(o0_ref, o1_ref, o2_ref, o3_ref, w_ref, g0_ref, g1_ref, g2_ref, g3_ref, out_ref):
    acc = None
    for r, (o_ref, g_ref) in enumerate(((o0_ref, g0_ref), (o1_ref, g1_ref), (o2_ref, g2_ref),
                                        (o3_ref, g3_ref))):
        y = jnp.dot(o_ref[...], w_ref[r], preferred_element_type=F32)
        gate = 1.0 / (1.0 + jnp.exp(-g_ref[...].astype(F32)))
        acc = gate * y if acc is None else acc + gate * y
    out_ref[...] = acc.astype(out_ref.dtype)


def _merge(branches, w_branch, proj, d, tm=512, tn=512):
    t = proj.shape[0]
    nj = d // tn
    o_spec = pl.BlockSpec((tm, BRANCH_W), lambda i, j: (i, 0))
    g_specs = [pl.BlockSpec((tm, tn), functools.partial(lambda i, j, r: (i, r * nj + j), r=r))
               for r in range(N_BRANCH)]
    est = 2 * (4 * tm * BRANCH_W * 2 + 4 * BRANCH_W * tn * 2 + 4 * tm * tn * 2 + tm * tn * 2) + (8 << 20)
    return pl.pallas_call(
        _merge_kernel,
        grid=(t // tm, nj),
        in_specs=[o_spec] * 4 + [pl.BlockSpec((N_BRANCH, BRANCH_W, tn), lambda i, j: (0, 0, j))] + g_specs,
        out_specs=pl.BlockSpec((tm, tn), lambda i, j: (i, j)),
        out_shape=jax.ShapeDtypeStruct((t, d), BF16),
        compiler_params=_params(("parallel", "parallel"), est),
        name="branch_merge",
    )(*branches, w_branch, proj, proj, proj, proj)


def _out_proj_kernel(a_ref, w_ref, x_ref, mod_ref, o_ref, *, g_idx):
    y = jnp.dot(a_ref[...], w_ref[...], preferred_element_type=F32)
    o_ref[...] = x_ref[...] + mod_ref[0, g_idx:g_idx + 1, :] * y


def _out_proj(merged, w_out, x2, mod, seq, g_idx, tm=512, tn=512):
    t, d = x2.shape
    per_b = seq // tm
    est = 2 * (tm * d * 2 + d * tn * 2 + 2 * tm * tn * 4) + tm * tn * 4 + (4 << 20)
    return pl.pallas_call(
        functools.partial(_out_proj_kernel, g_idx=g_idx),
        grid=(t // tm, d // tn),
        in_specs=[pl.BlockSpec((tm, d), lambda i, j: (i, 0)),
                  pl.BlockSpec((d, tn), lambda i, j: (0, j)),
                  pl.BlockSpec((tm, tn), lambda i, j: (i, j)),
                  pl.BlockSpec((1, 6, tn), lambda i, j: (i // per_b, 0, j))],
        out_specs=pl.BlockSpec((tm, tn), lambda i, j: (i, j)),
        out_shape=jax.ShapeDtypeStruct((t, d), F32),
        compiler_params=_params(("parallel", "parallel"), est),
        name="out_proj",
    )(merged, w_out, x2, mod)


def _router_kernel(x_ref, w_ref, mod_ref, rw_ref, rb_ref, h_ref, ri_ref, rwt_ref, cnt_ref, carry_scr,
                   *, sh_idx, sc_idx):
    step = pl.program_id(0) * pl.num_programs(1) + pl.program_id(1)

    @pl.when(step == 0)
    def _():
        carry_scr[...] = jnp.zeros(carry_scr.shape, F32)

    x = x_ref[0]
    y = x * lax.rsqrt(jnp.mean(x * x, axis=-1, keepdims=True) + NORM_EPS) * w_ref[...]
    m = mod_ref[0]
    h = y * (1.0 + m[sc_idx:sc_idx + 1]) + m[sh_idx:sh_idx + 1]
    h_ref[0] = h
    logits = jnp.dot(h, rw_ref[...], precision=HIGHEST, preferred_element_type=F32) + rb_ref[...]
    tm = logits.shape[0]
    lane = lax.broadcasted_iota(jnp.int32, logits.shape, 1)
    big = jnp.int32(4 * LANE)
    neg = -jnp.inf

    def first_argmax(vals):
        mx = jnp.max(vals, axis=1, keepdims=True)
        idx = jnp.min(jnp.where(vals == mx, lane, big), axis=1, keepdims=True)
        return mx, idx

    lg = jnp.where(lane < N_GROUPS, logits, neg)
    gmax, gidx = first_argmax(lg)
    g_w = 1.0 / jnp.sum(jnp.exp(lg - gmax), axis=1, keepdims=True)
    lo = N_GROUPS + gidx * EXPERTS_PER_GROUP
    in_group = jnp.where(lane >= lo, jnp.where(lane < lo + EXPERTS_PER_GROUP, 1.0, 0.0), 0.0) > 0.0
    le = jnp.where(in_group, logits, neg)
    m1, i1 = first_argmax(le)
    le2 = jnp.where(lane == i1, neg, le)
    m2, i2 = first_argmax(le2)
    denom = jnp.sum(jnp.exp(le - m1), axis=1, keepdims=True)
    p1 = 1.0 / denom
    p2 = jnp.exp(m2 - m1) / denom
    w1 = p1 / (p1 + p2) * g_w
    w2 = p2 / (p1 + p2) * g_w
    e1 = i1 - N_GROUPS
    e2 = i2 - N_GROUPS

    oh1 = jnp.where(lane == e1, 1.0, 0.0)
    oh2 = jnp.where(lane == e2, 1.0, 0.0)
    both = oh1 + oh2
    r_i = lax.broadcasted_iota(jnp.int32, (tm, tm), 0)
    c_i = lax.broadcasted_iota(jnp.int32, (tm, tm), 1)
    strict = jnp.where(r_i > c_i, 1.0, 0.0).astype(BF16)
    before = jnp.dot(strict, both.astype(BF16), preferred_element_type=F32) + carry_scr[...]
    rank1 = jnp.sum(before * oh1, axis=1, keepdims=True).astype(jnp.int32)
    rank2 = jnp.sum(before * oh2, axis=1, keepdims=True).astype(jnp.int32)
    carry_scr[...] = carry_scr[...] + jnp.sum(both, axis=0, keepdims=True)
    cnt_ref[...] = carry_scr[...]
    ri_ref[0] = jnp.where(lane == 0, e1, jnp.where(lane == 1, e2, jnp.where(
        lane == 2, rank1, jnp.where(lane == 3, rank2, 0))))
    rwt_ref[0] = jnp.where(lane == 0, w1, jnp.where(lane == 1, w2, 0.0))


def _router(x, w, mod, wg, bg, wr, br, sh_idx, sc_idx, tm=256):
    b, s, d = x.shape
    rw = jnp.zeros((d, LANE), F32).at[:, :N_GROUPS].set(wg).at[:, N_GROUPS:N_GROUPS + N_EXPERTS].set(wr)
    rb = jnp.zeros((1, LANE), F32).at[0, :N_GROUPS].set(bg).at[0, N_GROUPS:N_GROUPS + N_EXPERTS].set(br)
    return pl.pallas_call(
        functools.partial(_router_kernel, sh_idx=sh_idx, sc_idx=sc_idx),
        grid=(b, s // tm),
        in_specs=[pl.BlockSpec((1, tm, d), lambda i, j: (i, j, 0)),
                  pl.BlockSpec((1, d), lambda i, j: (0, 0)),
                  pl.BlockSpec((1, 6, d), lambda i, j: (i, 0, 0)),
                  pl.BlockSpec((d, LANE), lambda i, j: (0, 0)),
                  pl.BlockSpec((1, LANE), lambda i, j: (0, 0))],
        out_specs=[pl.BlockSpec((1, tm, d), lambda i, j: (i, j, 0)),
                   pl.BlockSpec((1, tm, LANE), lambda i, j: (i, j, 0)),
                   pl.BlockSpec((1, tm, LANE), lambda i, j: (i, j, 0)),
                   pl.BlockSpec((1, LANE), lambda i, j: (0, 0))],
        out_shape=[jax.ShapeDtypeStruct((b, s, d), F32),
                   jax.ShapeDtypeStruct((b, s, LANE), jnp.int32),
                   jax.ShapeDtypeStruct((b, s, LANE), F32),
                   jax.ShapeDtypeStruct((1, LANE), F32)],
        scratch_shapes=[pltpu.VMEM((1, LANE), F32)],
        compiler_params=_params(("arbitrary", "arbitrary"), 8 * tm * d * 4 + d * LANE * 8 + (8 << 20)),
        name="moe_router",
    )(x, w.reshape(1, d), mod, rw, rb)


def _slot_kernel(d1_ref, d2_ref, o_ref):
    def clear(i, carry):
        o_ref[i] = 0
        return carry

    lax.fori_loop(0, o_ref.shape[0], clear, 0)

    def place(t, carry):
        o_ref[d1_ref[t]] = t
        o_ref[d2_ref[t]] = t
        return carry

    lax.fori_loop(0, d1_ref.shape[0], place, 0)


def _slot_tokens(dest1, dest2, cap):
    smem = pl.BlockSpec(memory_space=pltpu.SMEM)
    return pl.pallas_call(
        _slot_kernel,
        in_specs=[smem, smem],
        out_specs=smem,
        out_shape=jax.ShapeDtypeStruct((cap,), jnp.int32),
        name="moe_slot_tokens",
    )(dest1, dest2)


def _expert_kernel(tok_ref, be_ref, nu_ref, h_hbm, wg_ref, wu_ref, wd_ref, o_ref, x_scr, sem):
    blk = pl.program_id(0)
    rows = x_scr.shape[0]

    def row_copy(i):
        tok = tok_ref[blk * rows + i]
        return pltpu.make_async_copy(h_hbm.at[pl.ds(tok, 1)], x_scr.at[pl.ds(i, 1)], sem)

    @pl.when(blk < nu_ref[0])
    def _():
        def start(i, carry):
            row_copy(i).start()
            return carry

        lax.fori_loop(0, rows, start, 0)

        def wait(i, carry):
            row_copy(i).wait()
            return carry

        lax.fori_loop(0, rows, wait, 0)
        xb = x_scr[...].astype(BF16)
        gate = jnp.dot(xb, wg_ref[0], preferred_element_type=F32)
        up = jnp.dot(xb, wu_ref[0], preferred_element_type=F32)
        act = (_silu(gate) * up).astype(BF16)
        o_ref[...] = jnp.dot(act, wd_ref[0], preferred_element_type=F32)

    @pl.when(blk >= nu_ref[0])
    def _():
        o_ref[...] = jnp.zeros(o_ref.shape, o_ref.dtype)


def _experts(slot_tok, block_expert, n_used, h2, wg, wu, wd):
    t, d = h2.shape
    n_blocks = block_expert.shape[0]
    hid = wg.shape[2]
    est = 2 * (3 * d * hid * 2 + MOE_BLOCK * d * 4) + MOE_BLOCK * d * 4 * 3 + (6 << 20)
    return pl.pallas_call(
        _expert_kernel,
        grid_spec=pltpu.PrefetchScalarGridSpec(
            num_scalar_prefetch=3,
            grid=(n_blocks,),
            in_specs=[pl.BlockSpec(memory_space=pl.ANY),
                      pl.BlockSpec((1, d, hid), lambda i, tok, be, nu: (be[i], 0, 0)),
                      pl.BlockSpec((1, d, hid), lambda i, tok, be, nu: (be[i], 0, 0)),
                      pl.BlockSpec((1, hid, d), lambda i, tok, be, nu: (be[i], 0, 0))],
            out_specs=pl.BlockSpec((MOE_BLOCK, d), lambda i, tok, be, nu: (i, 0)),
            scratch_shapes=[pltpu.VMEM((MOE_BLOCK, d), F32), pltpu.SemaphoreType.DMA(())]),
        out_shape=jax.ShapeDtypeStruct((n_blocks * MOE_BLOCK, d), F32),
        compiler_params=_params(("arbitrary",), est),
        name="moe_experts",
    )(slot_tok, block_expert, n_used, h2, wg, wu, wd)


def _combine_kernel(d1_ref, d2_ref, ys_hbm, x_ref, rw_ref, mod_ref, fw_ref, o_ref, a_scr, b_scr, sem,
                    *, g_idx, final):
    i = pl.program_id(0)
    rows = a_scr.shape[0]

    def copies(r):
        t = i * rows + r
        return (pltpu.make_async_copy(ys_hbm.at[pl.ds(d1_ref[t], 1)], a_scr.at[pl.ds(r, 1)], sem.at[0]),
                pltpu.make_async_copy(ys_hbm.at[pl.ds(d2_ref[t], 1)], b_scr.at[pl.ds(r, 1)], sem.at[1]))

    def start(r, carry):
        ca, cb = copies(r)
        ca.start()
        cb.start()
        return carry

    lax.fori_loop(0, rows, start, 0)

    def wait(r, carry):
        ca, cb = copies(r)
        ca.wait()
        cb.wait()
        return carry

    lax.fori_loop(0, rows, wait, 0)
    rw = rw_ref[...]
    moe = rw[:, 0:1] * a_scr[...] + rw[:, 1:2] * b_scr[...]
    xn = x_ref[...] + mod_ref[0, g_idx:g_idx + 1, :] * moe
    if final:
        xn = xn * lax.rsqrt(jnp.mean(xn * xn, axis=-1, keepdims=True) + NORM_EPS) * fw_ref[...]
    o_ref[...] = xn


def _combine(dest1, dest2, ys, x2, route_w, mod, final_w, seq, g_idx, final, tm=128):
    t, d = x2.shape
    per_b = seq // tm
    est = 2 * (2 * tm * d * 4 + tm * LANE * 4) + 4 * tm * d * 4 + (6 << 20)
    return pl.pallas_call(
        functools.partial(_combine_kernel, g_idx=g_idx, final=final),
        grid_spec=pltpu.PrefetchScalarGridSpec(
            num_scalar_prefetch=2,
            grid=(t // tm,),
            in_specs=[pl.BlockSpec(memory_space=pl.ANY),
                      pl.BlockSpec((tm, d), lambda i, d1, d2: (i, 0)),
                      pl.BlockSpec((tm, LANE), lambda i, d1, d2: (i, 0)),
                      pl.BlockSpec((1, 6, d), lambda i, d1, d2: (i // per_b, 0, 0)),
                      pl.BlockSpec((1, d), lambda i, d1, d2: (0, 0))],
            out_specs=pl.BlockSpec((tm, d), lambda i, d1, d2: (i, 0)),
            scratch_shapes=[pltpu.VMEM((tm, d), F32), pltpu.VMEM((tm, d), F32),
                            pltpu.SemaphoreType.DMA((2,))]),
        out_shape=jax.ShapeDtypeStruct((t, d), F32),
        compiler_params=_params(("arbitrary",), est),
        name="moe_combine",
    )(dest1, dest2, ys, x2, route_w, mod, final_w.reshape(1, d))


def _rot_cols(w):
    half = w.shape[1] // 2
    return jnp.concatenate([-w[:, half:], w[:, :half]], axis=1)


def _split_in_weights(w_in, d):
    sizes = (3 * MOBA_HEADS * MOBA_HEAD_DIM, SSM_D_INNER, SSM_D_INNER + SSM_BC, SSM_HEADS, MLA_Q_LORA,
             MLA_KV_LORA, MLA_ROPE, SWA_HEADS * SWA_HEAD_DIM, SWA_KV_HEADS * SWA_HEAD_DIM,
             SWA_KV_HEADS * SWA_HEAD_DIM, N_BRANCH * d)
    bounds = np.cumsum((0,) + sizes)
    (m_qkv, z, xbc, dt, q_lat, kv_lat, k_rope, sq, sk, sv, gates) = [
        w_in[:, bounds[i]:bounds[i + 1]] for i in range(len(sizes))]
    z64 = jnp.zeros((d, SWA_HEAD_DIM), w_in.dtype)

    def spread(w):
        h0, h1 = w[:, :SWA_HEAD_DIM], w[:, SWA_HEAD_DIM:]
        return jnp.concatenate([h0, z64, z64, h0, h1, z64, z64, h1], axis=1)

    main = jnp.concatenate([gates, z, xbc[:, :SSM_D_INNER], sq, m_qkv, xbc[:, SSM_D_INNER:], spread(sk),
                            spread(sv), kv_lat, q_lat], axis=1).astype(BF16)
    tail = jnp.concatenate([k_rope, z64, _rot_cols(k_rope), z64, dt,
                            jnp.zeros((d, LANE - SSM_HEADS), w_in.dtype)], axis=1).astype(BF16)
    return main, tail


def _extend_wq(wq_b):
    k = wq_b.shape[0]
    w = wq_b.reshape(k, MLA_HEADS, MLA_NOPE + MLA_ROPE)
    z = jnp.zeros((k, MLA_HEADS, LANE - MLA_ROPE), wq_b.dtype)
    rope = w[..., MLA_NOPE:]
    half = MLA_ROPE // 2
    rot = jnp.concatenate([-rope[..., half:], rope[..., :half]], axis=-1)
    return jnp.concatenate([w[..., :MLA_NOPE], rope, z, rot, z], axis=-1).reshape(k, -1).astype(BF16)


def _mixer(x, mod, pos_col, pos_row, cos_t, sin_t, norm_w, w_in, conv_w, conv_b, dt_bias, a_log, d_skip,
           ssm_norm, mla_q_norm, mla_wq_b, mla_kv_norm, mla_wkv_b, swa_sinks, w_branch, w_out):
    b, s, d = x.shape
    t = b * s
    off, n_proj = _proj_layout(d)
    swa_slopes, moba_slopes = _alibi_slopes()
    h = _norm_mod(x, norm_w, mod, sh_idx=0, sc_idx=1)
    w_main, w_tail = _split_in_weights(w_in, d)
    h2 = h.reshape(t, d)
    proj = _matmul(h2, w_main, BF16, tm=1024, tn=768, name="in_proj")
    tail = _matmul(h2, w_tail, F32, tm=1024, tn=3 * LANE, name="in_proj_tail")
    proj3 = proj.reshape(b, s, n_proj)
    tail3 = tail.reshape(b, s, 3 * LANE)

    o_moba = _flash(proj3, proj3, proj3, heads=MOBA_HEADS, dq=MOBA_HEAD_DIM, dv=MOBA_HEAD_DIM,
                    q_off=off["mq"] // LANE, k_off=off["mk"] // LANE, v_off=off["mv"] // LANE,
                    moba=True, q_scale=float(MOBA_HEAD_DIM ** -0.5), slopes=jnp.asarray(moba_slopes),
                    pos_col=pos_col, pos_row=pos_row, name="moba_attention")
    o_ssm = _ssd(proj3, tail3, off, conv_w, conv_b, dt_bias, a_log, d_skip, ssm_norm)
    q_m, k_m, v_m = _mla_project(proj, tail, off, mla_q_norm, _extend_wq(mla_wq_b), mla_kv_norm,
                                 mla_wkv_b.astype(BF16), cos_t, sin_t)
    o_mla = _flash(q_m.reshape(b, s, -1), k_m.reshape(b, s, -1), v_m.reshape(b, s, -1),
                   heads=MLA_HEADS, dq=MLA_QK, dv=MLA_V, q_off=0, k_off=0, v_off=0, name="mla_attention")
    o_swa = _swa(proj3, off, swa_sinks, pos_col, pos_row, swa_slopes)

    branches = [o.reshape(t, BRANCH_W) for o in (o_moba, o_ssm, o_mla, o_swa)]
    merged = _merge(branches, w_branch.astype(BF16), proj, d)
    return _out_proj(merged, w_out.astype(BF16), x.reshape(t, d), mod, s, g_idx=2)


def _moe(x2, seq, mod, norm_w, wg, bg, wr, br, w_gate, w_up, w_down, final_w, final):
    t, d = x2.shape
    b = t // seq
    h, route_i, route_w, counts = _router(x2.reshape(b, seq, d), norm_w, mod, wg, bg, wr, br,
                                          sh_idx=3, sc_idx=4)
    route_i = route_i.reshape(t, LANE)
    counts = counts[0, :N_EXPERTS].astype(jnp.int32)
    padded = (counts + MOE_BLOCK - 1) // MOE_BLOCK * MOE_BLOCK
    pad_end = jnp.cumsum(padded)
    pad_start = pad_end - padded
    n_blocks = -(-(t * 2) // MOE_BLOCK) + N_EXPERTS
    dest1 = pad_start[route_i[:, 0]] + route_i[:, 2]
    dest2 = pad_start[route_i[:, 1]] + route_i[:, 3]
    block_expert = jnp.minimum(
        jnp.searchsorted(pad_end, jnp.arange(n_blocks, dtype=jnp.int32) * MOE_BLOCK, side="right"),
        N_EXPERTS - 1).astype(jnp.int32)
    n_used = (pad_end[-1:] // MOE_BLOCK).astype(jnp.int32)
    slot_tok = _slot_tokens(dest1, dest2, n_blocks * MOE_BLOCK)
    ys = _experts(slot_tok, block_expert, n_used, h.reshape(t, d), w_gate.astype(BF16), w_up.astype(BF16),
                  w_down.astype(BF16))
    return _combine(dest1, dest2, ys, x2, route_w.reshape(t, LANE), mod, final_w, seq, g_idx=5, final=final)


def kernel(x, c, positions, ada_w, ada_b, norm_mix, norm_ffn, w_in, conv_w, conv_b, dt_bias, a_log, d_skip,
           ssm_norm, mla_q_norm, mla_wq_b, mla_kv_norm, mla_wkv_b, swa_sinks, w_branch, w_out,
           router_group_w, router_group_b, router_w, router_b, exp_w_gate, exp_w_up, exp_w_down, final_norm):
    b, s, d = x.shape
    depth = ada_w.shape[0]
    mods = _ada_mod(c, ada_w, ada_b)
    pos_f = positions.astype(F32)
    pos_col = pos_f.reshape(b, s, 1)
    pos_row = pos_f.reshape(b, 1, s)
    cos_t, sin_t = _rope_tables(pos_f.reshape(b * s, 1))
    for l in range(depth):
        x2 = _mixer(x, mods[l], pos_col, pos_row, cos_t, sin_t, norm_mix[l], w_in[l], conv_w[l], conv_b[l],
                    dt_bias[l], a_log[l], d_skip[l], ssm_norm[l], mla_q_norm[l], mla_wq_b[l],
                    mla_kv_norm[l], mla_wkv_b[l], swa_sinks[l], w_branch[l], w_out[l])
        x2 = _moe(x2, s, mods[l], norm_ffn[l], router_group_w[l], router_group_b[l], router_w[l],
                  router_b[l], exp_w_gate[l], exp_w_up[l], exp_w_down[l], final_norm,
                  final=(l == depth - 1))
        x = x2.reshape(b, s, d)
    return x
```

```python
import functools
import math

import numpy as np
import jax
import jax.numpy as jnp
from jax import lax
from jax.experimental import pallas as pl
from jax.experimental.pallas import tpu as pltpu

F32 = jnp.float32
BF16 = jnp.bfloat16
HIGHEST = lax.Precision.HIGHEST

MOBA_HEADS = 8
MOBA_HEAD_DIM = 128
MOBA_BLOCK = 256
MOBA_TOPK = 3
SSM_D_INNER = 1024
SSM_HEAD_DIM = 64
SSM_HEADS = SSM_D_INNER // SSM_HEAD_DIM
SSM_GROUPS = 2
SSM_STATE = 128
SSM_CONV = 4
SSM_CHUNK = 128
SSM_BC = 2 * SSM_GROUPS * SSM_STATE
MLA_HEADS = 8
MLA_Q_LORA = 768
MLA_KV_LORA = 512
MLA_NOPE = 128
MLA_ROPE = 64
MLA_V = 128
ROPE_THETA = 10000.0
SWA_HEADS = 16
SWA_KV_HEADS = 2
SWA_HEAD_DIM = 64
SWA_WINDOW = 128
N_BRANCH = 4
BRANCH_W = 1024
N_GROUPS = 4
EXPERTS_PER_GROUP = 8
N_EXPERTS = N_GROUPS * EXPERTS_PER_GROUP
EXPERT_HIDDEN = 512
MOE_BLOCK = 128
NORM_EPS = 1e-6
N_ALIBI = MOBA_HEADS + SWA_HEADS

LANE = 128
VMEM_CAP = 60 * 1024 * 1024
ATTN_TILE = 512
MLA_QK = 256


def _alibi_slopes():
    i = np.arange(1, N_ALIBI + 1, dtype=np.float64)
    s = np.exp2(-8.0 * i / N_ALIBI).astype(np.float32)
    return s[:SWA_HEADS], s[SWA_HEADS:]


IN_SIZES_HEAD = (3 * MOBA_HEADS * MOBA_HEAD_DIM, SSM_D_INNER, SSM_D_INNER + SSM_BC, SSM_HEADS, MLA_Q_LORA,
                 MLA_KV_LORA, MLA_ROPE, SWA_HEADS * SWA_HEAD_DIM, SWA_KV_HEADS * SWA_HEAD_DIM,
                 SWA_KV_HEADS * SWA_HEAD_DIM)
A_COLS = IN_SIZES_HEAD[0] + IN_SIZES_HEAD[1] + IN_SIZES_HEAD[2]
GATE_COL0 = sum(IN_SIZES_HEAD)
OFF_A = dict(mq=0, mk=1024, mv=2048, z=3072, xs=4096, bc=5120)
OFF_S = dict(sq=0, skx=1024, svx=1536, kvlat=2048, qlat=3072)
S_COLS = 3840


def _params(sem, est_bytes):
    limit = int(min(VMEM_CAP, max(est_bytes, 16 * 1024 * 1024)))
    return pltpu.CompilerParams(dimension_semantics=sem, vmem_limit_bytes=limit)


def _silu(v):
    return v * (1.0 / (1.0 + jnp.exp(-v)))


def _ada_kernel(c_ref, w_ref, b_ref, o_ref):
    k = pl.program_id(1)

    @pl.when(k == 0)
    def _():
        o_ref[0] = jnp.broadcast_to(b_ref[0], o_ref.shape[1:])

    o_ref[0] += jnp.dot(c_ref[...].astype(BF16), w_ref[0].astype(BF16), preferred_element_type=F32)


def _ada_mod(c, ada_w, ada_b):
    depth, d, n = ada_w.shape
    b = c.shape[0]
    c8 = jnp.zeros((8, d), F32).at[:b].set(c)
    tk = LANE
    out = pl.pallas_call(
        _ada_kernel,
        grid=(depth, d // tk),
        in_specs=[pl.BlockSpec((8, tk), lambda l, k: (0, k)),
                  pl.BlockSpec((1, tk, n), lambda l, k: (l, k, 0)),
                  pl.BlockSpec((1, 1, n), lambda l, k: (l, 0, 0))],
        out_specs=pl.BlockSpec((1, 8, n), lambda l, k: (l, 0, 0)),
        out_shape=jax.ShapeDtypeStruct((depth, 8, n), F32),
        compiler_params=_params(("parallel", "arbitrary"), 3 * tk * n * 4 + 32 * n * 4 + (4 << 20)),
        name="ada_mod",
    )(c8, ada_w, ada_b.reshape(depth, 1, n))
    return out[:, :b].reshape(depth, b, 6, d)


def _norm_mod_kernel(x_ref, w_ref, mod_ref, o_ref, *, sh_idx, sc_idx):
    x = x_ref[0]
    y = x * lax.rsqrt(jnp.mean(x * x, axis=-1, keepdims=True) + NORM_EPS) * w_ref[...]
    m = mod_ref[0]
    o_ref[0] = (y * (1.0 + m[sc_idx:sc_idx + 1]) + m[sh_idx:sh_idx + 1]).astype(o_ref.dtype)


def _norm_mod(x, w, mod, sh_idx, sc_idx, tm=256):
    b, s, d = x.shape
    return pl.pallas_call(
        functools.partial(_norm_mod_kernel, sh_idx=sh_idx, sc_idx=sc_idx),
        grid=(b, s // tm),
        in_specs=[pl.BlockSpec((1, tm, d), lambda i, j: (i, j, 0)),
                  pl.BlockSpec((1, d), lambda i, j: (0, 0)),
                  pl.BlockSpec((1, 6, d), lambda i, j: (i, 0, 0))],
        out_specs=pl.BlockSpec((1, tm, d), lambda i, j: (i, j, 0)),
        out_shape=jax.ShapeDtypeStruct((b, s, d), BF16),
        compiler_params=_params(("parallel", "parallel"), 6 * tm * d * 4 + (4 << 20)),
        name="norm_mod",
    )(x, w.reshape(1, d), mod)


def _mm_kernel(a_ref, w_ref, o_ref):
    o_ref[...] = jnp.dot(a_ref[...], w_ref[...], preferred_element_type=F32).astype(o_ref.dtype)


def _matmul(a, w, out_dtype, tm, tn, name):
    m, k = a.shape
    n = w.shape[1]
    est = 2 * (tm * k * 2 + k * tn * 2 + tm * tn * 4) + tm * tn * 4 + (4 << 20)
    return pl.pallas_call(
        _mm_kernel,
        grid=(m // tm, n // tn),
        in_specs=[pl.BlockSpec((tm, k), lambda i, j: (i, 0)),
                  pl.BlockSpec((k, tn), lambda i, j: (0, j))],
        out_specs=pl.BlockSpec((tm, tn), lambda i, j: (i, j)),
        out_shape=jax.ShapeDtypeStruct((m, n), out_dtype),
        compiler_params=_params(("parallel", "parallel"), est),
        name=name,
    )(a, w)


def _wproj_kernel(*refs, shift):
    if shift:
        a_ref, w_ref, wx_ref, o_ref, wbf_scr = refs
    else:
        a_ref, w_ref, o_ref, wbf_scr = refs
    k, tn = wbf_scr.shape
    rows = 256

    @pl.when(pl.program_id(1) == 0)
    def _():
        for r in range(0, k, rows):
            if shift:
                cat = jnp.concatenate([w_ref[0, r:r + rows, :], wx_ref[0, r:r + rows, :]], axis=1)
                wbf_scr[r:r + rows, :] = cat[:, shift:shift + tn].astype(BF16)
            else:
                wbf_scr[r:r + rows, :] = w_ref[0, r:r + rows, :].astype(BF16)

    o_ref[...] = jnp.dot(a_ref[...], wbf_scr[...], preferred_element_type=F32).astype(o_ref.dtype)


def _wproj(a, w_stack, layer, col0, n_cols, name, tm=1024, tn=512):
    m, k = a.shape
    aligned0 = col0 // LANE * LANE
    shift = col0 - aligned0
    assert aligned0 % tn == 0 and n_cols % tn == 0 and m % tm == 0
    j0 = aligned0 // tn
    in_specs = [pl.BlockSpec((tm, k), lambda j, i: (i, 0)),
                pl.BlockSpec((1, k, tn), lambda j, i: (layer, 0, j0 + j))]
    args = [a, w_stack]
    if shift:
        x0 = aligned0 // LANE + tn // LANE
        in_specs.append(pl.BlockSpec((1, k, LANE), lambda j, i: (layer, 0, x0 + j * (tn // LANE))))
        args.append(w_stack)
    est = 2 * (tm * k * 2 + k * (tn + LANE) * 4 + tm * tn * 2) + k * tn * 2 + tm * tn * 4 + (6 << 20)
    return pl.pallas_call(
        functools.partial(_wproj_kernel, shift=shift),
        grid=(n_cols // tn, m // tm),
        in_specs=in_specs,
        out_specs=pl.BlockSpec((tm, tn), lambda j, i: (i, j)),
        out_shape=jax.ShapeDtypeStruct((m, n_cols), BF16),
        scratch_shapes=[pltpu.VMEM((k, tn), BF16)],
        compiler_params=_params(("parallel", "arbitrary"), est),
        name=name,
    )(*args)


def _rope_kernel(pos_ref, freq_ref, cos_ref, sin_ref):
    ang = pos_ref[...] * freq_ref[...]
    lane = lax.broadcasted_iota(jnp.int32, ang.shape, 1)
    live = lane < MLA_ROPE
    cos_ref[...] = jnp.where(live, jnp.cos(ang), 0.0)
    sin_ref[...] = jnp.where(live, jnp.sin(ang), 0.0)


def _rope_tables(pos_col):
    t = pos_col.shape[0]
    half = MLA_ROPE // 2
    inv = ROPE_THETA ** (-np.arange(half, dtype=np.float32) / half)
    freq = np.zeros((1, LANE), np.float32)
    freq[0, :half] = inv
    freq[0, half:2 * half] = inv
    tm = 512
    return pl.pallas_call(
        _rope_kernel,
        grid=(t // tm,),
        in_specs=[pl.BlockSpec((tm, 1), lambda i: (i, 0)),
                  pl.BlockSpec((1, LANE), lambda i: (0, 0))],
        out_specs=[pl.BlockSpec((tm, LANE), lambda i: (i, 0))] * 2,
        out_shape=[jax.ShapeDtypeStruct((t, LANE), F32)] * 2,
        compiler_params=_params(("parallel",), 16 << 20),
        name="rope_tables",
    )(pos_col, jnp.asarray(freq))


def _mla_q_kernel(x_ref, nw_ref, w_ref, cos_ref, sin_ref, o_ref, *, scale):
    x = x_ref[...].astype(F32)
    y = x * lax.rsqrt(jnp.mean(x * x, axis=-1, keepdims=True) + NORM_EPS) * nw_ref[...]
    r = jnp.dot(y.astype(BF16), w_ref[...], preferred_element_type=F32)
    cos = cos_ref[...]
    sin = sin_ref[...]
    for h in range(MLA_HEADS):
        base = h * 3 * LANE
        nope = r[:, base:base + LANE]
        rope = r[:, base + LANE:base + 2 * LANE]
        rot = r[:, base + 2 * LANE:base + 3 * LANE]
        o_ref[:, h * MLA_QK:h * MLA_QK + LANE] = (nope * scale).astype(o_ref.dtype)
        o_ref[:, h * MLA_QK + LANE:(h + 1) * MLA_QK] = (
            (rope * cos + rot * sin) * scale).astype(o_ref.dtype)


def _mla_kv_kernel(x_ref, nw_ref, w_ref, kr_ref, krot_ref, cos_ref, sin_ref, k_ref, v_ref):
    x = x_ref[...].astype(F32)
    y = x * lax.rsqrt(jnp.mean(x * x, axis=-1, keepdims=True) + NORM_EPS) * nw_ref[...]
    r = jnp.dot(y.astype(BF16), w_ref[...], preferred_element_type=F32)
    kr = (kr_ref[...] * cos_ref[...] + krot_ref[...] * sin_ref[...]).astype(k_ref.dtype)
    for h in range(MLA_HEADS):
        base = h * (MLA_NOPE + MLA_V)
        k_ref[:, h * MLA_QK:h * MLA_QK + LANE] = r[:, base:base + MLA_NOPE].astype(k_ref.dtype)
        k_ref[:, h * MLA_QK + LANE:(h + 1) * MLA_QK] = kr
        v_ref[:, h * MLA_V:(h + 1) * MLA_V] = r[:, base + MLA_NOPE:base + MLA_NOPE + MLA_V].astype(
            v_ref.dtype)


def _mla_project(proj, tail, off, q_norm, wq_ext, kv_norm, wkv, cos_t, sin_t, tm=512):
    t = proj.shape[0]
    scale = float((MLA_NOPE + MLA_ROPE) ** -0.5)
    q = pl.pallas_call(
        functools.partial(_mla_q_kernel, scale=scale),
        grid=(t // tm,),
        in_specs=[pl.BlockSpec((tm, MLA_Q_LORA), lambda i: (i, off["qlat"] // MLA_Q_LORA)),
                  pl.BlockSpec((1, MLA_Q_LORA), lambda i: (0, 0)),
                  pl.BlockSpec(wq_ext.shape, lambda i: (0, 0)),
                  pl.BlockSpec((tm, LANE), lambda i: (i, 0)),
                  pl.BlockSpec((tm, LANE), lambda i: (i, 0))],
        out_specs=pl.BlockSpec((tm, MLA_HEADS * MLA_QK), lambda i: (i, 0)),
        out_shape=jax.ShapeDtypeStruct((t, MLA_HEADS * MLA_QK), BF16),
        compiler_params=_params(("parallel",), 40 << 20),
        name="mla_q_proj",
    )(proj, q_norm.reshape(1, -1), wq_ext, cos_t, sin_t)
    k, v = pl.pallas_call(
        _mla_kv_kernel,
        grid=(t // tm,),
        in_specs=[pl.BlockSpec((tm, MLA_KV_LORA), lambda i: (i, off["kvlat"] // MLA_KV_LORA)),
                  pl.BlockSpec((1, MLA_KV_LORA), lambda i: (0, 0)),
                  pl.BlockSpec(wkv.shape, lambda i: (0, 0)),
                  pl.BlockSpec((tm, LANE), lambda i: (i, 0)),
                  pl.BlockSpec((tm, LANE), lambda i: (i, 1)),
                  pl.BlockSpec((tm, LANE), lambda i: (i, 0)),
                  pl.BlockSpec((tm, LANE), lambda i: (i, 0))],
        out_specs=[pl.BlockSpec((tm, MLA_HEADS * MLA_QK), lambda i: (i, 0)),
                   pl.BlockSpec((tm, MLA_HEADS * MLA_V), lambda i: (i, 0))],
        out_shape=[jax.ShapeDtypeStruct((t, MLA_HEADS * MLA_QK), BF16),
                   jax.ShapeDtypeStruct((t, MLA_HEADS * MLA_V), BF16)],
        compiler_params=_params(("parallel",), 40 << 20),
        name="mla_kv_proj",
    )(proj, kv_norm.reshape(1, -1), wkv, tail, tail, cos_t, sin_t)
    return q, k, v


def _flash_kernel(*refs, moba, q_scale):
    if moba:
        slope_ref, q_ref, k_ref, v_ref, posk_ref, posq_ref, o_ref, vt_scr, kmean_scr, sel_scr = refs
    else:
        q_ref, k_ref, v_ref, o_ref, vt_scr = refs
    tile = ATTN_TILE
    seq = k_ref.shape[1]
    sub = tile // MOBA_BLOCK
    qi = pl.program_id(2)

    @pl.when(qi == 0)
    def _():
        for c in range(seq // tile):
            vt_scr[c] = v_ref[0, c * tile:(c + 1) * tile, :].astype(F32).T.astype(BF16)
        if moba:
            for n in range(seq // MOBA_BLOCK):
                kblk = k_ref[0, n * MOBA_BLOCK:(n + 1) * MOBA_BLOCK, :].astype(F32)
                kmean_scr[n:n + 1, :] = jnp.mean(kblk, axis=0, keepdims=True)

    q_raw = q_ref[0]
    if q_scale is None:
        q = q_raw
    else:
        q = (q_raw.astype(F32) * q_scale).astype(BF16)

    if moba:
        nblk = seq // MOBA_BLOCK
        slope = slope_ref[pl.program_id(1)]
        pq = posq_ref[0]
        gate = lax.dot_general(kmean_scr[...], q_raw.astype(F32), (((1,), (1,)), ((), ())),
                               precision=HIGHEST, preferred_element_type=F32)
        n_iota = lax.broadcasted_iota(jnp.int32, gate.shape, 0)
        q_blk = qi * sub + lax.broadcasted_iota(jnp.int32, gate.shape, 1) // MOBA_BLOCK
        beaten = jnp.zeros(gate.shape, F32)
        for m in range(nblk):
            gm = gate[m:m + 1, :]
            wins = jnp.where(gm > gate, 1.0, jnp.where(gm == gate, jnp.where(m < n_iota, 1.0, 0.0), 0.0))
            beaten = beaten + jnp.where(m < q_blk, wins, 0.0)
        sel_scr[...] = jnp.where(beaten < MOBA_TOPK, jnp.where(n_iota < q_blk, 1.0, 0.0), 0.0)

    def scores(c):
        start = pl.multiple_of(c * tile, tile)
        kb = k_ref[0, pl.ds(start, tile), :]
        st = lax.dot_general(kb, q, (((1,), (1,)), ((), ())), preferred_element_type=F32)
        if moba:
            pk = posk_ref[0, pl.ds(start, tile), :]
            st = st - slope * jnp.abs(pq - pk)
        return st

    def selected(c):
        rows = [jnp.broadcast_to(sel_scr[pl.ds(c * sub + j, 1), :], (MOBA_BLOCK, tile)) for j in range(sub)]
        return jnp.concatenate(rows, axis=0)

    st = scores(qi)
    key_i = lax.broadcasted_iota(jnp.int32, st.shape, 0)
    qry_i = lax.broadcasted_iota(jnp.int32, st.shape, 1)
    causal = key_i <= qry_i
    if moba:
        same_blk = (key_i // MOBA_BLOCK) == (qry_i // MOBA_BLOCK)
        keep = jnp.where(same_blk, jnp.where(causal, 1.0, 0.0), selected(qi)) > 0.0
    else:
        keep = causal
    st = jnp.where(keep, st, -jnp.inf)
    m0 = jnp.max(st, axis=0, keepdims=True)
    p = jnp.exp(st - m0)
    l0 = jnp.sum(p, axis=0, keepdims=True)
    acc0 = jnp.dot(vt_scr[qi], p.astype(BF16), preferred_element_type=F32)

    def body(c, carry):
        m_prev, l_prev, acc = carry
        s_c = scores(c)
        if moba:
            s_c = jnp.where(selected(c) > 0.0, s_c, -jnp.inf)
        m_new = jnp.maximum(m_prev, jnp.max(s_c, axis=0, keepdims=True))
        alpha = jnp.exp(m_prev - m_new)
        p_c = jnp.exp(s_c - m_new)
        l_new = alpha * l_prev + jnp.sum(p_c, axis=0, keepdims=True)
        acc_new = alpha * acc + jnp.dot(vt_scr[c], p_c.astype(BF16), preferred_element_type=F32)
        return m_new, l_new, acc_new

    _, l_fin, acc = lax.fori_loop(0, qi, body, (m0, l0, acc0))
    o_ref[0] = (acc * (1.0 / l_fin)).T.astype(o_ref.dtype)


def _flash(q, k, v, *, heads, dq, dv, q_off, k_off, v_off, moba=False, q_scale=None,
           slopes=None, pos_col=None, pos_row=None, name="flash"):
    b, s, _ = q.shape
    tile = ATTN_TILE
    in_specs = [pl.BlockSpec((1, tile, dq), lambda bi, h, i: (bi, i, q_off + h)),
                pl.BlockSpec((1, s, dq), lambda bi, h, i: (bi, 0, k_off + h)),
                pl.BlockSpec((1, s, dv), lambda bi, h, i: (bi, 0, v_off + h))]
    args = [q, k, v]
    scratch = [pltpu.VMEM((s // tile, dv, tile), BF16)]
    if moba:
        in_specs = [pl.BlockSpec(memory_space=pltpu.SMEM)] + in_specs + [
            pl.BlockSpec((1, s, 1), lambda bi, h, i: (bi, 0, 0)),
            pl.BlockSpec((1, 1, tile), lambda bi, h, i: (bi, 0, i))]
        args = [slopes] + args + [pos_col, pos_row]
        scratch += [pltpu.VMEM((s // MOBA_BLOCK, dq), F32), pltpu.VMEM((s // MOBA_BLOCK, tile), F32)]
    est = 2 * (s * dq * 2 + s * dv * 2) + s * dv * 2 + s * LANE * 4 * 2 + (16 << 20)
    return pl.pallas_call(
        functools.partial(_flash_kernel, moba=moba, q_scale=q_scale),
        grid=(b, heads, s // tile),
        in_specs=in_specs,
        out_specs=pl.BlockSpec((1, tile, dv), lambda bi, h, i: (bi, i, h)),
        out_shape=jax.ShapeDtypeStruct((b, s, heads * dv), BF16),
        scratch_shapes=scratch,
        compiler_params=_params(("parallel", "parallel", "arbitrary"), est),
        name=name,
    )(*args)


def _swa_kernel(sink_ref, q_ref, kp_ref, kc_ref, vp_ref, vc_ref, pq_ref, pkp_ref, pkc_ref, o_ref,
                *, slopes):
    w = SWA_WINDOW
    n = pl.program_id(1)
    kx = jnp.concatenate([kp_ref[0], kc_ref[0]], axis=0)
    vx = jnp.concatenate([vp_ref[0], vc_ref[0]], axis=0)
    pk = jnp.concatenate([pkp_ref[0], pkc_ref[0]], axis=1)
    dist = jnp.abs(pq_ref[0] - pk)
    qi = lax.broadcasted_iota(jnp.int32, dist.shape, 0)
    kk = lax.broadcasted_iota(jnp.int32, dist.shape, 1)
    allowed = jnp.where(kk > qi, jnp.where(kk <= qi + w, 1.0, 0.0), 0.0)
    allowed = jnp.where(kk >= w, allowed, jnp.where(n > 0, allowed, 0.0)) > 0.0
    scale = float(SWA_HEAD_DIM ** -0.5)
    rep = SWA_HEADS // SWA_KV_HEADS
    for pair in range(SWA_HEADS // 2):
        qp = q_ref[0, :, pair * LANE:(pair + 1) * LANE]
        acc = jnp.zeros((w, LANE), F32)
        for half in range(2):
            h = 2 * pair + half
            col = (2 * (h // rep) + half) * LANE
            s = lax.dot_general(qp, kx[:, col:col + LANE], (((1,), (1,)), ((), ())),
                                preferred_element_type=F32)
            s = s * scale - float(slopes[h]) * dist
            s = jnp.where(allowed, s, -jnp.inf)
            sink = sink_ref[h]
            m = jnp.maximum(jnp.max(s, axis=1, keepdims=True), sink)
            p = jnp.exp(s - m)
            denom = jnp.sum(p, axis=1, keepdims=True) + jnp.exp(sink - m)
            p = p * (1.0 / denom)
            acc = acc + jnp.dot(p.astype(BF16), vx[:, col:col + LANE], preferred_element_type=F32)
        o_ref[0, :, pair * LANE:(pair + 1) * LANE] = acc.astype(o_ref.dtype)


def _swa(proj3, off, sinks, pos_col, pos_row, slopes):
    b, s, _ = proj3.shape
    w = SWA_WINDOW
    qw = SWA_HEADS * SWA_HEAD_DIM
    kw = 4 * LANE
    prev = lambda j: jnp.maximum(j - 1, 0)
    return pl.pallas_call(
        functools.partial(_swa_kernel, slopes=tuple(float(v) for v in slopes)),
        grid=(b, s // w),
        in_specs=[pl.BlockSpec(memory_space=pltpu.SMEM),
                  pl.BlockSpec((1, w, qw), lambda i, j: (i, j, off["sq"] // qw)),
                  pl.BlockSpec((1, w, kw), lambda i, j: (i, prev(j), off["skx"] // kw)),
                  pl.BlockSpec((1, w, kw), lambda i, j: (i, j, off["skx"] // kw)),
                  pl.BlockSpec((1, w, kw), lambda i, j: (i, prev(j), off["svx"] // kw)),
                  pl.BlockSpec((1, w, kw), lambda i, j: (i, j, off["svx"] // kw)),
                  pl.BlockSpec((1, w, 1), lambda i, j: (i, j, 0)),
                  pl.BlockSpec((1, 1, w), lambda i, j: (i, 0, prev(j))),
                  pl.BlockSpec((1, 1, w), lambda i, j: (i, 0, j))],
        out_specs=pl.BlockSpec((1, w, qw), lambda i, j: (i, j, 0)),
        out_shape=jax.ShapeDtypeStruct((b, s, qw), BF16),
        compiler_params=_params(("parallel", "parallel"), 24 << 20),
        name="swa_attention",
    )(sinks, proj3, proj3, proj3, proj3, proj3, pos_col, pos_row, pos_row)


def _ssd_kernel(xs_ref, bc_ref, z_ref, dt_ref, cwx_ref, cwb_ref, cbx_ref, cbb_ref, dtb_ref,
                alog_ref, dsk_ref, nw_ref, exp_ref, o_ref, padx_scr, padb_scr, st_scr):
    ch = SSM_CHUNK
    c = pl.program_id(1)

    @pl.when(c == 0)
    def _():
        padx_scr[0:8, :] = jnp.zeros((8, padx_scr.shape[1]), F32)
        padb_scr[0:8, :] = jnp.zeros((8, padb_scr.shape[1]), F32)
        st_scr[...] = jnp.zeros(st_scr.shape, F32)

    padx_scr[8:8 + ch, :] = xs_ref[0].astype(F32)
    padb_scr[8:8 + ch, :] = bc_ref[0].astype(F32)

    def conv(pad_scr, w_ref, b_ref):
        acc = b_ref[...] + w_ref[0:1, :] * pad_scr[5:5 + ch, :]
        for k in range(1, SSM_CONV):
            acc = acc + w_ref[k:k + 1, :] * pad_scr[5 + k:5 + k + ch, :]
        return _silu(acc)

    xs = conv(padx_scr, cwx_ref, cbx_ref)
    bcv = conv(padb_scr, cwb_ref, cbb_ref)
    padx_scr[0:8, :] = padx_scr[ch:ch + 8, :]
    padb_scr[0:8, :] = padb_scr[ch:ch + 8, :]

    gn = SSM_GROUPS * SSM_STATE
    dtr = dt_ref[0] + dtb_ref[...]
    dt = jnp.maximum(dtr, 0.0) + jnp.log(1.0 + jnp.exp(-jnp.abs(dtr)))
    a = dt * (-jnp.exp(alog_ref[...]))
    expand = exp_ref[...]
    row = lax.broadcasted_iota(jnp.int32, (ch, ch), 0)
    colm = lax.broadcasted_iota(jnp.int32, (ch, ch), 1)
    tril = row >= colm
    tri = jnp.where(tril, 1.0, 0.0)
    dt_e = jnp.dot(dt, expand, precision=HIGHEST, preferred_element_type=F32)
    acs = jnp.dot(tri, a, precision=HIGHEST, preferred_element_type=F32)
    acs_e = jnp.dot(acs, expand, precision=HIGHEST, preferred_element_type=F32)
    acs_t = acs.T
    x_dt = xs * dt_e
    last = acs_e[ch - 1:ch, :]
    x_dec = (x_dt * jnp.exp(last - acs_e)).astype(BF16)
    lane = lax.broadcasted_iota(jnp.int32, (ch, LANE), 1)
    lo = lane < SSM_HEAD_DIM
    half_w = SSM_D_INNER // SSM_GROUPS
    heads_per_group = SSM_HEADS // SSM_GROUPS
    ydiag = []
    yoff = []
    for g in range(SSM_GROUPS):
        bg = bcv[:, g * SSM_STATE:(g + 1) * SSM_STATE]
        cg = bcv[:, gn + g * SSM_STATE:gn + (g + 1) * SSM_STATE].astype(BF16)
        gmat = lax.dot_general(cg, bg.astype(BF16), (((1,), (1,)), ((), ())),
                               preferred_element_type=F32)
        st_g = st_scr[:, g * half_w:(g + 1) * half_w]
        yoff.append(jnp.dot(cg, st_g.astype(BF16), preferred_element_type=F32))
        for j in range(heads_per_group // 2):
            pair = g * (heads_per_group // 2) + j
            xp = x_dt[:, pair * LANE:(pair + 1) * LANE]
            acc = jnp.zeros((ch, LANE), F32)
            for half in range(2):
                h = 2 * pair + half
                seg = jnp.exp(jnp.where(tril, acs[:, h:h + 1] - acs_t[h:h + 1, :], -jnp.inf))
                mmat = (gmat * seg).astype(BF16)
                xh = (jnp.where(lo, xp, 0.0) if half == 0 else jnp.where(lo, 0.0, xp)).astype(BF16)
                acc = acc + jnp.dot(mmat, xh, preferred_element_type=F32)
            ydiag.append(acc)
        upd = jnp.dot(bg.T.astype(BF16), x_dec[:, g * half_w:(g + 1) * half_w],
                      preferred_element_type=F32)
        st_scr[:, g * half_w:(g + 1) * half_w] = st_g * jnp.exp(last[:, g * half_w:(g + 1) * half_w]) + upd
    y = (jnp.concatenate(ydiag, axis=1) + jnp.concatenate(yoff, axis=1) * jnp.exp(acs_e)
         + dsk_ref[...] * xs)
    gz = y * _silu(z_ref[0].astype(F32))
    outs = []
    for g in range(SSM_GROUPS):
        gg = gz[:, g * half_w:(g + 1) * half_w]
        outs.append(gg * lax.rsqrt(jnp.mean(gg * gg, axis=-1, keepdims=True) + NORM_EPS))
    o_ref[0] = (jnp.concatenate(outs, axis=1) * nw_ref[...]).astype(o_ref.dtype)


def _ssd(proj3, tail3, off, conv_w, conv_b, dt_bias, a_log, d_skip, norm_w):
    b, s, _ = proj3.shape
    ch = SSM_CHUNK
    di = SSM_D_INNER
    pad16 = lambda v: jnp.zeros((1, LANE), F32).at[0, :SSM_HEADS].set(v)
    expand = np.zeros((LANE, di), np.float32)
    for h in range(SSM_HEADS):
        expand[h, h * SSM_HEAD_DIM:(h + 1) * SSM_HEAD_DIM] = 1.0
    full = lambda shape: pl.BlockSpec(shape, lambda i, j: (0,) * len(shape))
    return pl.pallas_call(
        _ssd_kernel,
        grid=(b, s // ch),
        in_specs=[pl.BlockSpec((1, ch, di), lambda i, j: (i, j, off["xs"] // di)),
                  pl.BlockSpec((1, ch, SSM_BC), lambda i, j: (i, j, off["bc"] // SSM_BC)),
                  pl.BlockSpec((1, ch, di), lambda i, j: (i, j, off["z"] // di)),
                  pl.BlockSpec((1, ch, LANE), lambda i, j: (i, j, 2)),
                  full((SSM_CONV, di)), full((SSM_CONV, SSM_BC)), full((1, di)), full((1, SSM_BC)),
                  full((1, LANE)), full((1, LANE)), full((1, di)), full((1, di)), full((LANE, di))],
        out_specs=pl.BlockSpec((1, ch, di), lambda i, j: (i, j, 0)),
        out_shape=jax.ShapeDtypeStruct((b, s, di), BF16),
        scratch_shapes=[pltpu.VMEM((ch + 8, di), F32), pltpu.VMEM((ch + 8, SSM_BC), F32),
                        pltpu.VMEM((SSM_STATE, di), F32)],
        compiler_params=_params(("parallel", "arbitrary"), 32 << 20),
        name="ssd_mixer",
    )(proj3, proj3, proj3, tail3, conv_w[:, :di], conv_w[:, di:], conv_b[:di].reshape(1, di),
      conv_b[di:].reshape(1, SSM_BC), pad16(dt_bias), pad16(a_log),
      jnp.repeat(d_skip, SSM_HEAD_DIM).reshape(1, di), norm_w.reshape(1, di), jnp.asarray(expand))


def _merge_kernel(o0_ref, o1_ref, o2_ref, o3_ref, w_ref, g0_ref, g1_ref, g2_ref, g3_ref, out_ref):
    acc = None
    for r, (o_ref, g_ref) in enumerate(((o0_ref, g0_ref), (o1_ref, g1_ref), (o2_ref, g2_ref),
                                        (o3_ref, g3_ref))):
        y = jnp.dot(o_ref[...], w_ref[r], preferred_element_type=F32)
        gate = 1.0 / (1.0 + jnp.exp(-g_ref[...].astype(F32)))
        acc = gate * y if acc is None else acc + gate * y
    out_ref[...] = acc.astype(out_ref.dtype)


def _merge(branches, w_branch, proj, d, tm=512, tn=512):
    t = proj.shape[0]
    nj = d // tn
    o_spec = pl.BlockSpec((tm, BRANCH_W), lambda i, j: (i, 0))
    g_specs = [pl.BlockSpec((tm, tn), functools.partial(lambda i, j, r: (i, r * nj + j), r=r))
               for r in range(N_BRANCH)]
    est = 2 * (4 * tm * BRANCH_W * 2 + 4 * BRANCH_W * tn * 2 + 4 * tm * tn * 2 + tm * tn * 2) + (8 << 20)
    return pl.pallas_call(
        _merge_kernel,
        grid=(t // tm, nj),
        in_specs=[o_spec] * 4 + [pl.BlockSpec((N_BRANCH, BRANCH_W, tn), lambda i, j: (0, 0, j))] + g_specs,
        out_specs=pl.BlockSpec((tm, tn), lambda i, j: (i, j)),
        out_shape=jax.ShapeDtypeStruct((t, d), BF16),
        compiler_params=_params(("parallel", "parallel"), est),
        name="branch_merge",
    )(*branches, w_branch, proj, proj, proj, proj)


def _out_proj_kernel(a_ref, w_ref, x_ref, mod_ref, o_ref, *, g_idx):
    y = jnp.dot(a_ref[...], w_ref[...], preferred_element_type=F32)
    o_ref[...] = x_ref[...] + mod_ref[0, g_idx:g_idx + 1, :] * y


def _out_proj(merged, w_out, x2, mod, seq, g_idx, tm=512, tn=512):
    t, d = x2.shape
    per_b = seq // tm
    est = 2 * (tm * d * 2 + d * tn * 2 + 2 * tm * tn * 4) + tm * tn * 4 + (4 << 20)
    return pl.pallas_call(
        functools.partial(_out_proj_kernel, g_idx=g_idx),
        grid=(t // tm, d // tn),
        in_specs=[pl.BlockSpec((tm, d), lambda i, j: (i, 0)),
                  pl.BlockSpec((d, tn), lambda i, j: (0, j)),
                  pl.BlockSpec((tm, tn), lambda i, j: (i, j)),
                  pl.BlockSpec((1, 6, tn), lambda i, j: (i // per_b, 0, j))],
        out_specs=pl.BlockSpec((tm, tn), lambda i, j: (i, j)),
        out_shape=jax.ShapeDtypeStruct((t, d), F32),
        compiler_params=_params(("parallel", "parallel"), est),
        name="out_proj",
    )(merged, w_out, x2, mod)


def _router_kernel(x_ref, w_ref, mod_ref, rw_ref, rb_ref, h_ref, ri_ref, rwt_ref, cnt_ref, carry_scr,
                   *, sh_idx, sc_idx):
    step = pl.program_id(0) * pl.num_programs(1) + pl.program_id(1)

    @pl.when(step == 0)
    def _():
        carry_scr[...] = jnp.zeros(carry_scr.shape, F32)

    x = x_ref[0]
    y = x * lax.rsqrt(jnp.mean(x * x, axis=-1, keepdims=True) + NORM_EPS) * w_ref[...]
    m = mod_ref[0]
    h = y * (1.0 + m[sc_idx:sc_idx + 1]) + m[sh_idx:sh_idx + 1]
    h_ref[0] = h
    logits = jnp.dot(h, rw_ref[...], precision=HIGHEST, preferred_element_type=F32) + rb_ref[...]
    tm = logits.shape[0]
    lane = lax.broadcasted_iota(jnp.int32, logits.shape, 1)
    big = jnp.int32(4 * LANE)
    neg = -jnp.inf

    def first_argmax(vals):
        mx = jnp.max(vals, axis=1, keepdims=True)
        idx = jnp.min(jnp.where(vals == mx, lane, big), axis=1, keepdims=True)
        return mx, idx

    lg = jnp.where(lane < N_GROUPS, logits, neg)
    gmax, gidx = first_argmax(lg)
    g_w = 1.0 / jnp.sum(jnp.exp(lg - gmax), axis=1, keepdims=True)
    lo = N_GROUPS + gidx * EXPERTS_PER_GROUP
    in_group = jnp.where(lane >= lo, jnp.where(lane < lo + EXPERTS_PER_GROUP, 1.0, 0.0), 0.0) > 0.0
    le = jnp.where(in_group, logits, neg)
    m1, i1 = first_argmax(le)
    le2 = jnp.where(lane == i1, neg, le)
    m2, i2 = first_argmax(le2)
    denom = jnp.sum(jnp.exp(le - m1), axis=1, keepdims=True)
    p1 = 1.0 / denom
    p2 = jnp.exp(m2 - m1) / denom
    w1 = p1 / (p1 + p2) * g_w
    w2 = p2 / (p1 + p2) * g_w
    e1 = i1 - N_GROUPS
    e2 = i2 - N_GROUPS

    oh1 = jnp.where(lane == e1, 1.0, 0.0)
    oh2 = jnp.where(lane == e2, 1.0, 0.0)
    both = oh1 + oh2
    r_i = lax.broadcasted_iota(jnp.int32, (tm, tm), 0)
    c_i = lax.broadcasted_iota(jnp.int32, (tm, tm), 1)
    strict = jnp.where(r_i > c_i, 1.0, 0.0).astype(BF16)
    before = jnp.dot(strict, both.astype(BF16), preferred_element_type=F32) + carry_scr[...]
    rank1 = jnp.sum(before * oh1, axis=1, keepdims=True).astype(jnp.int32)
    rank2 = jnp.sum(before * oh2, axis=1, keepdims=True).astype(jnp.int32)
    carry_scr[...] = carry_scr[...] + jnp.sum(both, axis=0, keepdims=True)
    cnt_ref[...] = carry_scr[...]
    ri_ref[0] = jnp.where(lane == 0, e1, jnp.where(lane == 1, e2, jnp.where(
        lane == 2, rank1, jnp.where(lane == 3, rank2, 0))))
    rwt_ref[0] = jnp.where(lane == 0, w1, jnp.where(lane == 1, w2, 0.0))


def _router(x, w, mod, wg, bg, wr, br, sh_idx, sc_idx, tm=256):
    b, s, d = x.shape
    rw = jnp.zeros((d, LANE), F32).at[:, :N_GROUPS].set(wg).at[:, N_GROUPS:N_GROUPS + N_EXPERTS].set(wr)
    rb = jnp.zeros((1, LANE), F32).at[0, :N_GROUPS].set(bg).at[0, N_GROUPS:N_GROUPS + N_EXPERTS].set(br)
    return pl.pallas_call(
        functools.partial(_router_kernel, sh_idx=sh_idx, sc_idx=sc_idx),
        grid=(b, s // tm),
        in_specs=[pl.BlockSpec((1, tm, d), lambda i, j: (i, j, 0)),
                  pl.BlockSpec((1, d), lambda i, j: (0, 0)),
                  pl.BlockSpec((1, 6, d), lambda i, j: (i, 0, 0)),
                  pl.BlockSpec((d, LANE), lambda i, j: (0, 0)),
                  pl.BlockSpec((1, LANE), lambda i, j: (0, 0))],
        out_specs=[pl.BlockSpec((1, tm, d), lambda i, j: (i, j, 0)),
                   pl.BlockSpec((1, tm, LANE), lambda i, j: (i, j, 0)),
                   pl.BlockSpec((1, tm, LANE), lambda i, j: (i, j, 0)),
                   pl.BlockSpec((1, LANE), lambda i, j: (0, 0))],
        out_shape=[jax.ShapeDtypeStruct((b, s, d), F32),
                   jax.ShapeDtypeStruct((b, s, LANE), jnp.int32),
                   jax.ShapeDtypeStruct((b, s, LANE), F32),
                   jax.ShapeDtypeStruct((1, LANE), F32)],
        scratch_shapes=[pltpu.VMEM((1, LANE), F32)],
        compiler_params=_params(("arbitrary", "arbitrary"), 8 * tm * d * 4 + d * LANE * 8 + (8 << 20)),
        name="moe_router",
    )(x, w.reshape(1, d), mod, rw, rb)


def _slot_kernel(d1_ref, d2_ref, o_ref):
    def clear(i, carry):
        o_ref[i] = 0
        return carry

    lax.fori_loop(0, o_ref.shape[0], clear, 0, unroll=16)

    def place(t, carry):
        o_ref[d1_ref[t]] = t
        o_ref[d2_ref[t]] = t
        return carry

    lax.fori_loop(0, d1_ref.shape[0], place, 0, unroll=8)


def _slot_tokens(dest1, dest2, cap):
    smem = pl.BlockSpec(memory_space=pltpu.SMEM)
    return pl.pallas_call(
        _slot_kernel,
        in_specs=[smem, smem],
        out_specs=smem,
        out_shape=jax.ShapeDtypeStruct((cap,), jnp.int32),
        name="moe_slot_tokens",
    )(dest1, dest2)


def _expert_kernel(tok_ref, be_ref, nu_ref, first_ref, nxt_ref, h_hbm, wg_hbm, wu_hbm, wd_hbm, o_ref,
                   x_scr, sg_scr, su_scr, sd_scr, wg_scr, wu_scr, wd_scr, sem_x, sem_w, *, layer):
    blk = pl.program_id(0)
    n_used = nu_ref[0]
    rows = x_scr.shape[1]
    slot = lax.rem(blk, 2)

    def row_copy(b, s, i):
        tok = tok_ref[b * rows + i]
        return pltpu.make_async_copy(h_hbm.at[pl.ds(tok, 1)], x_scr.at[s, pl.ds(i, 1)], sem_x.at[s])

    def start_rows(b, s):
        def go(i, carry):
            row_copy(b, s, i).start()
            return carry

        lax.fori_loop(0, rows, go, 0, unroll=8)

    def weight_copies(e):
        return (pltpu.make_async_copy(wg_hbm.at[layer, e], sg_scr, sem_w.at[0]),
                pltpu.make_async_copy(wu_hbm.at[layer, e], su_scr, sem_w.at[1]),
                pltpu.make_async_copy(wd_hbm.at[layer, e], sd_scr, sem_w.at[2]))

    @pl.when(blk < n_used)
    def _():
        @pl.when(blk == 0)
        def _():
            start_rows(0, 0)
            for cp in weight_copies(be_ref[0]):
                cp.start()

        @pl.when(blk + 1 < n_used)
        def _():
            start_rows(blk + 1, 1 - slot)

        @pl.when(first_ref[blk] == 1)
        def _():
            for cp in weight_copies(be_ref[blk]):
                cp.wait()
            step = 512
            for r in range(0, sg_scr.shape[0], step):
                wg_scr[r:r + step, :] = sg_scr[r:r + step, :].astype(BF16)
                wu_scr[r:r + step, :] = su_scr[r:r + step, :].astype(BF16)
            for r in range(0, sd_scr.shape[1], step * 4):
                wd_scr[:, r:r + step * 4] = sd_scr[:, r:r + step * 4].astype(BF16)

            @pl.when(nxt_ref[blk] >= 0)
            def _():
                for cp in weight_copies(nxt_ref[blk]):
                    cp.start()

        def wait(i, carry):
            row_copy(blk, slot, i).wait()
            return carry

        lax.fori_loop(0, rows, wait, 0, unroll=8)
        xb = x_scr[slot].astype(BF16)
        gate = jnp.dot(xb, wg_scr[...], preferred_element_type=F32)
        up = jnp.dot(xb, wu_scr[...], preferred_element_type=F32)
        act = (_silu(gate) * up).astype(BF16)
        o_ref[...] = jnp.dot(act, wd_scr[...], preferred_element_type=F32)

    @pl.when(blk >= n_used)
    def _():
        o_ref[...] = jnp.zeros(o_ref.shape, o_ref.dtype)


def _experts(slot_tok, block_expert, n_used, first, nxt, h2, wg_stack, wu_stack, wd_stack, layer):
    t, d = h2.shape
    n_blocks = block_expert.shape[0]
    hid = wg_stack.shape[3]
    est = 3 * d * hid * 6 + 8 * MOE_BLOCK * d * 4 + (6 << 20)
    any_spec = pl.BlockSpec(memory_space=pl.ANY)
    return pl.pallas_call(
        functools.partial(_expert_kernel, layer=layer),
        grid_spec=pltpu.PrefetchScalarGridSpec(
            num_scalar_prefetch=5,
            grid=(n_blocks,),
            in_specs=[any_spec, any_spec, any_spec, any_spec],
            out_specs=pl.BlockSpec((MOE_BLOCK, d), lambda i, *_: (i, 0)),
            scratch_shapes=[pltpu.VMEM((2, MOE_BLOCK, d), F32),
                            pltpu.VMEM((d, hid), F32), pltpu.VMEM((d, hid), F32), pltpu.VMEM((hid, d), F32),
                            pltpu.VMEM((d, hid), BF16), pltpu.VMEM((d, hid), BF16), pltpu.VMEM((hid, d), BF16),
                            pltpu.SemaphoreType.DMA((2,)), pltpu.SemaphoreType.DMA((3,))]),
        out_shape=jax.ShapeDtypeStruct((n_blocks * MOE_BLOCK, d), F32),
        compiler_params=_params(("arbitrary",), est),
        name="moe_experts",
    )(slot_tok, block_expert, n_used, first, nxt, h2, wg_stack, wu_stack, wd_stack)


def _combine_kernel(d1_ref, d2_ref, ys_hbm, x_ref, rw_ref, mod_ref, fw_ref, o_ref, a_scr, b_scr, sem,
                    *, g_idx, final):
    i = pl.program_id(0)
    rows = a_scr.shape[1]
    slot = lax.rem(i, 2)

    def copies(step, s, r):
        t = step * rows + r
        return (pltpu.make_async_copy(ys_hbm.at[pl.ds(d1_ref[t], 1)], a_scr.at[s, pl.ds(r, 1)], sem.at[0, s]),
                pltpu.make_async_copy(ys_hbm.at[pl.ds(d2_ref[t], 1)], b_scr.at[s, pl.ds(r, 1)], sem.at[1, s]))

    def start_rows(step, s):
        def go(r, carry):
            ca, cb = copies(step, s, r)
            ca.start()
            cb.start()
            return carry

        lax.fori_loop(0, rows, go, 0, unroll=8)

    @pl.when(i == 0)
    def _():
        start_rows(0, 0)

    @pl.when(i + 1 < pl.num_programs(0))
    def _():
        start_rows(i + 1, 1 - slot)

    def wait(r, carry):
        ca, cb = copies(i, slot, r)
        ca.wait()
        cb.wait()
        return carry

    lax.fori_loop(0, rows, wait, 0, unroll=8)
    rw = rw_ref[...]
    moe = rw[:, 0:1] * a_scr[slot] + rw[:, 1:2] * b_scr[slot]
    xn = x_ref[...] + mod_ref[0, g_idx:g_idx + 1, :] * moe
    if final:
        xn = xn * lax.rsqrt(jnp.mean(xn * xn, axis=-1, keepdims=True) + NORM_EPS) * fw_ref[...]
    o_ref[...] = xn


def _combine(dest1, dest2, ys, x2, route_w, mod, final_w, seq, g_idx, final, tm=128):
    t, d = x2.shape
    per_b = seq // tm
    est = 2 * (2 * tm * d * 4 + tm * LANE * 4) + 4 * tm * d * 4 + (6 << 20)
    return pl.pallas_call(
        functools.partial(_combine_kernel, g_idx=g_idx, final=final),
        grid_spec=pltpu.PrefetchScalarGridSpec(
            num_scalar_prefetch=2,
            grid=(t // tm,),
            in_specs=[pl.BlockSpec(memory_space=pl.ANY),
                      pl.BlockSpec((tm, d), lambda i, d1, d2: (i, 0)),
                      pl.BlockSpec((tm, LANE), lambda i, d1, d2: (i, 0)),
                      pl.BlockSpec((1, 6, d), lambda i, d1, d2: (i // per_b, 0, 0)),
                      pl.BlockSpec((1, d), lambda i, d1, d2: (0, 0))],
            out_specs=pl.BlockSpec((tm, d), lambda i, d1, d2: (i, 0)),
            scratch_shapes=[pltpu.VMEM((2, tm, d), F32), pltpu.VMEM((2, tm, d), F32),
                            pltpu.SemaphoreType.DMA((2, 2))]),
        out_shape=jax.ShapeDtypeStruct((t, d), F32),
        compiler_params=_params(("arbitrary",), est),
        name="moe_combine",
    )(dest1, dest2, ys, x2, route_w, mod, final_w.reshape(1, d))


def _rot_cols(w):
    half = w.shape[1] // 2
    return jnp.concatenate([-w[:, half:], w[:, :half]], axis=1)


def _small_in_weights(w_in, d):
    bounds = np.cumsum((0,) + IN_SIZES_HEAD)
    seg = [w_in[:, bounds[i]:bounds[i + 1]] for i in range(len(IN_SIZES_HEAD))]
    dt, q_lat, kv_lat, k_rope, sq, sk, sv = seg[3:]
    z64 = jnp.zeros((d, SWA_HEAD_DIM), w_in.dtype)

    def spread(w):
        h0, h1 = w[:, :SWA_HEAD_DIM], w[:, SWA_HEAD_DIM:]
        return jnp.concatenate([h0, z64, z64, h0, h1, z64, z64, h1], axis=1)

    small = jnp.concatenate([sq, spread(sk), spread(sv), kv_lat,
                             jnp.zeros((d, OFF_S["qlat"] - OFF_S["kvlat"] - MLA_KV_LORA), w_in.dtype),
                             q_lat], axis=1).astype(BF16)
    tail = jnp.concatenate([k_rope, z64, _rot_cols(k_rope), z64, dt,
                            jnp.zeros((d, LANE - SSM_HEADS), w_in.dtype)], axis=1).astype(BF16)
    return small, tail


def _extend_wq(wq_b):
    k = wq_b.shape[0]
    w = wq_b.reshape(k, MLA_HEADS, MLA_NOPE + MLA_ROPE)
    z = jnp.zeros((k, MLA_HEADS, LANE - MLA_ROPE), wq_b.dtype)
    rope = w[..., MLA_NOPE:]
    half = MLA_ROPE // 2
    rot = jnp.concatenate([-rope[..., half:], rope[..., :half]], axis=-1)
    return jnp.concatenate([w[..., :MLA_NOPE], rope, z, rot, z], axis=-1).reshape(k, -1).astype(BF16)


def _mixer(x, mod, pos_col, pos_row, cos_t, sin_t, norm_w, w_in_stack, layer, conv_w, conv_b, dt_bias, a_log,
           d_skip, ssm_norm, mla_q_norm, mla_wq_b, mla_kv_norm, mla_wkv_b, swa_sinks, w_branch, w_out):
    b, s, d = x.shape
    t = b * s
    swa_slopes, moba_slopes = _alibi_slopes()
    h = _norm_mod(x, norm_w, mod, sh_idx=0, sc_idx=1)
    h2 = h.reshape(t, d)
    w_small, w_tail = _small_in_weights(w_in_stack[layer], d)
    proj_a = _wproj(h2, w_in_stack, layer, 0, A_COLS, "in_proj_head")
    proj_g = _wproj(h2, w_in_stack, layer, GATE_COL0, N_BRANCH * d, "in_proj_gates")
    proj_s = _matmul(h2, w_small, BF16, tm=1024, tn=768, name="in_proj_small")
    tail = _matmul(h2, w_tail, F32, tm=1024, tn=3 * LANE, name="in_proj_tail")
    proj_a3 = proj_a.reshape(b, s, A_COLS)
    proj_s3 = proj_s.reshape(b, s, S_COLS)
    tail3 = tail.reshape(b, s, 3 * LANE)

    o_moba = _flash(proj_a3, proj_a3, proj_a3, heads=MOBA_HEADS, dq=MOBA_HEAD_DIM, dv=MOBA_HEAD_DIM,
                    q_off=OFF_A["mq"] // LANE, k_off=OFF_A["mk"] // LANE, v_off=OFF_A["mv"] // LANE,
                    moba=True, q_scale=float(MOBA_HEAD_DIM ** -0.5), slopes=jnp.asarray(moba_slopes),
                    pos_col=pos_col, pos_row=pos_row, name="moba_attention")
    o_ssm = _ssd(proj_a3, tail3, OFF_A, conv_w, conv_b, dt_bias, a_log, d_skip, ssm_norm)
    q_m, k_m, v_m = _mla_project(proj_s, tail, OFF_S, mla_q_norm, _extend_wq(mla_wq_b), mla_kv_norm,
                                 mla_wkv_b.astype(BF16), cos_t, sin_t)
    o_mla = _flash(q_m.reshape(b, s, -1), k_m.reshape(b, s, -1), v_m.reshape(b, s, -1),
                   heads=MLA_HEADS, dq=MLA_QK, dv=MLA_V, q_off=0, k_off=0, v_off=0, name="mla_attention")
    o_swa = _swa(proj_s3, OFF_S, swa_sinks, pos_col, pos_row, swa_slopes)

    branches = [o.reshape(t, BRANCH_W) for o in (o_moba, o_ssm, o_mla, o_swa)]
    merged = _merge(branches, w_branch.astype(BF16), proj_g, d)
    return _out_proj(merged, w_out.astype(BF16), x.reshape(t, d), mod, s, g_idx=2)


def _moe(x2, seq, mod, norm_w, wg, bg, wr, br, w_gate, w_up, w_down, layer, final_w, final):
    t, d = x2.shape
    b = t // seq
    h, route_i, route_w, counts = _router(x2.reshape(b, seq, d), norm_w, mod, wg, bg, wr, br,
                                          sh_idx=3, sc_idx=4)
    route_i = route_i.reshape(t, LANE)
    counts = counts[0, :N_EXPERTS].astype(jnp.int32)
    padded = (counts + MOE_BLOCK - 1) // MOE_BLOCK * MOE_BLOCK
    pad_end = jnp.cumsum(padded)
    pad_start = pad_end - padded
    n_blocks = -(-(t * 2) // MOE_BLOCK) + N_EXPERTS
    dest1 = pad_start[route_i[:, 0]] + route_i[:, 2]
    dest2 = pad_start[route_i[:, 1]] + route_i[:, 3]
    block_row0 = jnp.arange(n_blocks, dtype=jnp.int32) * MOE_BLOCK
    block_expert = jnp.minimum(jnp.sum((pad_end[None, :] <= block_row0[:, None]).astype(jnp.int32), axis=1),
                               N_EXPERTS - 1)
    n_used = (pad_end[-1:] // MOE_BLOCK).astype(jnp.int32)
    first = (block_expert != jnp.concatenate([jnp.full((1,), -1, jnp.int32), block_expert[:-1]])).astype(jnp.int32)
    e_ids = jnp.arange(N_EXPERTS, dtype=jnp.int32)
    later = jnp.where((counts > 0)[None, :] & (e_ids[None, :] > e_ids[:, None]), e_ids[None, :], N_EXPERTS)
    nxt_e = jnp.min(later, axis=1)
    nxt = jnp.where(nxt_e == N_EXPERTS, -1, nxt_e)[block_expert].astype(jnp.int32)
    slot_tok = _slot_tokens(dest1, dest2, n_blocks * MOE_BLOCK)
    ys = _experts(slot_tok, block_expert, n_used, first, nxt, h.reshape(t, d), w_gate, w_up, w_down, layer)
    return _combine(dest1, dest2, ys, x2, route_w.reshape(t, LANE), mod, final_w, seq, g_idx=5, final=final)


def kernel(x, c, positions, ada_w, ada_b, norm_mix, norm_ffn, w_in, conv_w, conv_b, dt_bias, a_log, d_skip,
           ssm_norm, mla_q_norm, mla_wq_b, mla_kv_norm, mla_wkv_b, swa_sinks, w_branch, w_out,
           router_group_w, router_group_b, router_w, router_b, exp_w_gate, exp_w_up, exp_w_down, final_norm):
    b, s, d = x.shape
    depth = ada_w.shape[0]
    mods = _ada_mod(c, ada_w, ada_b)
    pos_f = positions.astype(F32)
    pos_col = pos_f.reshape(b, s, 1)
    pos_row = pos_f.reshape(b, 1, s)
    cos_t, sin_t = _rope_tables(pos_f.reshape(b * s, 1))
    for l in range(depth):
        x2 = _mixer(x, mods[l], pos_col, pos_row, cos_t, sin_t, norm_mix[l], w_in, l, conv_w[l], conv_b[l],
                    dt_bias[l], a_log[l], d_skip[l], ssm_norm[l], mla_q_norm[l], mla_wq_b[l],
                    mla_kv_norm[l], mla_wkv_b[l], swa_sinks[l], w_branch[l], w_out[l])
        x2 = _moe(x2, s, mods[l], norm_ffn[l], router_group_w[l], router_group_b[l], router_w[l],
                  router_b[l], exp_w_gate, exp_w_up, exp_w_down, l, final_norm,
                  final=(l == depth - 1))
        x = x2.reshape(b, s, d)
    return x
```

```python
import functools
import math

import numpy as np
import jax
import jax.numpy as jnp
from jax import lax
from jax.experimental import pallas as pl
from jax.experimental.pallas import tpu as pltpu

F32 = jnp.float32
BF16 = jnp.bfloat16
HIGHEST = lax.Precision.HIGHEST

MOBA_HEADS = 8
MOBA_HEAD_DIM = 128
MOBA_BLOCK = 256
MOBA_TOPK = 3
SSM_D_INNER = 1024
SSM_HEAD_DIM = 64
SSM_HEADS = SSM_D_INNER // SSM_HEAD_DIM
SSM_GROUPS = 2
SSM_STATE = 128
SSM_CONV = 4
SSM_CHUNK = 128
SSM_BC = 2 * SSM_GROUPS * SSM_STATE
MLA_HEADS = 8
MLA_Q_LORA = 768
MLA_KV_LORA = 512
MLA_NOPE = 128
MLA_ROPE = 64
MLA_V = 128
ROPE_THETA = 10000.0
SWA_HEADS = 16
SWA_KV_HEADS = 2
SWA_HEAD_DIM = 64
SWA_WINDOW = 128
N_BRANCH = 4
BRANCH_W = 1024
N_GROUPS = 4
EXPERTS_PER_GROUP = 8
N_EXPERTS = N_GROUPS * EXPERTS_PER_GROUP
EXPERT_HIDDEN = 512
EXP_BLOCK = 256
NORM_EPS = 1e-6
N_ALIBI = MOBA_HEADS + SWA_HEADS

LANE = 128
VMEM_CAP = 60 * 1024 * 1024
ATTN_TILE = 512
FLASH_HEADS = 2
MLA_QK = 256


def _alibi_slopes():
    i = np.arange(1, N_ALIBI + 1, dtype=np.float64)
    s = np.exp2(-8.0 * i / N_ALIBI).astype(np.float32)
    return s[:SWA_HEADS], s[SWA_HEADS:]


IN_SIZES_HEAD = (3 * MOBA_HEADS * MOBA_HEAD_DIM, SSM_D_INNER, SSM_D_INNER + SSM_BC, SSM_HEADS, MLA_Q_LORA,
                 MLA_KV_LORA, MLA_ROPE, SWA_HEADS * SWA_HEAD_DIM, SWA_KV_HEADS * SWA_HEAD_DIM,
                 SWA_KV_HEADS * SWA_HEAD_DIM)
A_COLS = IN_SIZES_HEAD[0] + IN_SIZES_HEAD[1] + IN_SIZES_HEAD[2]
GATE_COL0 = sum(IN_SIZES_HEAD)
OFF_A = dict(mq=0, mk=1024, mv=2048, z=3072, xs=4096, bc=5120)
OFF_S = dict(sq=0, skx=1024, svx=1536, kvlat=2048, qlat=3072)
S_COLS = 3840


def _params(sem, est_bytes):
    limit = int(min(VMEM_CAP, max(est_bytes, 16 * 1024 * 1024)))
    return pltpu.CompilerParams(dimension_semantics=sem, vmem_limit_bytes=limit)


def _silu(v):
    return v * (1.0 / (1.0 + jnp.exp(-v)))


def _pack_bf16_pair(lo, hi):
    lo_bits = pltpu.bitcast(lo.astype(BF16).astype(F32), jnp.uint32)
    hi_bits = pltpu.bitcast(hi.astype(BF16).astype(F32), jnp.uint32)
    return hi_bits | lax.shift_right_logical(lo_bits, jnp.uint32(16))


def _unpack_bf16_pair(word):
    lo = pltpu.bitcast(lax.shift_left(word, jnp.uint32(16)), F32)
    hi = pltpu.bitcast(word & jnp.uint32(0xFFFF0000), F32)
    return lo, hi


def _ada_kernel(c_ref, w_ref, b_ref, o_ref):
    k = pl.program_id(1)

    @pl.when(k == 0)
    def _():
        o_ref[0] = jnp.broadcast_to(b_ref[0], o_ref.shape[1:])

    o_ref[0] += jnp.dot(c_ref[...].astype(BF16), w_ref[0].astype(BF16), preferred_element_type=F32)


def _ada_mod(c, ada_w, ada_b):
    depth, d, n = ada_w.shape
    b = c.shape[0]
    c8 = jnp.zeros((8, d), F32).at[:b].set(c)
    tk = LANE
    out = pl.pallas_call(
        _ada_kernel,
        grid=(depth, d // tk),
        in_specs=[pl.BlockSpec((8, tk), lambda l, k: (0, k)),
                  pl.BlockSpec((1, tk, n), lambda l, k: (l, k, 0)),
                  pl.BlockSpec((1, 1, n), lambda l, k: (l, 0, 0))],
        out_specs=pl.BlockSpec((1, 8, n), lambda l, k: (l, 0, 0)),
        out_shape=jax.ShapeDtypeStruct((depth, 8, n), F32),
        compiler_params=_params(("parallel", "arbitrary"), 3 * tk * n * 4 + 32 * n * 4 + (4 << 20)),
        name="ada_mod",
    )(c8, ada_w, ada_b.reshape(depth, 1, n))
    return out[:, :b].reshape(depth, b, 6, d)


def _norm_mod_kernel(x_ref, w_ref, mod_ref, o_ref, *, sh_idx, sc_idx):
    x = x_ref[0]
    y = x * lax.rsqrt(jnp.mean(x * x, axis=-1, keepdims=True) + NORM_EPS) * w_ref[...]
    m = mod_ref[0]
    o_ref[0] = (y * (1.0 + m[sc_idx:sc_idx + 1]) + m[sh_idx:sh_idx + 1]).astype(o_ref.dtype)


def _norm_mod(x, w, mod, sh_idx, sc_idx, tm=256):
    b, s, d = x.shape
    return pl.pallas_call(
        functools.partial(_norm_mod_kernel, sh_idx=sh_idx, sc_idx=sc_idx),
        grid=(b, s // tm),
        in_specs=[pl.BlockSpec((1, tm, d), lambda i, j: (i, j, 0)),
                  pl.BlockSpec((1, d), lambda i, j: (0, 0)),
                  pl.BlockSpec((1, 6, d), lambda i, j: (i, 0, 0))],
        out_specs=pl.BlockSpec((1, tm, d), lambda i, j: (i, j, 0)),
        out_shape=jax.ShapeDtypeStruct((b, s, d), BF16),
        compiler_params=_params(("parallel", "parallel"), 6 * tm * d * 4 + (4 << 20)),
        name="norm_mod",
    )(x, w.reshape(1, d), mod)


def _wproj_kernel(a_ref, w_ref, o_ref, wbf_scr):
    tn = wbf_scr.shape[0]
    rows = 64

    @pl.when(pl.program_id(1) == 0)
    def _():
        for r in range(0, tn, rows):
            wbf_scr[r:r + rows, :] = w_ref[0, r:r + rows, :].astype(BF16)

    o_ref[...] = lax.dot_general(a_ref[...], wbf_scr[...], (((1,), (1,)), ((), ())),
                                 preferred_element_type=F32).astype(o_ref.dtype)


def _wproj(a, wt_stack, layer, col0, n_cols, name, tm=1024, tn=512, out_dtype=BF16):
    m, k = a.shape
    assert n_cols % tn == 0 and m % tm == 0 and col0 % 8 == 0
    if col0 % tn == 0:
        w_spec = pl.BlockSpec((1, tn, k), lambda j, i: (layer, col0 // tn + j, 0))
    else:
        w_spec = pl.BlockSpec((pl.Element(1), pl.Element(tn), pl.Element(k)),
                              lambda j, i: (layer, pl.multiple_of(col0 + j * tn, 8), 0))
    est = 2 * (tm * k * 2 + tn * k * 4 + tm * tn * 2) + k * tn * 2 + tm * tn * 4 + (6 << 20)
    return pl.pallas_call(
        _wproj_kernel,
        grid=(n_cols // tn, m // tm),
        in_specs=[pl.BlockSpec((tm, k), lambda j, i: (i, 0)), w_spec],
        out_specs=pl.BlockSpec((tm, tn), lambda j, i: (i, j)),
        out_shape=jax.ShapeDtypeStruct((m, n_cols), out_dtype),
        scratch_shapes=[pltpu.VMEM((tn, k), BF16)],
        compiler_params=_params(("parallel", "arbitrary"), est),
        name=name,
    )(a, wt_stack)


def _rope_kernel(pos_ref, freq_ref, cos_ref, sin_ref):
    ang = pos_ref[...] * freq_ref[...]
    lane = lax.broadcasted_iota(jnp.int32, ang.shape, 1)
    live = lane < MLA_ROPE
    cos_ref[...] = jnp.where(live, jnp.cos(ang), 0.0)
    sin_ref[...] = jnp.where(live, jnp.sin(ang), 0.0)


def _rope_tables(pos_col):
    t = pos_col.shape[0]
    half = MLA_ROPE // 2
    inv = ROPE_THETA ** (-np.arange(half, dtype=np.float32) / half)
    freq = np.zeros((1, LANE), np.float32)
    freq[0, :half] = inv
    freq[0, half:2 * half] = inv
    tm = 512
    return pl.pallas_call(
        _rope_kernel,
        grid=(t // tm,),
        in_specs=[pl.BlockSpec((tm, 1), lambda i: (i, 0)),
                  pl.BlockSpec((1, LANE), lambda i: (0, 0))],
        out_specs=[pl.BlockSpec((tm, LANE), lambda i: (i, 0))] * 2,
        out_shape=[jax.ShapeDtypeStruct((t, LANE), F32)] * 2,
        compiler_params=_params(("parallel",), 16 << 20),
        name="rope_tables",
    )(pos_col, jnp.asarray(freq))


def _mla_q_kernel(x_ref, nw_ref, w_ref, cos_ref, sin_ref, o_ref, *, scale):
    x = x_ref[...].astype(F32)
    y = x * lax.rsqrt(jnp.mean(x * x, axis=-1, keepdims=True) + NORM_EPS) * nw_ref[...]
    r = jnp.dot(y.astype(BF16), w_ref[...], preferred_element_type=F32)
    cos = cos_ref[...]
    sin = sin_ref[...]
    for h in range(MLA_HEADS):
        base = h * 3 * LANE
        nope = r[:, base:base + LANE]
        rope = r[:, base + LANE:base + 2 * LANE]
        rot = r[:, base + 2 * LANE:base + 3 * LANE]
        o_ref[:, h * MLA_QK:h * MLA_QK + LANE] = (nope * scale).astype(o_ref.dtype)
        o_ref[:, h * MLA_QK + LANE:(h + 1) * MLA_QK] = (
            (rope * cos + rot * sin) * scale).astype(o_ref.dtype)


def _mla_kv_kernel(x_ref, nw_ref, w_ref, kr_ref, krot_ref, cos_ref, sin_ref, k_ref, v_ref):
    x = x_ref[...].astype(F32)
    y = x * lax.rsqrt(jnp.mean(x * x, axis=-1, keepdims=True) + NORM_EPS) * nw_ref[...]
    r = jnp.dot(y.astype(BF16), w_ref[...], preferred_element_type=F32)
    kr = (kr_ref[...] * cos_ref[...] + krot_ref[...] * sin_ref[...]).astype(k_ref.dtype)
    for h in range(MLA_HEADS):
        base = h * (MLA_NOPE + MLA_V)
        k_ref[:, h * MLA_QK:h * MLA_QK + LANE] = r[:, base:base + MLA_NOPE].astype(k_ref.dtype)
        k_ref[:, h * MLA_QK + LANE:(h + 1) * MLA_QK] = kr
        v_ref[:, h * MLA_V:(h + 1) * MLA_V] = r[:, base + MLA_NOPE:base + MLA_NOPE + MLA_V].astype(
            v_ref.dtype)


def _mla_project(proj, tail, off, q_norm, wq_ext, kv_norm, wkv, cos_t, sin_t, tm=512):
    t = proj.shape[0]
    scale = float((MLA_NOPE + MLA_ROPE) ** -0.5)
    q = pl.pallas_call(
        functools.partial(_mla_q_kernel, scale=scale),
        grid=(t // tm,),
        in_specs=[pl.BlockSpec((tm, MLA_Q_LORA), lambda i: (i, off["qlat"] // MLA_Q_LORA)),
                  pl.BlockSpec((1, MLA_Q_LORA), lambda i: (0, 0)),
                  pl.BlockSpec(wq_ext.shape, lambda i: (0, 0)),
                  pl.BlockSpec((tm, LANE), lambda i: (i, 0)),
                  pl.BlockSpec((tm, LANE), lambda i: (i, 0))],
        out_specs=pl.BlockSpec((tm, MLA_HEADS * MLA_QK), lambda i: (i, 0)),
        out_shape=jax.ShapeDtypeStruct((t, MLA_HEADS * MLA_QK), BF16),
        compiler_params=_params(("parallel",), 40 << 20),
        name="mla_q_proj",
    )(proj, q_norm.reshape(1, -1), wq_ext, cos_t, sin_t)
    k, v = pl.pallas_call(
        _mla_kv_kernel,
        grid=(t // tm,),
        in_specs=[pl.BlockSpec((tm, MLA_KV_LORA), lambda i: (i, off["kvlat"] // MLA_KV_LORA)),
                  pl.BlockSpec((1, MLA_KV_LORA), lambda i: (0, 0)),
                  pl.BlockSpec(wkv.shape, lambda i: (0, 0)),
                  pl.BlockSpec((tm, LANE), lambda i: (i, 0)),
                  pl.BlockSpec((tm, LANE), lambda i: (i, 1)),
                  pl.BlockSpec((tm, LANE), lambda i: (i, 0)),
                  pl.BlockSpec((tm, LANE), lambda i: (i, 0))],
        out_specs=[pl.BlockSpec((tm, MLA_HEADS * MLA_QK), lambda i: (i, 0)),
                   pl.BlockSpec((tm, MLA_HEADS * MLA_V), lambda i: (i, 0))],
        out_shape=[jax.ShapeDtypeStruct((t, MLA_HEADS * MLA_QK), BF16),
                   jax.ShapeDtypeStruct((t, MLA_HEADS * MLA_V), BF16)],
        compiler_params=_params(("parallel",), 40 << 20),
        name="mla_kv_proj",
    )(proj, kv_norm.reshape(1, -1), wkv, tail, tail, cos_t, sin_t)
    return q, k, v


def _flash_kernel(*refs, moba, q_scale, dq, dv):
    if moba:
        slope_ref, q_ref, k_ref, v_ref, posk_ref, posq_ref, o_ref, vt_scr, kmean_scr, sel_scr = refs
    else:
        q_ref, k_ref, v_ref, o_ref, vt_scr = refs
    tile = ATTN_TILE
    seq = k_ref.shape[1]
    sub = tile // MOBA_BLOCK
    nblk = seq // MOBA_BLOCK
    qi = pl.program_id(2)
    heads = range(FLASH_HEADS)

    @pl.when(qi == 0)
    def _():
        for j in heads:
            for c in range(seq // tile):
                vt_scr[j, c] = v_ref[0, c * tile:(c + 1) * tile, j * dv:(j + 1) * dv].astype(F32).T.astype(BF16)
            if moba:
                for n in range(nblk):
                    kblk = k_ref[0, n * MOBA_BLOCK:(n + 1) * MOBA_BLOCK, j * dq:(j + 1) * dq].astype(F32)
                    kmean_scr[j, n:n + 1, :] = jnp.mean(kblk, axis=0, keepdims=True)

    q = []
    for j in heads:
        q_raw = q_ref[0, :, j * dq:(j + 1) * dq]
        q.append(q_raw if q_scale is None else (q_raw.astype(F32) * q_scale).astype(BF16))
        if moba:
            gate = lax.dot_general(kmean_scr[j], q_raw.astype(F32), (((1,), (1,)), ((), ())),
                                   precision=HIGHEST, preferred_element_type=F32)
            n_iota = lax.broadcasted_iota(jnp.int32, gate.shape, 0)
            q_blk = qi * sub + lax.broadcasted_iota(jnp.int32, gate.shape, 1) // MOBA_BLOCK
            beaten = jnp.zeros(gate.shape, F32)
            for m in range(nblk):
                gm = gate[m:m + 1, :]
                wins = jnp.where(gm > gate, 1.0, jnp.where(gm == gate, jnp.where(m < n_iota, 1.0, 0.0), 0.0))
                beaten = beaten + jnp.where(m < q_blk, wins, 0.0)
            sel_scr[j] = jnp.where(beaten < MOBA_TOPK, jnp.where(n_iota < q_blk, 1.0, 0.0), 0.0)
    if moba:
        pq = posq_ref[0]

    def scores(j, c):
        start = pl.multiple_of(c * tile, tile)
        kb = k_ref[0, pl.ds(start, tile), j * dq:(j + 1) * dq]
        st = lax.dot_general(kb, q[j], (((1,), (1,)), ((), ())), preferred_element_type=F32)
        if moba:
            pk = posk_ref[0, pl.ds(start, tile), :]
            st = st - slope_ref[pl.program_id(1) * FLASH_HEADS + j] * jnp.abs(pq - pk)
        return st

    def selected(j, c):
        rows = [jnp.broadcast_to(sel_scr[j, pl.ds(c * sub + r, 1), :], (MOBA_BLOCK, tile)) for r in range(sub)]
        return jnp.concatenate(rows, axis=0)

    key_i = lax.broadcasted_iota(jnp.int32, (tile, tile), 0)
    qry_i = lax.broadcasted_iota(jnp.int32, (tile, tile), 1)
    causal = key_i <= qry_i
    init = []
    for j in heads:
        st = scores(j, qi)
        if moba:
            same_blk = (key_i // MOBA_BLOCK) == (qry_i // MOBA_BLOCK)
            keep = jnp.where(same_blk, jnp.where(causal, 1.0, 0.0), selected(j, qi)) > 0.0
        else:
            keep = causal
        st = jnp.where(keep, st, -jnp.inf)
        m0 = jnp.max(st, axis=0, keepdims=True)
        p = jnp.exp(st - m0)
        l0 = jnp.sum(p, axis=0, keepdims=True)
        init.append((m0, l0, jnp.dot(vt_scr[j, qi], p.astype(BF16), preferred_element_type=F32)))

    def body(c, carry):
        out = []
        for j in heads:
            m_prev, l_prev, acc = carry[j]
            s_c = scores(j, c)
            if moba:
                s_c = jnp.where(selected(j, c) > 0.0, s_c, -jnp.inf)
            m_new = jnp.maximum(m_prev, jnp.max(s_c, axis=0, keepdims=True))
            alpha = jnp.exp(m_prev - m_new)
            p_c = jnp.exp(s_c - m_new)
            l_new = alpha * l_prev + jnp.sum(p_c, axis=0, keepdims=True)
            acc_new = alpha * acc + jnp.dot(vt_scr[j, c], p_c.astype(BF16), preferred_element_type=F32)
            out.append((m_new, l_new, acc_new))
        return tuple(out)

    final = lax.fori_loop(0, qi, body, tuple(init))
    for j in heads:
        _, l_fin, acc = final[j]
        o_ref[0, :, j * dv:(j + 1) * dv] = (acc * (1.0 / l_fin)).T.astype(o_ref.dtype)


def _flash(q, k, v, *, heads, dq, dv, q_off, k_off, v_off, moba=False, q_scale=None,
           slopes=None, pos_col=None, pos_row=None, name="flash"):
    b, s, _ = q.shape
    tile = ATTN_TILE
    hp = FLASH_HEADS
    assert heads % hp == 0 and q_off % (hp * dq) == 0 and k_off % (hp * dq) == 0 and v_off % (hp * dv) == 0
    qb, kb, vb = q_off // (hp * dq), k_off // (hp * dq), v_off // (hp * dv)
    in_specs = [pl.BlockSpec((1, tile, hp * dq), lambda bi, h, i: (bi, i, qb + h)),
                pl.BlockSpec((1, s, hp * dq), lambda bi, h, i: (bi, 0, kb + h)),
                pl.BlockSpec((1, s, hp * dv), lambda bi, h, i: (bi, 0, vb + h))]
    args = [q, k, v]
    scratch = [pltpu.VMEM((hp, s // tile, dv, tile), BF16)]
    if moba:
        in_specs = [pl.BlockSpec(memory_space=pltpu.SMEM)] + in_specs + [
            pl.BlockSpec((1, s, 1), lambda bi, h, i: (bi, 0, 0)),
            pl.BlockSpec((1, 1, tile), lambda bi, h, i: (bi, 0, i))]
        args = [slopes] + args + [pos_col, pos_row]
        scratch += [pltpu.VMEM((hp, s // MOBA_BLOCK, dq), F32), pltpu.VMEM((hp, s // MOBA_BLOCK, tile), F32)]
    est = hp * (2 * (s * dq * 2 + s * dv * 2) + s * dv * 2) + s * LANE * 4 * 2 + (20 << 20)
    return pl.pallas_call(
        functools.partial(_flash_kernel, moba=moba, q_scale=q_scale, dq=dq, dv=dv),
        grid=(b, heads // hp, s // tile),
        in_specs=in_specs,
        out_specs=pl.BlockSpec((1, tile, hp * dv), lambda bi, h, i: (bi, i, h)),
        out_shape=jax.ShapeDtypeStruct((b, s, heads * dv), BF16),
        scratch_shapes=scratch,
        compiler_params=_params(("parallel", "parallel", "arbitrary"), est),
        name=name,
    )(*args)


def _swa_kernel(sink_ref, q_ref, kp_ref, kc_ref, vp_ref, vc_ref, pq_ref, pkp_ref, pkc_ref, o_ref,
                *, slopes):
    w = SWA_WINDOW
    n = pl.program_id(1)
    kx = jnp.concatenate([kp_ref[0], kc_ref[0]], axis=0)
    vx = jnp.concatenate([vp_ref[0], vc_ref[0]], axis=0)
    pk = jnp.concatenate([pkp_ref[0], pkc_ref[0]], axis=1)
    dist = jnp.abs(pq_ref[0] - pk)
    qi = lax.broadcasted_iota(jnp.int32, dist.shape, 0)
    kk = lax.broadcasted_iota(jnp.int32, dist.shape, 1)
    allowed = jnp.where(kk > qi, jnp.where(kk <= qi + w, 1.0, 0.0), 0.0)
    allowed = jnp.where(kk >= w, allowed, jnp.where(n > 0, allowed, 0.0)) > 0.0
    scale = float(SWA_HEAD_DIM ** -0.5)
    rep = SWA_HEADS // SWA_KV_HEADS
    for pair in range(SWA_HEADS // 2):
        qp = q_ref[0, :, pair * LANE:(pair + 1) * LANE]
        acc = jnp.zeros((w, LANE), F32)
        for half in range(2):
            h = 2 * pair + half
            col = (2 * (h // rep) + half) * LANE
            s = lax.dot_general(qp, kx[:, col:col + LANE], (((1,), (1,)), ((), ())),
                                preferred_element_type=F32)
            s = s * scale - float(slopes[h]) * dist
            s = jnp.where(allowed, s, -jnp.inf)
            sink = sink_ref[h]
            m = jnp.maximum(jnp.max(s, axis=1, keepdims=True), sink)
            p = jnp.exp(s - m)
            denom = jnp.sum(p, axis=1, keepdims=True) + jnp.exp(sink - m)
            p = p * (1.0 / denom)
            acc = acc + jnp.dot(p.astype(BF16), vx[:, col:col + LANE], preferred_element_type=F32)
        o_ref[0, :, pair * LANE:(pair + 1) * LANE] = acc.astype(o_ref.dtype)


def _swa(proj3, off, sinks, pos_col, pos_row, slopes):
    b, s, _ = proj3.shape
    w = SWA_WINDOW
    qw = SWA_HEADS * SWA_HEAD_DIM
    kw = 4 * LANE
    prev = lambda j: jnp.maximum(j - 1, 0)
    return pl.pallas_call(
        functools.partial(_swa_kernel, slopes=tuple(float(v) for v in slopes)),
        grid=(b, s // w),
        in_specs=[pl.BlockSpec(memory_space=pltpu.SMEM),
                  pl.BlockSpec((1, w, qw), lambda i, j: (i, j, off["sq"] // qw)),
                  pl.BlockSpec((1, w, kw), lambda i, j: (i, prev(j), off["skx"] // kw)),
                  pl.BlockSpec((1, w, kw), lambda i, j: (i, j, off["skx"] // kw)),
                  pl.BlockSpec((1, w, kw), lambda i, j: (i, prev(j), off["svx"] // kw)),
                  pl.BlockSpec((1, w, kw), lambda i, j: (i, j, off["svx"] // kw)),
                  pl.BlockSpec((1, w, 1), lambda i, j: (i, j, 0)),
                  pl.BlockSpec((1, 1, w), lambda i, j: (i, 0, prev(j))),
                  pl.BlockSpec((1, 1, w), lambda i, j: (i, 0, j))],
        out_specs=pl.BlockSpec((1, w, qw), lambda i, j: (i, j, 0)),
        out_shape=jax.ShapeDtypeStruct((b, s, qw), BF16),
        compiler_params=_params(("parallel", "parallel"), 24 << 20),
        name="swa_attention",
    )(sinks, proj3, proj3, proj3, proj3, proj3, pos_col, pos_row, pos_row)


def _ssd_kernel(xs_ref, bc_ref, z_ref, dt_ref, cwx_ref, cwb_ref, cbx_ref, cbb_ref, dtb_ref,
                alog_ref, dsk_ref, nw_ref, exp_ref, o_ref, padx_scr, padb_scr, st_scr):
    ch = SSM_CHUNK
    c = pl.program_id(1)

    @pl.when(c == 0)
    def _():
        padx_scr[0:8, :] = jnp.zeros((8, padx_scr.shape[1]), F32)
        padb_scr[0:8, :] = jnp.zeros((8, padb_scr.shape[1]), F32)
        st_scr[...] = jnp.zeros(st_scr.shape, F32)

    padx_scr[8:8 + ch, :] = xs_ref[0].astype(F32)
    padb_scr[8:8 + ch, :] = bc_ref[0].astype(F32)

    def conv(pad_scr, w_ref, b_ref):
        acc = b_ref[...] + w_ref[0:1, :] * pad_scr[5:5 + ch, :]
        for k in range(1, SSM_CONV):
            acc = acc + w_ref[k:k + 1, :] * pad_scr[5 + k:5 + k + ch, :]
        return _silu(acc)

    xs = conv(padx_scr, cwx_ref, cbx_ref)
    bcv = conv(padb_scr, cwb_ref, cbb_ref)
    padx_scr[0:8, :] = padx_scr[ch:ch + 8, :]
    padb_scr[0:8, :] = padb_scr[ch:ch + 8, :]

    gn = SSM_GROUPS * SSM_STATE
    dtr = dt_ref[0] + dtb_ref[...]
    dt = jnp.maximum(dtr, 0.0) + jnp.log(1.0 + jnp.exp(-jnp.abs(dtr)))
    a = dt * (-jnp.exp(alog_ref[...]))
    expand = exp_ref[...]
    row = lax.broadcasted_iota(jnp.int32, (ch, ch), 0)
    colm = lax.broadcasted_iota(jnp.int32, (ch, ch), 1)
    tril = row >= colm
    tri = jnp.where(tril, 1.0, 0.0)
    dt_e = jnp.dot(dt, expand, precision=HIGHEST, preferred_element_type=F32)
    acs = jnp.dot(tri, a, precision=HIGHEST, preferred_element_type=F32)
    acs_e = jnp.dot(acs, expand, precision=HIGHEST, preferred_element_type=F32)
    acs_t = acs.T
    x_dt = xs * dt_e
    last = acs_e[ch - 1:ch, :]
    x_dec = (x_dt * jnp.exp(last - acs_e)).astype(BF16)
    lane = lax.broadcasted_iota(jnp.int32, (ch, LANE), 1)
    lo = lane < SSM_HEAD_DIM
    half_w = SSM_D_INNER // SSM_GROUPS
    heads_per_group = SSM_HEADS // SSM_GROUPS
    ydiag = []
    yoff = []
    for g in range(SSM_GROUPS):
        bg = bcv[:, g * SSM_STATE:(g + 1) * SSM_STATE]
        cg = bcv[:, gn + g * SSM_STATE:gn + (g + 1) * SSM_STATE].astype(BF16)
        gmat = lax.dot_general(cg, bg.astype(BF16), (((1,), (1,)), ((), ())),
                               preferred_element_type=F32)
        st_g = st_scr[:, g * half_w:(g + 1) * half_w]
        yoff.append(jnp.dot(cg, st_g.astype(BF16), preferred_element_type=F32))
        for j in range(heads_per_group // 2):
            pair = g * (heads_per_group // 2) + j
            xp = x_dt[:, pair * LANE:(pair + 1) * LANE]
            acc = jnp.zeros((ch, LANE), F32)
            for half in range(2):
                h = 2 * pair + half
                seg = jnp.exp(jnp.where(tril, acs[:, h:h + 1] - acs_t[h:h + 1, :], -jnp.inf))
                mmat = (gmat * seg).astype(BF16)
                xh = (jnp.where(lo, xp, 0.0) if half == 0 else jnp.where(lo, 0.0, xp)).astype(BF16)
                acc = acc + jnp.dot(mmat, xh, preferred_element_type=F32)
            ydiag.append(acc)
        upd = jnp.dot(bg.T.astype(BF16), x_dec[:, g * half_w:(g + 1) * half_w],
                      preferred_element_type=F32)
        st_scr[:, g * half_w:(g + 1) * half_w] = st_g * jnp.exp(last[:, g * half_w:(g + 1) * half_w]) + upd
    y = (jnp.concatenate(ydiag, axis=1) + jnp.concatenate(yoff, axis=1) * jnp.exp(acs_e)
         + dsk_ref[...] * xs)
    gz = y * _silu(z_ref[0].astype(F32))
    outs = []
    for g in range(SSM_GROUPS):
        gg = gz[:, g * half_w:(g + 1) * half_w]
        outs.append(gg * lax.rsqrt(jnp.mean(gg * gg, axis=-1, keepdims=True) + NORM_EPS))
    o_ref[0] = (jnp.concatenate(outs, axis=1) * nw_ref[...]).astype(o_ref.dtype)


def _ssd(proj3, tail3, off, conv_w, conv_b, dt_bias, a_log, d_skip, norm_w):
    b, s, _ = proj3.shape
    ch = SSM_CHUNK
    di = SSM_D_INNER
    pad16 = lambda v: jnp.zeros((1, LANE), F32).at[0, :SSM_HEADS].set(v)
    expand = np.zeros((LANE, di), np.float32)
    for h in range(SSM_HEADS):
        expand[h, h * SSM_HEAD_DIM:(h + 1) * SSM_HEAD_DIM] = 1.0
    full = lambda shape: pl.BlockSpec(shape, lambda i, j: (0,) * len(shape))
    return pl.pallas_call(
        _ssd_kernel,
        grid=(b, s // ch),
        in_specs=[pl.BlockSpec((1, ch, di), lambda i, j: (i, j, off["xs"] // di)),
                  pl.BlockSpec((1, ch, SSM_BC), lambda i, j: (i, j, off["bc"] // SSM_BC)),
                  pl.BlockSpec((1, ch, di), lambda i, j: (i, j, off["z"] // di)),
                  pl.BlockSpec((1, ch, LANE), lambda i, j: (i, j, 2)),
                  full((SSM_CONV, di)), full((SSM_CONV, SSM_BC)), full((1, di)), full((1, SSM_BC)),
                  full((1, LANE)), full((1, LANE)), full((1, di)), full((1, di)), full((LANE, di))],
        out_specs=pl.BlockSpec((1, ch, di), lambda i, j: (i, j, 0)),
        out_shape=jax.ShapeDtypeStruct((b, s, di), BF16),
        scratch_shapes=[pltpu.VMEM((ch + 8, di), F32), pltpu.VMEM((ch + 8, SSM_BC), F32),
                        pltpu.VMEM((SSM_STATE, di), F32)],
        compiler_params=_params(("parallel", "arbitrary"), 32 << 20),
        name="ssd_mixer",
    )(proj3, proj3, proj3, tail3, conv_w[:, :di], conv_w[:, di:], conv_b[:di].reshape(1, di),
      conv_b[di:].reshape(1, SSM_BC), pad16(dt_bias), pad16(a_log),
      jnp.repeat(d_skip, SSM_HEAD_DIM).reshape(1, di), norm_w.reshape(1, di), jnp.asarray(expand))


def _merge_kernel(o0_ref, o1_ref, o2_ref, o3_ref, w_ref, g0_ref, g1_ref, g2_ref, g3_ref, out_ref):
    acc = None
    for r, (o_ref, g_ref) in enumerate(((o0_ref, g0_ref), (o1_ref, g1_ref), (o2_ref, g2_ref),
                                        (o3_ref, g3_ref))):
        y = jnp.dot(o_ref[...], w_ref[r], preferred_element_type=F32)
        gate = 1.0 / (1.0 + jnp.exp(-g_ref[...].astype(F32)))
        acc = gate * y if acc is None else acc + gate * y
    out_ref[...] = acc.astype(out_ref.dtype)


def _merge(branches, w_branch, proj, d, tm=512, tn=512):
    t = proj.shape[0]
    nj = d // tn
    o_spec = pl.BlockSpec((tm, BRANCH_W), lambda i, j: (i, 0))
    g_specs = [pl.BlockSpec((tm, tn), functools.partial(lambda i, j, r: (i, r * nj + j), r=r))
               for r in range(N_BRANCH)]
    est = 2 * (4 * tm * BRANCH_W * 2 + 4 * BRANCH_W * tn * 2 + 4 * tm * tn * 2 + tm * tn * 2) + (8 << 20)
    return pl.pallas_call(
        _merge_kernel,
        grid=(t // tm, nj),
        in_specs=[o_spec] * 4 + [pl.BlockSpec((N_BRANCH, BRANCH_W, tn), lambda i, j: (0, 0, j))] + g_specs,
        out_specs=pl.BlockSpec((tm, tn), lambda i, j: (i, j)),
        out_shape=jax.ShapeDtypeStruct((t, d), BF16),
        compiler_params=_params(("parallel", "parallel"), est),
        name="branch_merge",
    )(*branches, w_branch, proj, proj, proj, proj)


def _out_proj_kernel(a_ref, w_ref, x_ref, mod_ref, o_ref, *, g_idx):
    y = jnp.dot(a_ref[...], w_ref[...], preferred_element_type=F32)
    o_ref[...] = x_ref[...] + mod_ref[0, g_idx:g_idx + 1, :] * y


def _out_proj(merged, w_out, x2, mod, seq, g_idx, tm=512, tn=512):
    t, d = x2.shape
    per_b = seq // tm
    est = 2 * (tm * d * 2 + d * tn * 2 + 2 * tm * tn * 4) + tm * tn * 4 + (4 << 20)
    return pl.pallas_call(
        functools.partial(_out_proj_kernel, g_idx=g_idx),
        grid=(t // tm, d // tn),
        in_specs=[pl.BlockSpec((tm, d), lambda i, j: (i, 0)),
                  pl.BlockSpec((d, tn), lambda i, j: (0, j)),
                  pl.BlockSpec((tm, tn), lambda i, j: (i, j)),
                  pl.BlockSpec((1, 6, tn), lambda i, j: (i // per_b, 0, j))],
        out_specs=pl.BlockSpec((tm, tn), lambda i, j: (i, j)),
        out_shape=jax.ShapeDtypeStruct((t, d), F32),
        compiler_params=_params(("parallel", "parallel"), est),
        name="out_proj",
    )(merged, w_out, x2, mod)


def _router_kernel(x_ref, w_ref, mod_ref, rw_ref, rb_ref, h_ref, ri_ref, rwt_ref, cnt_ref, carry_scr,
                   *, sh_idx, sc_idx):
    step = pl.program_id(0) * pl.num_programs(1) + pl.program_id(1)

    @pl.when(step == 0)
    def _():
        carry_scr[...] = jnp.zeros(carry_scr.shape, F32)

    x = x_ref[0]
    y = x * lax.rsqrt(jnp.mean(x * x, axis=-1, keepdims=True) + NORM_EPS) * w_ref[...]
    m = mod_ref[0]
    h = y * (1.0 + m[sc_idx:sc_idx + 1]) + m[sh_idx:sh_idx + 1]
    for g in range(h.shape[1] // (2 * LANE)):
        h_ref[0, :, g * LANE:(g + 1) * LANE] = _pack_bf16_pair(h[:, 2 * g * LANE:(2 * g + 1) * LANE],
                                                               h[:, (2 * g + 1) * LANE:(2 * g + 2) * LANE])
    logits = lax.dot_general(h, rw_ref[...], (((1,), (1,)), ((), ())), precision=HIGHEST,
                             preferred_element_type=F32) + rb_ref[...]
    tm = logits.shape[0]
    lane = lax.broadcasted_iota(jnp.int32, logits.shape, 1)
    big = jnp.int32(4 * LANE)
    neg = -jnp.inf

    def first_argmax(vals):
        mx = jnp.max(vals, axis=1, keepdims=True)
        idx = jnp.min(jnp.where(vals == mx, lane, big), axis=1, keepdims=True)
        return mx, idx

    lg = jnp.where(lane < N_GROUPS, logits, neg)
    gmax, gidx = first_argmax(lg)
    g_w = 1.0 / jnp.sum(jnp.exp(lg - gmax), axis=1, keepdims=True)
    lo = N_GROUPS + gidx * EXPERTS_PER_GROUP
    in_group = jnp.where(lane >= lo, jnp.where(lane < lo + EXPERTS_PER_GROUP, 1.0, 0.0), 0.0) > 0.0
    le = jnp.where(in_group, logits, neg)
    m1, i1 = first_argmax(le)
    le2 = jnp.where(lane == i1, neg, le)
    m2, i2 = first_argmax(le2)
    denom = jnp.sum(jnp.exp(le - m1), axis=1, keepdims=True)
    p1 = 1.0 / denom
    p2 = jnp.exp(m2 - m1) / denom
    w1 = p1 / (p1 + p2) * g_w
    w2 = p2 / (p1 + p2) * g_w
    e1 = i1 - N_GROUPS
    e2 = i2 - N_GROUPS

    oh1 = jnp.where(lane == e1, 1.0, 0.0)
    oh2 = jnp.where(lane == e2, 1.0, 0.0)
    both = oh1 + oh2
    r_i = lax.broadcasted_iota(jnp.int32, (tm, tm), 0)
    c_i = lax.broadcasted_iota(jnp.int32, (tm, tm), 1)
    strict = jnp.where(r_i > c_i, 1.0, 0.0).astype(BF16)
    before = jnp.dot(strict, both.astype(BF16), preferred_element_type=F32) + carry_scr[...]
    rank1 = jnp.sum(before * oh1, axis=1, keepdims=True).astype(jnp.int32)
    rank2 = jnp.sum(before * oh2, axis=1, keepdims=True).astype(jnp.int32)
    carry_scr[...] = carry_scr[...] + jnp.sum(both, axis=0, keepdims=True)
    cnt_ref[...] = carry_scr[...]
    ri_ref[0] = jnp.where(lane == 0, e1, jnp.where(lane == 1, e2, jnp.where(
        lane == 2, rank1, jnp.where(lane == 3, rank2, 0))))
    rwt_ref[0] = jnp.where(lane == 0, w1, jnp.where(lane == 1, w2, 0.0))


def _router(x, w, mod, wg, bg, wr, br, sh_idx, sc_idx, tm=256):
    b, s, d = x.shape
    rw = jnp.concatenate([wg.T, wr.T, jnp.zeros((LANE - N_GROUPS - N_EXPERTS, d), F32)], axis=0)
    rb = jnp.zeros((1, LANE), F32).at[0, :N_GROUPS].set(bg).at[0, N_GROUPS:N_GROUPS + N_EXPERTS].set(br)
    return pl.pallas_call(
        functools.partial(_router_kernel, sh_idx=sh_idx, sc_idx=sc_idx),
        grid=(b, s // tm),
        in_specs=[pl.BlockSpec((1, tm, d), lambda i, j: (i, j, 0)),
                  pl.BlockSpec((1, d), lambda i, j: (0, 0)),
                  pl.BlockSpec((1, 6, d), lambda i, j: (i, 0, 0)),
                  pl.BlockSpec((LANE, d), lambda i, j: (0, 0)),
                  pl.BlockSpec((1, LANE), lambda i, j: (0, 0))],
        out_specs=[pl.BlockSpec((1, tm, d // 2), lambda i, j: (i, j, 0)),
                   pl.BlockSpec((1, tm, LANE), lambda i, j: (i, j, 0)),
                   pl.BlockSpec((1, tm, LANE), lambda i, j: (i, j, 0)),
                   pl.BlockSpec((1, LANE), lambda i, j: (0, 0))],
        out_shape=[jax.ShapeDtypeStruct((b, s, d // 2), jnp.uint32),
                   jax.ShapeDtypeStruct((b, s, LANE), jnp.int32),
                   jax.ShapeDtypeStruct((b, s, LANE), F32),
                   jax.ShapeDtypeStruct((1, LANE), F32)],
        scratch_shapes=[pltpu.VMEM((1, LANE), F32)],
        compiler_params=_params(("arbitrary", "arbitrary"), 8 * tm * d * 4 + d * LANE * 8 + (8 << 20)),
        name="moe_router",
    )(x, w.reshape(1, d), mod, rw, rb)


def _slot_kernel(d1_ref, d2_ref, o_ref):
    def clear(i, carry):
        o_ref[i] = 0
        return carry

    lax.fori_loop(0, o_ref.shape[0], clear, 0, unroll=16)

    def place(t, carry):
        o_ref[d1_ref[t]] = t
        o_ref[d2_ref[t]] = t
        return carry

    lax.fori_loop(0, d1_ref.shape[0], place, 0, unroll=8)


def _slot_tokens(dest1, dest2, cap):
    smem = pl.BlockSpec(memory_space=pltpu.SMEM)
    return pl.pallas_call(
        _slot_kernel,
        in_specs=[smem, smem],
        out_specs=smem,
        out_shape=jax.ShapeDtypeStruct((cap,), jnp.int32),
        name="moe_slot_tokens",
    )(dest1, dest2)


def _expert_kernel(tok_ref, be_ref, nu_ref, first_ref, nxt_ref, h_hbm, wg_hbm, wu_hbm, wd_hbm, o_ref,
                   x_scr, xb_scr, sg_scr, su_scr, sd_scr, wg_scr, wu_scr, wd_scr, sem_x, sem_w, *, layer):
    blk = pl.program_id(0)
    n_used = nu_ref[0]
    rows = x_scr.shape[1]
    slot = lax.rem(blk, 2)

    def row_copy(b, s, i):
        tok = tok_ref[b * rows + i]
        return pltpu.make_async_copy(h_hbm.at[pl.ds(tok, 1)], x_scr.at[s, pl.ds(i, 1)], sem_x.at[s])

    def start_rows(b, s):
        def go(i, carry):
            row_copy(b, s, i).start()
            return carry

        lax.fori_loop(0, rows, go, 0, unroll=8)

    def weight_copies(e):
        return (pltpu.make_async_copy(wg_hbm.at[layer, e], sg_scr, sem_w.at[0]),
                pltpu.make_async_copy(wu_hbm.at[layer, e], su_scr, sem_w.at[1]),
                pltpu.make_async_copy(wd_hbm.at[layer, e], sd_scr, sem_w.at[2]))

    @pl.when(blk < n_used)
    def _():
        @pl.when(blk == 0)
        def _():
            start_rows(0, 0)
            for cp in weight_copies(be_ref[0]):
                cp.start()

        @pl.when(blk + 1 < n_used)
        def _():
            start_rows(blk + 1, 1 - slot)

        @pl.when(first_ref[blk] == 1)
        def _():
            for cp in weight_copies(be_ref[blk]):
                cp.wait()
            step = 512
            for r in range(0, sg_scr.shape[0], step):
                wg_scr[r:r + step, :] = sg_scr[r:r + step, :].astype(BF16)
                wu_scr[r:r + step, :] = su_scr[r:r + step, :].astype(BF16)
            for r in range(0, sd_scr.shape[1], step * 4):
                wd_scr[:, r:r + step * 4] = sd_scr[:, r:r + step * 4].astype(BF16)

            @pl.when(nxt_ref[blk] >= 0)
            def _():
                for cp in weight_copies(nxt_ref[blk]):
                    cp.start()

        def wait(i, carry):
            row_copy(blk, slot, i).wait()
            return carry

        lax.fori_loop(0, rows, wait, 0, unroll=8)
        groups = xb_scr.shape[1] // (2 * LANE)
        for g in range(groups):
            lo, hi = _unpack_bf16_pair(x_scr[slot, :, g * LANE:(g + 1) * LANE])
            xb_scr[:, 2 * g * LANE:(2 * g + 1) * LANE] = lo.astype(BF16)
            xb_scr[:, (2 * g + 1) * LANE:(2 * g + 2) * LANE] = hi.astype(BF16)
        xb = xb_scr[...]
        gate = jnp.dot(xb, wg_scr[...], preferred_element_type=F32)
        up = jnp.dot(xb, wu_scr[...], preferred_element_type=F32)
        act = (_silu(gate) * up).astype(BF16)
        for g in range(groups):
            y = jnp.dot(act, wd_scr[:, 2 * g * LANE:(2 * g + 2) * LANE], preferred_element_type=F32)
            o_ref[:, g * LANE:(g + 1) * LANE] = _pack_bf16_pair(y[:, :LANE], y[:, LANE:])

    @pl.when(blk >= n_used)
    def _():
        o_ref[...] = jnp.zeros(o_ref.shape, o_ref.dtype)


def _experts(slot_tok, block_expert, n_used, first, nxt, h2, wg_stack, wu_stack, wd_stack, layer):
    t, dw = h2.shape
    d = 2 * dw
    n_blocks = block_expert.shape[0]
    hid = wg_stack.shape[3]
    est = 3 * d * hid * 6 + EXP_BLOCK * d * (2 * 2 + 2 + 2 * 2 + 8) + (6 << 20)
    any_spec = pl.BlockSpec(memory_space=pl.ANY)
    return pl.pallas_call(
        functools.partial(_expert_kernel, layer=layer),
        grid_spec=pltpu.PrefetchScalarGridSpec(
            num_scalar_prefetch=5,
            grid=(n_blocks,),
            in_specs=[any_spec, any_spec, any_spec, any_spec],
            out_specs=pl.BlockSpec((EXP_BLOCK, dw), lambda i, *_: (i, 0)),
            scratch_shapes=[pltpu.VMEM((2, EXP_BLOCK, dw), jnp.uint32), pltpu.VMEM((EXP_BLOCK, d), BF16),
                            pltpu.VMEM((d, hid), F32), pltpu.VMEM((d, hid), F32), pltpu.VMEM((hid, d), F32),
                            pltpu.VMEM((d, hid), BF16), pltpu.VMEM((d, hid), BF16), pltpu.VMEM((hid, d), BF16),
                            pltpu.SemaphoreType.DMA((2,)), pltpu.SemaphoreType.DMA((3,))]),
        out_shape=jax.ShapeDtypeStruct((n_blocks * EXP_BLOCK, dw), jnp.uint32),
        compiler_params=_params(("arbitrary",), est),
        name="moe_experts",
    )(slot_tok, block_expert, n_used, first, nxt, h2, wg_stack, wu_stack, wd_stack)


def _combine_kernel(d1_ref, d2_ref, ys_hbm, x_ref, rw_ref, mod_ref, fw_ref, o_ref, a_scr, b_scr, sem,
                    *, g_idx, final):
    i = pl.program_id(0)
    rows = a_scr.shape[1]
    slot = lax.rem(i, 2)

    def copies(step, s, r):
        t = step * rows + r
        return (pltpu.make_async_copy(ys_hbm.at[pl.ds(d1_ref[t], 1)], a_scr.at[s, pl.ds(r, 1)], sem.at[0, s]),
                pltpu.make_async_copy(ys_hbm.at[pl.ds(d2_ref[t], 1)], b_scr.at[s, pl.ds(r, 1)], sem.at[1, s]))

    def start_rows(step, s):
        def go(r, carry):
            ca, cb = copies(step, s, r)
            ca.start()
            cb.start()
            return carry

        lax.fori_loop(0, rows, go, 0, unroll=8)

    @pl.when(i == 0)
    def _():
        start_rows(0, 0)

    @pl.when(i + 1 < pl.num_programs(0))
    def _():
        start_rows(i + 1, 1 - slot)

    def wait(r, carry):
        ca, cb = copies(i, slot, r)
        ca.wait()
        cb.wait()
        return carry

    lax.fori_loop(0, rows, wait, 0, unroll=8)
    rw = rw_ref[...]
    w1 = rw[:, 0:1]
    w2 = rw[:, 1:2]
    d = x_ref.shape[1]
    ssq = jnp.zeros((rows, 1), F32)
    for g in range(d // (2 * LANE)):
        halves_a = _unpack_bf16_pair(a_scr[slot, :, g * LANE:(g + 1) * LANE])
        halves_b = _unpack_bf16_pair(b_scr[slot, :, g * LANE:(g + 1) * LANE])
        for half in range(2):
            c0 = (2 * g + half) * LANE
            moe = w1 * halves_a[half] + w2 * halves_b[half]
            xn = x_ref[:, c0:c0 + LANE] + mod_ref[0, g_idx:g_idx + 1, c0:c0 + LANE] * moe
            o_ref[:, c0:c0 + LANE] = xn
            if final:
                ssq = ssq + jnp.sum(xn * xn, axis=-1, keepdims=True)
    if final:
        o_ref[...] = o_ref[...] * lax.rsqrt(ssq * (1.0 / d) + NORM_EPS) * fw_ref[...]


def _combine(dest1, dest2, ys, x2, route_w, mod, final_w, seq, g_idx, final, tm=128):
    t, d = x2.shape
    per_b = seq // tm
    est = 2 * (2 * tm * d * 4 + tm * LANE * 4) + 4 * tm * d * 4 + (6 << 20)
    return pl.pallas_call(
        functools.partial(_combine_kernel, g_idx=g_idx, final=final),
        grid_spec=pltpu.PrefetchScalarGridSpec(
            num_scalar_prefetch=2,
            grid=(t // tm,),
            in_specs=[pl.BlockSpec(memory_space=pl.ANY),
                      pl.BlockSpec((tm, d), lambda i, d1, d2: (i, 0)),
                      pl.BlockSpec((tm, LANE), lambda i, d1, d2: (i, 0)),
                      pl.BlockSpec((1, 6, d), lambda i, d1, d2: (i // per_b, 0, 0)),
                      pl.BlockSpec((1, d), lambda i, d1, d2: (0, 0))],
            out_specs=pl.BlockSpec((tm, d), lambda i, d1, d2: (i, 0)),
            scratch_shapes=[pltpu.VMEM((2, tm, d // 2), jnp.uint32), pltpu.VMEM((2, tm, d // 2), jnp.uint32),
                            pltpu.SemaphoreType.DMA((2, 2))]),
        out_shape=jax.ShapeDtypeStruct((t, d), F32),
        compiler_params=_params(("arbitrary",), est),
        name="moe_combine",
    )(dest1, dest2, ys, x2, route_w, mod, final_w.reshape(1, d))


def _small_in_weights(wt_stack, layer):
    d = wt_stack.shape[2]
    bounds = np.cumsum((0,) + IN_SIZES_HEAD)
    seg = [wt_stack[layer, bounds[i]:bounds[i + 1], :] for i in range(3, len(IN_SIZES_HEAD))]
    dt, q_lat, kv_lat, k_rope, sq, sk, sv = seg
    z64 = jnp.zeros((SWA_HEAD_DIM, d), wt_stack.dtype)

    def spread(w):
        h0, h1 = w[:SWA_HEAD_DIM], w[SWA_HEAD_DIM:]
        return jnp.concatenate([h0, z64, z64, h0, h1, z64, z64, h1], axis=0)

    half = MLA_ROPE // 2
    rot = jnp.concatenate([-k_rope[half:], k_rope[:half]], axis=0)
    small = jnp.concatenate([sq, spread(sk), spread(sv), kv_lat,
                             jnp.zeros((OFF_S["qlat"] - OFF_S["kvlat"] - MLA_KV_LORA, d), wt_stack.dtype),
                             q_lat], axis=0)
    tail = jnp.concatenate([k_rope, z64, rot, z64, dt,
                            jnp.zeros((LANE - SSM_HEADS, d), wt_stack.dtype)], axis=0)
    return small[None], tail[None]


def _extend_wq(wq_b):
    k = wq_b.shape[0]
    w = wq_b.reshape(k, MLA_HEADS, MLA_NOPE + MLA_ROPE)
    z = jnp.zeros((k, MLA_HEADS, LANE - MLA_ROPE), wq_b.dtype)
    rope = w[..., MLA_NOPE:]
    half = MLA_ROPE // 2
    rot = jnp.concatenate([-rope[..., half:], rope[..., :half]], axis=-1)
    return jnp.concatenate([w[..., :MLA_NOPE], rope, z, rot, z], axis=-1).reshape(k, -1).astype(BF16)


def _mixer(x, mod, pos_col, pos_row, cos_t, sin_t, norm_w, w_in_stack, layer, conv_w, conv_b, dt_bias, a_log,
           d_skip, ssm_norm, mla_q_norm, mla_wq_b, mla_kv_norm, mla_wkv_b, swa_sinks, w_branch, w_out):
    b, s, d = x.shape
    t = b * s
    swa_slopes, moba_slopes = _alibi_slopes()
    h = _norm_mod(x, norm_w, mod, sh_idx=0, sc_idx=1)
    h2 = h.reshape(t, d)
    w_small, w_tail = _small_in_weights(w_in_stack, layer)
    proj_a = _wproj(h2, w_in_stack, layer, 0, A_COLS, "in_proj_head")
    proj_g = _wproj(h2, w_in_stack, layer, GATE_COL0, N_BRANCH * d, "in_proj_gates")
    proj_s = _wproj(h2, w_small, 0, 0, S_COLS, "in_proj_small", tn=768)
    tail = _wproj(h2, w_tail, 0, 0, 3 * LANE, "in_proj_tail", tn=3 * LANE, out_dtype=F32)
    proj_a3 = proj_a.reshape(b, s, A_COLS)
    proj_s3 = proj_s.reshape(b, s, S_COLS)
    tail3 = tail.reshape(b, s, 3 * LANE)

    o_moba = _flash(proj_a3, proj_a3, proj_a3, heads=MOBA_HEADS, dq=MOBA_HEAD_DIM, dv=MOBA_HEAD_DIM,
                    q_off=OFF_A["mq"], k_off=OFF_A["mk"], v_off=OFF_A["mv"],
                    moba=True, q_scale=float(MOBA_HEAD_DIM ** -0.5), slopes=jnp.asarray(moba_slopes),
                    pos_col=pos_col, pos_row=pos_row, name="moba_attention")
    o_ssm = _ssd(proj_a3, tail3, OFF_A, conv_w, conv_b, dt_bias, a_log, d_skip, ssm_norm)
    q_m, k_m, v_m = _mla_project(proj_s, tail, OFF_S, mla_q_norm, _extend_wq(mla_wq_b), mla_kv_norm,
                                 mla_wkv_b.astype(BF16), cos_t, sin_t)
    o_mla = _flash(q_m.reshape(b, s, -1), k_m.reshape(b, s, -1), v_m.reshape(b, s, -1),
                   heads=MLA_HEADS, dq=MLA_QK, dv=MLA_V, q_off=0, k_off=0, v_off=0, name="mla_attention")
    o_swa = _swa(proj_s3, OFF_S, swa_sinks, pos_col, pos_row, swa_slopes)

    branches = [o.reshape(t, BRANCH_W) for o in (o_moba, o_ssm, o_mla, o_swa)]
    merged = _merge(branches, w_branch.astype(BF16), proj_g, d)
    return _out_proj(merged, w_out.astype(BF16), x.reshape(t, d), mod, s, g_idx=2)


def _moe(x2, seq, mod, norm_w, wg, bg, wr, br, w_gate, w_up, w_down, layer, final_w, final):
    t, d = x2.shape
    b = t // seq
    h, route_i, route_w, counts = _router(x2.reshape(b, seq, d), norm_w, mod, wg, bg, wr, br,
                                          sh_idx=3, sc_idx=4)
    route_i = route_i.reshape(t, LANE)
    counts = counts[0, :N_EXPERTS].astype(jnp.int32)
    padded = (counts + EXP_BLOCK - 1) // EXP_BLOCK * EXP_BLOCK
    pad_end = jnp.cumsum(padded)
    pad_start = pad_end - padded
    n_blocks = -(-(t * 2) // EXP_BLOCK) + N_EXPERTS
    dest1 = pad_start[route_i[:, 0]] + route_i[:, 2]
    dest2 = pad_start[route_i[:, 1]] + route_i[:, 3]
    block_row0 = jnp.arange(n_blocks, dtype=jnp.int32) * EXP_BLOCK
    block_expert = jnp.minimum(jnp.sum((pad_end[None, :] <= block_row0[:, None]).astype(jnp.int32), axis=1),
                               N_EXPERTS - 1)
    n_used = (pad_end[-1:] // EXP_BLOCK).astype(jnp.int32)
    first = (block_expert != jnp.concatenate([jnp.full((1,), -1, jnp.int32), block_expert[:-1]])).astype(jnp.int32)
    e_ids = jnp.arange(N_EXPERTS, dtype=jnp.int32)
    later = jnp.where((counts > 0)[None, :] & (e_ids[None, :] > e_ids[:, None]), e_ids[None, :], N_EXPERTS)
    nxt_e = jnp.min(later, axis=1)
    nxt = jnp.where(nxt_e == N_EXPERTS, -1, nxt_e)[block_expert].astype(jnp.int32)
    slot_tok = _slot_tokens(dest1, dest2, n_blocks * EXP_BLOCK)
    ys = _experts(slot_tok, block_expert, n_used, first, nxt, h.reshape(t, d // 2), w_gate, w_up, w_down, layer)
    return _combine(dest1, dest2, ys, x2, route_w.reshape(t, LANE), mod, final_w, seq, g_idx=5, final=final)


def kernel(x, c, positions, ada_w, ada_b, norm_mix, norm_ffn, w_in, conv_w, conv_b, dt_bias, a_log, d_skip,
           ssm_norm, mla_q_norm, mla_wq_b, mla_kv_norm, mla_wkv_b, swa_sinks, w_branch, w_out,
           router_group_w, router_group_b, router_w, router_b, exp_w_gate, exp_w_up, exp_w_down, final_norm):
    b, s, d = x.shape
    depth = ada_w.shape[0]
    mods = _ada_mod(c, ada_w, ada_b)
    pos_f = positions.astype(F32)
    pos_col = pos_f.reshape(b, s, 1)
    pos_row = pos_f.reshape(b, 1, s)
    cos_t, sin_t = _rope_tables(pos_f.reshape(b * s, 1))
    w_in_t = jnp.swapaxes(w_in, 1, 2)
    for l in range(depth):
        x2 = _mixer(x, mods[l], pos_col, pos_row, cos_t, sin_t, norm_mix[l], w_in_t, l, conv_w[l], conv_b[l],
                    dt_bias[l], a_log[l], d_skip[l], ssm_norm[l], mla_q_norm[l], mla_wq_b[l],
                    mla_kv_norm[l], mla_wkv_b[l], swa_sinks[l], w_branch[l], w_out[l])
        x2 = _moe(x2, s, mods[l], norm_ffn[l], router_group_w[l], router_group_b[l], router_w[l],
                  router_b[l], exp_w_gate, exp_w_up, exp_w_down, l, final_norm,
                  final=(l == depth - 1))
        x = x2.reshape(b, s, d)
    return x
```

```python
import functools
import math

import numpy as np
import jax
import jax.numpy as jnp
from jax import lax
from jax.experimental import pallas as pl
from jax.experimental.pallas import tpu as pltpu

F32 = jnp.float32
BF16 = jnp.bfloat16
HIGHEST = lax.Precision.HIGHEST

MOBA_HEADS = 8
MOBA_HEAD_DIM = 128
MOBA_BLOCK = 256
MOBA_TOPK = 3
SSM_D_INNER = 1024
SSM_HEAD_DIM = 64
SSM_HEADS = SSM_D_INNER // SSM_HEAD_DIM
SSM_GROUPS = 2
SSM_STATE = 128
SSM_CONV = 4
SSM_CHUNK = 128
SSM_BC = 2 * SSM_GROUPS * SSM_STATE
MLA_HEADS = 8
MLA_Q_LORA = 768
MLA_KV_LORA = 512
MLA_NOPE = 128
MLA_ROPE = 64
MLA_V = 128
ROPE_THETA = 10000.0
SWA_HEADS = 16
SWA_KV_HEADS = 2
SWA_HEAD_DIM = 64
SWA_WINDOW = 128
N_BRANCH = 4
BRANCH_W = 1024
N_GROUPS = 4
EXPERTS_PER_GROUP = 8
N_EXPERTS = N_GROUPS * EXPERTS_PER_GROUP
EXPERT_HIDDEN = 512
EXP_BLOCK = 256
NORM_EPS = 1e-6
N_ALIBI = MOBA_HEADS + SWA_HEADS

LOG2E = math.log2(math.e)
MASKED = -1e30
LANE = 128
VMEM_CAP = 60 * 1024 * 1024
ATTN_TILE = 512
FLASH_HEADS = 2
MLA_QK = 256


def _alibi_slopes():
    i = np.arange(1, N_ALIBI + 1, dtype=np.float64)
    s = np.exp2(-8.0 * i / N_ALIBI).astype(np.float32)
    return s[:SWA_HEADS], s[SWA_HEADS:]


IN_SIZES_HEAD = (3 * MOBA_HEADS * MOBA_HEAD_DIM, SSM_D_INNER, SSM_D_INNER + SSM_BC, SSM_HEADS, MLA_Q_LORA,
                 MLA_KV_LORA, MLA_ROPE, SWA_HEADS * SWA_HEAD_DIM, SWA_KV_HEADS * SWA_HEAD_DIM,
                 SWA_KV_HEADS * SWA_HEAD_DIM)
A_COLS = IN_SIZES_HEAD[0] + IN_SIZES_HEAD[1] + IN_SIZES_HEAD[2]
GATE_COL0 = sum(IN_SIZES_HEAD)
OFF_A = dict(mq=0, mk=1024, mv=2048, z=3072, xs=4096, bc=5120)
OFF_S = dict(sq=0, skx=1024, svx=1536, kvlat=2048, qlat=3072)
S_COLS = 3840


def _params(sem, est_bytes):
    limit = int(min(VMEM_CAP, max(est_bytes, 16 * 1024 * 1024)))
    return pltpu.CompilerParams(dimension_semantics=sem, vmem_limit_bytes=limit)


def _silu(v):
    return v * (1.0 / (1.0 + jnp.exp(-v)))


def _pack_bf16_pair(lo, hi):
    lo_bits = pltpu.bitcast(lo.astype(BF16).astype(F32), jnp.uint32)
    hi_bits = pltpu.bitcast(hi.astype(BF16).astype(F32), jnp.uint32)
    return hi_bits | lax.shift_right_logical(lo_bits, jnp.uint32(16))


def _unpack_bf16_pair(word):
    lo = pltpu.bitcast(lax.shift_left(word, jnp.uint32(16)), F32)
    hi = pltpu.bitcast(word & jnp.uint32(0xFFFF0000), F32)
    return lo, hi


def _ada_kernel(c_ref, w_ref, b_ref, o_ref):
    k = pl.program_id(1)

    @pl.when(k == 0)
    def _():
        o_ref[0] = jnp.broadcast_to(b_ref[0], o_ref.shape[1:])

    o_ref[0] += jnp.dot(c_ref[...].astype(BF16), w_ref[0].astype(BF16), preferred_element_type=F32)


def _ada_mod(c, ada_w, ada_b):
    depth, d, n = ada_w.shape
    b = c.shape[0]
    c8 = jnp.zeros((8, d), F32).at[:b].set(c)
    tk = LANE
    out = pl.pallas_call(
        _ada_kernel,
        grid=(depth, d // tk),
        in_specs=[pl.BlockSpec((8, tk), lambda l, k: (0, k)),
                  pl.BlockSpec((1, tk, n), lambda l, k: (l, k, 0)),
                  pl.BlockSpec((1, 1, n), lambda l, k: (l, 0, 0))],
        out_specs=pl.BlockSpec((1, 8, n), lambda l, k: (l, 0, 0)),
        out_shape=jax.ShapeDtypeStruct((depth, 8, n), F32),
        compiler_params=_params(("parallel", "arbitrary"), 3 * tk * n * 4 + 32 * n * 4 + (4 << 20)),
        name="ada_mod",
    )(c8, ada_w, ada_b.reshape(depth, 1, n))
    return out[:, :b].reshape(depth, b, 6, d)


def _norm_mod_kernel(x_ref, w_ref, mod_ref, o_ref, *, sh_idx, sc_idx):
    x = x_ref[0]
    y = x * lax.rsqrt(jnp.mean(x * x, axis=-1, keepdims=True) + NORM_EPS) * w_ref[...]
    m = mod_ref[0]
    o_ref[0] = (y * (1.0 + m[sc_idx:sc_idx + 1]) + m[sh_idx:sh_idx + 1]).astype(o_ref.dtype)


def _norm_mod(x, w, mod, sh_idx, sc_idx, tm=256):
    b, s, d = x.shape
    return pl.pallas_call(
        functools.partial(_norm_mod_kernel, sh_idx=sh_idx, sc_idx=sc_idx),
        grid=(b, s // tm),
        in_specs=[pl.BlockSpec((1, tm, d), lambda i, j: (i, j, 0)),
                  pl.BlockSpec((1, d), lambda i, j: (0, 0)),
                  pl.BlockSpec((1, 6, d), lambda i, j: (i, 0, 0))],
        out_specs=pl.BlockSpec((1, tm, d), lambda i, j: (i, j, 0)),
        out_shape=jax.ShapeDtypeStruct((b, s, d), BF16),
        compiler_params=_params(("parallel", "parallel"), 6 * tm * d * 4 + (4 << 20)),
        name="norm_mod",
    )(x, w.reshape(1, d), mod)


def _wproj_kernel(a_ref, w_ref, o_ref, wbf_scr):
    tn = wbf_scr.shape[0]
    rows = 64

    @pl.when(pl.program_id(1) == 0)
    def _():
        for r in range(0, tn, rows):
            wbf_scr[r:r + rows, :] = w_ref[0, r:r + rows, :].astype(BF16)

    o_ref[...] = lax.dot_general(a_ref[...], wbf_scr[...], (((1,), (1,)), ((), ())),
                                 preferred_element_type=F32).astype(o_ref.dtype)


def _wproj(a, wt_stack, layer, col0, n_cols, name, tm=1024, tn=512, out_dtype=BF16):
    m, k = a.shape
    assert n_cols % tn == 0 and m % tm == 0 and col0 % 8 == 0
    if col0 % tn == 0:
        w_spec = pl.BlockSpec((1, tn, k), lambda j, i: (layer, col0 // tn + j, 0))
    else:
        w_spec = pl.BlockSpec((pl.Element(1), pl.Element(tn), pl.Element(k)),
                              lambda j, i: (layer, pl.multiple_of(col0 + j * tn, 8), 0))
    est = 2 * (tm * k * 2 + tn * k * 4 + tm * tn * 2) + k * tn * 2 + tm * tn * 4 + (6 << 20)
    return pl.pallas_call(
        _wproj_kernel,
        grid=(n_cols // tn, m // tm),
        in_specs=[pl.BlockSpec((tm, k), lambda j, i: (i, 0)), w_spec],
        out_specs=pl.BlockSpec((tm, tn), lambda j, i: (i, j)),
        out_shape=jax.ShapeDtypeStruct((m, n_cols), out_dtype),
        scratch_shapes=[pltpu.VMEM((tn, k), BF16)],
        compiler_params=_params(("parallel", "arbitrary"), est),
        name=name,
    )(a, wt_stack)


def _rope_kernel(pos_ref, freq_ref, cos_ref, sin_ref):
    ang = pos_ref[...] * freq_ref[...]
    lane = lax.broadcasted_iota(jnp.int32, ang.shape, 1)
    live = lane < MLA_ROPE
    cos_ref[...] = jnp.where(live, jnp.cos(ang), 0.0)
    sin_ref[...] = jnp.where(live, jnp.sin(ang), 0.0)


def _rope_tables(pos_col):
    t = pos_col.shape[0]
    half = MLA_ROPE // 2
    inv = ROPE_THETA ** (-np.arange(half, dtype=np.float32) / half)
    freq = np.zeros((1, LANE), np.float32)
    freq[0, :half] = inv
    freq[0, half:2 * half] = inv
    tm = 512
    return pl.pallas_call(
        _rope_kernel,
        grid=(t // tm,),
        in_specs=[pl.BlockSpec((tm, 1), lambda i: (i, 0)),
                  pl.BlockSpec((1, LANE), lambda i: (0, 0))],
        out_specs=[pl.BlockSpec((tm, LANE), lambda i: (i, 0))] * 2,
        out_shape=[jax.ShapeDtypeStruct((t, LANE), F32)] * 2,
        compiler_params=_params(("parallel",), 16 << 20),
        name="rope_tables",
    )(pos_col, jnp.asarray(freq))


def _mla_q_kernel(x_ref, nw_ref, w_ref, cos_ref, sin_ref, o_ref, *, scale):
    x = x_ref[...].astype(F32)
    y = x * lax.rsqrt(jnp.mean(x * x, axis=-1, keepdims=True) + NORM_EPS) * nw_ref[...]
    r = jnp.dot(y.astype(BF16), w_ref[...], preferred_element_type=F32)
    cos = cos_ref[...]
    sin = sin_ref[...]
    for h in range(MLA_HEADS):
        base = h * 3 * LANE
        nope = r[:, base:base + LANE]
        rope = r[:, base + LANE:base + 2 * LANE]
        rot = r[:, base + 2 * LANE:base + 3 * LANE]
        o_ref[:, h * MLA_QK:h * MLA_QK + LANE] = (nope * scale).astype(o_ref.dtype)
        o_ref[:, h * MLA_QK + LANE:(h + 1) * MLA_QK] = (
            (rope * cos + rot * sin) * scale).astype(o_ref.dtype)


def _mla_kv_kernel(x_ref, nw_ref, w_ref, kr_ref, krot_ref, cos_ref, sin_ref, k_ref, v_ref):
    x = x_ref[...].astype(F32)
    y = x * lax.rsqrt(jnp.mean(x * x, axis=-1, keepdims=True) + NORM_EPS) * nw_ref[...]
    r = jnp.dot(y.astype(BF16), w_ref[...], preferred_element_type=F32)
    kr = (kr_ref[...] * cos_ref[...] + krot_ref[...] * sin_ref[...]).astype(k_ref.dtype)
    for h in range(MLA_HEADS):
        base = h * (MLA_NOPE + MLA_V)
        k_ref[:, h * MLA_QK:h * MLA_QK + LANE] = r[:, base:base + MLA_NOPE].astype(k_ref.dtype)
        k_ref[:, h * MLA_QK + LANE:(h + 1) * MLA_QK] = kr
        v_ref[:, h * MLA_V:(h + 1) * MLA_V] = r[:, base + MLA_NOPE:base + MLA_NOPE + MLA_V].astype(
            v_ref.dtype)


def _mla_project(proj, tail, off, q_norm, wq_ext, kv_norm, wkv, cos_t, sin_t, tm=512):
    t = proj.shape[0]
    scale = float((MLA_NOPE + MLA_ROPE) ** -0.5) * LOG2E
    q = pl.pallas_call(
        functools.partial(_mla_q_kernel, scale=scale),
        grid=(t // tm,),
        in_specs=[pl.BlockSpec((tm, MLA_Q_LORA), lambda i: (i, off["qlat"] // MLA_Q_LORA)),
                  pl.BlockSpec((1, MLA_Q_LORA), lambda i: (0, 0)),
                  pl.BlockSpec(wq_ext.shape, lambda i: (0, 0)),
                  pl.BlockSpec((tm, LANE), lambda i: (i, 0)),
                  pl.BlockSpec((tm, LANE), lambda i: (i, 0))],
        out_specs=pl.BlockSpec((tm, MLA_HEADS * MLA_QK), lambda i: (i, 0)),
        out_shape=jax.ShapeDtypeStruct((t, MLA_HEADS * MLA_QK), BF16),
        compiler_params=_params(("parallel",), 40 << 20),
        name="mla_q_proj",
    )(proj, q_norm.reshape(1, -1), wq_ext, cos_t, sin_t)
    k, v = pl.pallas_call(
        _mla_kv_kernel,
        grid=(t // tm,),
        in_specs=[pl.BlockSpec((tm, MLA_KV_LORA), lambda i: (i, off["kvlat"] // MLA_KV_LORA)),
                  pl.BlockSpec((1, MLA_KV_LORA), lambda i: (0, 0)),
                  pl.BlockSpec(wkv.shape, lambda i: (0, 0)),
                  pl.BlockSpec((tm, LANE), lambda i: (i, 0)),
                  pl.BlockSpec((tm, LANE), lambda i: (i, 1)),
                  pl.BlockSpec((tm, LANE), lambda i: (i, 0)),
                  pl.BlockSpec((tm, LANE), lambda i: (i, 0))],
        out_specs=[pl.BlockSpec((tm, MLA_HEADS * MLA_QK), lambda i: (i, 0)),
                   pl.BlockSpec((tm, MLA_HEADS * MLA_V), lambda i: (i, 0))],
        out_shape=[jax.ShapeDtypeStruct((t, MLA_HEADS * MLA_QK), BF16),
                   jax.ShapeDtypeStruct((t, MLA_HEADS * MLA_V), BF16)],
        compiler_params=_params(("parallel",), 40 << 20),
        name="mla_kv_proj",
    )(proj, kv_norm.reshape(1, -1), wkv, tail, tail, cos_t, sin_t)
    return q, k, v


def _flash_kernel(*refs, moba, q_scale, dq, dv):
    if moba:
        slope_ref, q_ref, k_ref, v_ref, posk_ref, posq_ref, o_ref, vt_scr, kmean_scr, sel_scr = refs
    else:
        q_ref, k_ref, v_ref, o_ref, vt_scr = refs
    tile = ATTN_TILE
    seq = k_ref.shape[1]
    sub = tile // MOBA_BLOCK
    nblk = seq // MOBA_BLOCK
    qi = pl.program_id(2)
    heads = range(FLASH_HEADS)

    @pl.when(qi == 0)
    def _():
        for j in heads:
            for c in range(seq // tile):
                vt_scr[j, c] = v_ref[0, c * tile:(c + 1) * tile, j * dv:(j + 1) * dv].astype(F32).T.astype(BF16)
            if moba:
                for n in range(nblk):
                    kblk = k_ref[0, n * MOBA_BLOCK:(n + 1) * MOBA_BLOCK, j * dq:(j + 1) * dq].astype(F32)
                    kmean_scr[j, n:n + 1, :] = jnp.mean(kblk, axis=0, keepdims=True)

    q = []
    for j in heads:
        q_raw = q_ref[0, :, j * dq:(j + 1) * dq]
        q.append(q_raw if q_scale is None else (q_raw.astype(F32) * q_scale).astype(BF16))
        if moba:
            gate = lax.dot_general(kmean_scr[j], q_raw.astype(F32), (((1,), (1,)), ((), ())),
                                   precision=HIGHEST, preferred_element_type=F32)
            n_iota = lax.broadcasted_iota(jnp.int32, gate.shape, 0)
            q_blk = qi * sub + lax.broadcasted_iota(jnp.int32, gate.shape, 1) // MOBA_BLOCK
            beaten = jnp.zeros(gate.shape, F32)
            for m in range(nblk):
                gm = gate[m:m + 1, :]
                wins = jnp.where(gm > gate, 1.0, jnp.where(gm == gate, jnp.where(m < n_iota, 1.0, 0.0), 0.0))
                beaten = beaten + jnp.where(m < q_blk, wins, 0.0)
            sel_scr[j] = jnp.where(beaten < MOBA_TOPK, jnp.where(n_iota < q_blk, 0.0, MASKED), MASKED)
    if moba:
        slope = [slope_ref[pl.program_id(1) * FLASH_HEADS + j] for j in heads]
        pq = [posq_ref[0] * slope[j] for j in heads]

    def scores(j, c):
        start = pl.multiple_of(c * tile, tile)
        kb = k_ref[0, pl.ds(start, tile), j * dq:(j + 1) * dq]
        st = lax.dot_general(kb, q[j], (((1,), (1,)), ((), ())), preferred_element_type=F32)
        if moba:
            pk = posk_ref[0, pl.ds(start, tile), :] * slope[j]
            st = st - jnp.abs(pq[j] - pk)
        return st

    def block_bias(j, c):
        rows = [jnp.broadcast_to(sel_scr[j, pl.ds(c * sub + r, 1), :], (MOBA_BLOCK, tile)) for r in range(sub)]
        return jnp.concatenate(rows, axis=0)

    key_i = lax.broadcasted_iota(jnp.int32, (tile, tile), 0)
    qry_i = lax.broadcasted_iota(jnp.int32, (tile, tile), 1)
    causal_bias = jnp.where(key_i <= qry_i, 0.0, MASKED)
    init = []
    for j in heads:
        st = scores(j, qi)
        if moba:
            same_blk = (key_i // MOBA_BLOCK) == (qry_i // MOBA_BLOCK)
            st = st + jnp.where(same_blk, causal_bias, block_bias(j, qi))
        else:
            st = st + causal_bias
        m0 = jnp.max(st, axis=0, keepdims=True)
        p = jnp.exp2(st - m0)
        l0 = jnp.sum(p, axis=0, keepdims=True)
        init.append((m0, l0, jnp.dot(vt_scr[j, qi], p.astype(BF16), preferred_element_type=F32)))

    def body(c, carry):
        out = []
        for j in heads:
            m_prev, l_prev, acc = carry[j]
            s_c = scores(j, c)
            if moba:
                s_c = s_c + block_bias(j, c)
            m_new = jnp.maximum(m_prev, jnp.max(s_c, axis=0, keepdims=True))
            alpha = jnp.exp2(m_prev - m_new)
            p_c = jnp.exp2(s_c - m_new)
            l_new = alpha * l_prev + jnp.sum(p_c, axis=0, keepdims=True)
            acc_new = alpha * acc + jnp.dot(vt_scr[j, c], p_c.astype(BF16), preferred_element_type=F32)
            out.append((m_new, l_new, acc_new))
        return tuple(out)

    final = lax.fori_loop(0, qi, body, tuple(init))
    for j in heads:
        _, l_fin, acc = final[j]
        o_ref[0, :, j * dv:(j + 1) * dv] = (acc * (1.0 / l_fin)).T.astype(o_ref.dtype)


def _flash(q, k, v, *, heads, dq, dv, q_off, k_off, v_off, moba=False, q_scale=None,
           slopes=None, pos_col=None, pos_row=None, name="flash"):
    b, s, _ = q.shape
    tile = ATTN_TILE
    hp = FLASH_HEADS
    assert heads % hp == 0 and q_off % (hp * dq) == 0 and k_off % (hp * dq) == 0 and v_off % (hp * dv) == 0
    qb, kb, vb = q_off // (hp * dq), k_off // (hp * dq), v_off // (hp * dv)
    in_specs = [pl.BlockSpec((1, tile, hp * dq), lambda bi, h, i: (bi, i, qb + h)),
                pl.BlockSpec((1, s, hp * dq), lambda bi, h, i: (bi, 0, kb + h)),
                pl.BlockSpec((1, s, hp * dv), lambda bi, h, i: (bi, 0, vb + h))]
    args = [q, k, v]
    scratch = [pltpu.VMEM((hp, s // tile, dv, tile), BF16)]
    if moba:
        in_specs = [pl.BlockSpec(memory_space=pltpu.SMEM)] + in_specs + [
            pl.BlockSpec((1, s, 1), lambda bi, h, i: (bi, 0, 0)),
            pl.BlockSpec((1, 1, tile), lambda bi, h, i: (bi, 0, i))]
        args = [slopes] + args + [pos_col, pos_row]
        scratch += [pltpu.VMEM((hp, s // MOBA_BLOCK, dq), F32), pltpu.VMEM((hp, s // MOBA_BLOCK, tile), F32)]
    est = hp * (2 * (s * dq * 2 + s * dv * 2) + s * dv * 2) + s * LANE * 4 * 2 + (20 << 20)
    return pl.pallas_call(
        functools.partial(_flash_kernel, moba=moba, q_scale=q_scale, dq=dq, dv=dv),
        grid=(b, heads // hp, s // tile),
        in_specs=in_specs,
        out_specs=pl.BlockSpec((1, tile, hp * dv), lambda bi, h, i: (bi, i, h)),
        out_shape=jax.ShapeDtypeStruct((b, s, heads * dv), BF16),
        scratch_shapes=scratch,
        compiler_params=_params(("parallel", "parallel", "arbitrary"), est),
        name=name,
    )(*args)


def _swa_kernel(sink_ref, q_ref, kp_ref, kc_ref, vp_ref, vc_ref, pq_ref, pkp_ref, pkc_ref, o_ref,
                *, slopes):
    w = SWA_WINDOW
    n = pl.program_id(1)
    kx = jnp.concatenate([kp_ref[0], kc_ref[0]], axis=0)
    vx = jnp.concatenate([vp_ref[0], vc_ref[0]], axis=0)
    pk = jnp.concatenate([pkp_ref[0], pkc_ref[0]], axis=1)
    dist = jnp.abs(pq_ref[0] - pk)
    qi = lax.broadcasted_iota(jnp.int32, dist.shape, 0)
    kk = lax.broadcasted_iota(jnp.int32, dist.shape, 1)
    allowed = jnp.where(kk > qi, jnp.where(kk <= qi + w, 1.0, 0.0), 0.0)
    allowed = jnp.where(kk >= w, allowed, jnp.where(n > 0, allowed, 0.0)) > 0.0
    scale = float(SWA_HEAD_DIM ** -0.5)
    rep = SWA_HEADS // SWA_KV_HEADS
    for pair in range(SWA_HEADS // 2):
        qp = q_ref[0, :, pair * LANE:(pair + 1) * LANE]
        acc = jnp.zeros((w, LANE), F32)
        for half in range(2):
            h = 2 * pair + half
            col = (2 * (h // rep) + half) * LANE
            s = lax.dot_general(qp, kx[:, col:col + LANE], (((1,), (1,)), ((), ())),
                                preferred_element_type=F32)
            s = s * scale - float(slopes[h]) * dist
            s = jnp.where(allowed, s, -jnp.inf)
            sink = sink_ref[h]
            m = jnp.maximum(jnp.max(s, axis=1, keepdims=True), sink)
            p = jnp.exp(s - m)
            denom = jnp.sum(p, axis=1, keepdims=True) + jnp.exp(sink - m)
            p = p * (1.0 / denom)
            acc = acc + jnp.dot(p.astype(BF16), vx[:, col:col + LANE], preferred_element_type=F32)
        o_ref[0, :, pair * LANE:(pair + 1) * LANE] = acc.astype(o_ref.dtype)


def _swa(proj3, off, sinks, pos_col, pos_row, slopes):
    b, s, _ = proj3.shape
    w = SWA_WINDOW
    qw = SWA_HEADS * SWA_HEAD_DIM
    kw = 4 * LANE
    prev = lambda j: jnp.maximum(j - 1, 0)
    return pl.pallas_call(
        functools.partial(_swa_kernel, slopes=tuple(float(v) for v in slopes)),
        grid=(b, s // w),
        in_specs=[pl.BlockSpec(memory_space=pltpu.SMEM),
                  pl.BlockSpec((1, w, qw), lambda i, j: (i, j, off["sq"] // qw)),
                  pl.BlockSpec((1, w, kw), lambda i, j: (i, prev(j), off["skx"] // kw)),
                  pl.BlockSpec((1, w, kw), lambda i, j: (i, j, off["skx"] // kw)),
                  pl.BlockSpec((1, w, kw), lambda i, j: (i, prev(j), off["svx"] // kw)),
                  pl.BlockSpec((1, w, kw), lambda i, j: (i, j, off["svx"] // kw)),
                  pl.BlockSpec((1, w, 1), lambda i, j: (i, j, 0)),
                  pl.BlockSpec((1, 1, w), lambda i, j: (i, 0, prev(j))),
                  pl.BlockSpec((1, 1, w), lambda i, j: (i, 0, j))],
        out_specs=pl.BlockSpec((1, w, qw), lambda i, j: (i, j, 0)),
        out_shape=jax.ShapeDtypeStruct((b, s, qw), BF16),
        compiler_params=_params(("parallel", "parallel"), 24 << 20),
        name="swa_attention",
    )(sinks, proj3, proj3, proj3, proj3, proj3, pos_col, pos_row, pos_row)


def _ssd_kernel(xs_ref, bc_ref, z_ref, dt_ref, cwx_ref, cwb_ref, cbx_ref, cbb_ref, dtb_ref,
                alog_ref, dsk_ref, nw_ref, exp_ref, o_ref, padx_scr, padb_scr, st_scr):
    ch = SSM_CHUNK
    c = pl.program_id(1)

    @pl.when(c == 0)
    def _():
        padx_scr[0:8, :] = jnp.zeros((8, padx_scr.shape[1]), F32)
        padb_scr[0:8, :] = jnp.zeros((8, padb_scr.shape[1]), F32)
        st_scr[...] = jnp.zeros(st_scr.shape, F32)

    padx_scr[8:8 + ch, :] = xs_ref[0].astype(F32)
    padb_scr[8:8 + ch, :] = bc_ref[0].astype(F32)

    def conv(pad_scr, w_ref, b_ref):
        acc = b_ref[...] + w_ref[0:1, :] * pad_scr[5:5 + ch, :]
        for k in range(1, SSM_CONV):
            acc = acc + w_ref[k:k + 1, :] * pad_scr[5 + k:5 + k + ch, :]
        return _silu(acc)

    xs = conv(padx_scr, cwx_ref, cbx_ref)
    bcv = conv(padb_scr, cwb_ref, cbb_ref)
    padx_scr[0:8, :] = padx_scr[ch:ch + 8, :]
    padb_scr[0:8, :] = padb_scr[ch:ch + 8, :]

    gn = SSM_GROUPS * SSM_STATE
    dtr = dt_ref[0] + dtb_ref[...]
    dt = jnp.maximum(dtr, 0.0) + jnp.log(1.0 + jnp.exp(-jnp.abs(dtr)))
    a = dt * (-jnp.exp(alog_ref[...]))
    expand = exp_ref[...]
    row = lax.broadcasted_iota(jnp.int32, (ch, ch), 0)
    colm = lax.broadcasted_iota(jnp.int32, (ch, ch), 1)
    tril = row >= colm
    tri = jnp.where(tril, 1.0, 0.0)
    dt_e = jnp.dot(dt, expand, precision=HIGHEST, preferred_element_type=F32)
    acs = jnp.dot(tri, a, precision=HIGHEST, preferred_element_type=F32)
    acs_e = jnp.dot(acs, expand, precision=HIGHEST, preferred_element_type=F32)
    acs_t = acs.T
    x_dt = xs * dt_e
    last = acs_e[ch - 1:ch, :]
    x_dec = (x_dt * jnp.exp(last - acs_e)).astype(BF16)
    lane = lax.broadcasted_iota(jnp.int32, (ch, LANE), 1)
    lo = lane < SSM_HEAD_DIM
    half_w = SSM_D_INNER // SSM_GROUPS
    heads_per_group = SSM_HEADS // SSM_GROUPS
    ydiag = []
    yoff = []
    for g in range(SSM_GROUPS):
        bg = bcv[:, g * SSM_STATE:(g + 1) * SSM_STATE]
        cg = bcv[:, gn + g * SSM_STATE:gn + (g + 1) * SSM_STATE].astype(BF16)
        gmat = lax.dot_general(cg, bg.astype(BF16), (((1,), (1,)), ((), ())),
                               preferred_element_type=F32)
        st_g = st_scr[:, g * half_w:(g + 1) * half_w]
        yoff.append(jnp.dot(cg, st_g.astype(BF16), preferred_element_type=F32))
        for j in range(heads_per_group // 2):
            pair = g * (heads_per_group // 2) + j
            xp = x_dt[:, pair * LANE:(pair + 1) * LANE]
            acc = jnp.zeros((ch, LANE), F32)
            for half in range(2):
                h = 2 * pair + half
                seg = jnp.exp(jnp.where(tril, acs[:, h:h + 1] - acs_t[h:h + 1, :], -jnp.inf))
                mmat = (gmat * seg).astype(BF16)
                xh = (jnp.where(lo, xp, 0.0) if half == 0 else jnp.where(lo, 0.0, xp)).astype(BF16)
                acc = acc + jnp.dot(mmat, xh, preferred_element_type=F32)
            ydiag.append(acc)
        upd = jnp.dot(bg.T.astype(BF16), x_dec[:, g * half_w:(g + 1) * half_w],
                      preferred_element_type=F32)
        st_scr[:, g * half_w:(g + 1) * half_w] = st_g * jnp.exp(last[:, g * half_w:(g + 1) * half_w]) + upd
    y = (jnp.concatenate(ydiag, axis=1) + jnp.concatenate(yoff, axis=1) * jnp.exp(acs_e)
         + dsk_ref[...] * xs)
    gz = y * _silu(z_ref[0].astype(F32))
    outs = []
    for g in range(SSM_GROUPS):
        gg = gz[:, g * half_w:(g + 1) * half_w]
        outs.append(gg * lax.rsqrt(jnp.mean(gg * gg, axis=-1, keepdims=True) + NORM_EPS))
    o_ref[0] = (jnp.concatenate(outs, axis=1) * nw_ref[...]).astype(o_ref.dtype)


def _ssd(proj3, tail3, off, conv_w, conv_b, dt_bias, a_log, d_skip, norm_w):
    b, s, _ = proj3.shape
    ch = SSM_CHUNK
    di = SSM_D_INNER
    pad16 = lambda v: jnp.zeros((1, LANE), F32).at[0, :SSM_HEADS].set(v)
    expand = np.zeros((LANE, di), np.float32)
    for h in range(SSM_HEADS):
        expand[h, h * SSM_HEAD_DIM:(h + 1) * SSM_HEAD_DIM] = 1.0
    full = lambda shape: pl.BlockSpec(shape, lambda i, j: (0,) * len(shape))
    return pl.pallas_call(
        _ssd_kernel,
        grid=(b, s // ch),
        in_specs=[pl.BlockSpec((1, ch, di), lambda i, j: (i, j, off["xs"] // di)),
                  pl.BlockSpec((1, ch, SSM_BC), lambda i, j: (i, j, off["bc"] // SSM_BC)),
                  pl.BlockSpec((1, ch, di), lambda i, j: (i, j, off["z"] // di)),
                  pl.BlockSpec((1, ch, LANE), lambda i, j: (i, j, 2)),
                  full((SSM_CONV, di)), full((SSM_CONV, SSM_BC)), full((1, di)), full((1, SSM_BC)),
                  full((1, LANE)), full((1, LANE)), full((1, di)), full((1, di)), full((LANE, di))],
        out_specs=pl.BlockSpec((1, ch, di), lambda i, j: (i, j, 0)),
        out_shape=jax.ShapeDtypeStruct((b, s, di), BF16),
        scratch_shapes=[pltpu.VMEM((ch + 8, di), F32), pltpu.VMEM((ch + 8, SSM_BC), F32),
                        pltpu.VMEM((SSM_STATE, di), F32)],
        compiler_params=_params(("parallel", "arbitrary"), 32 << 20),
        name="ssd_mixer",
    )(proj3, proj3, proj3, tail3, conv_w[:, :di], conv_w[:, di:], conv_b[:di].reshape(1, di),
      conv_b[di:].reshape(1, SSM_BC), pad16(dt_bias), pad16(a_log),
      jnp.repeat(d_skip, SSM_HEAD_DIM).reshape(1, di), norm_w.reshape(1, di), jnp.asarray(expand))


def _merge_kernel(o0_ref, o1_ref, o2_ref, o3_ref, w_ref, g0_ref, g1_ref, g2_ref, g3_ref, out_ref, wbf_scr):
    @pl.when(pl.program_id(1) == 0)
    def _():
        for r in range(N_BRANCH):
            wbf_scr[r] = w_ref[0, r].astype(BF16)

    acc = None
    for r, (o_ref, g_ref) in enumerate(((o0_ref, g0_ref), (o1_ref, g1_ref), (o2_ref, g2_ref),
                                        (o3_ref, g3_ref))):
        y = jnp.dot(o_ref[...], wbf_scr[r], preferred_element_type=F32)
        gate = 1.0 / (1.0 + jnp.exp(-g_ref[...].astype(F32)))
        acc = gate * y if acc is None else acc + gate * y
    out_ref[...] = acc.astype(out_ref.dtype)


def _merge(branches, w_branch, layer, proj, d, tm=1024, tn=512):
    t = proj.shape[0]
    nj = d // tn
    o_spec = pl.BlockSpec((tm, BRANCH_W), lambda j, i: (i, 0))
    g_specs = [pl.BlockSpec((tm, tn), functools.partial(lambda j, i, r: (i, r * nj + j), r=r))
               for r in range(N_BRANCH)]
    est = (2 * (4 * tm * BRANCH_W * 2 + 4 * BRANCH_W * tn * 4 + 4 * tm * tn * 2 + tm * tn * 2)
           + 4 * BRANCH_W * tn * 2 + 3 * tm * tn * 4 + (6 << 20))
    return pl.pallas_call(
        _merge_kernel,
        grid=(nj, t // tm),
        in_specs=[o_spec] * 4 + [pl.BlockSpec((1, N_BRANCH, BRANCH_W, tn), lambda j, i: (layer, 0, 0, j))]
        + g_specs,
        out_specs=pl.BlockSpec((tm, tn), lambda j, i: (i, j)),
        out_shape=jax.ShapeDtypeStruct((t, d), BF16),
        scratch_shapes=[pltpu.VMEM((N_BRANCH, BRANCH_W, tn), BF16)],
        compiler_params=_params(("parallel", "arbitrary"), est),
        name="branch_merge",
    )(*branches, w_branch, proj, proj, proj, proj)


def _out_proj_kernel(a_ref, w_ref, x_ref, mod_ref, o_ref, wbf_scr, *, g_idx):
    @pl.when(pl.program_id(1) == 0)
    def _():
        rows = 512
        for r in range(0, wbf_scr.shape[0], rows):
            wbf_scr[r:r + rows, :] = w_ref[0, r:r + rows, :].astype(BF16)

    y = jnp.dot(a_ref[...], wbf_scr[...], preferred_element_type=F32)
    o_ref[...] = x_ref[...] + mod_ref[0, g_idx:g_idx + 1, :] * y


def _out_proj(merged, w_out, layer, x2, mod, seq, g_idx, tm=1024, tn=512):
    t, d = x2.shape
    per_b = seq // tm
    est = 2 * (tm * d * 2 + d * tn * 4 + 2 * tm * tn * 4) + d * tn * 2 + tm * tn * 4 + (4 << 20)
    return pl.pallas_call(
        functools.partial(_out_proj_kernel, g_idx=g_idx),
        grid=(d // tn, t // tm),
        in_specs=[pl.BlockSpec((tm, d), lambda j, i: (i, 0)),
                  pl.BlockSpec((1, d, tn), lambda j, i: (layer, 0, j)),
                  pl.BlockSpec((tm, tn), lambda j, i: (i, j)),
                  pl.BlockSpec((1, 6, tn), lambda j, i: (i // per_b, 0, j))],
        out_specs=pl.BlockSpec((tm, tn), lambda j, i: (i, j)),
        out_shape=jax.ShapeDtypeStruct((t, d), F32),
        scratch_shapes=[pltpu.VMEM((d, tn), BF16)],
        compiler_params=_params(("parallel", "arbitrary"), est),
        name="out_proj",
    )(merged, w_out, x2, mod)


def _router_kernel(x_ref, w_ref, mod_ref, rw_ref, rb_ref, h_ref, ri_ref, rwt_ref, cnt_ref, carry_scr,
                   *, sh_idx, sc_idx):
    step = pl.program_id(0) * pl.num_programs(1) + pl.program_id(1)

    @pl.when(step == 0)
    def _():
        carry_scr[...] = jnp.zeros(carry_scr.shape, F32)

    x = x_ref[0]
    y = x * lax.rsqrt(jnp.mean(x * x, axis=-1, keepdims=True) + NORM_EPS) * w_ref[...]
    m = mod_ref[0]
    h = y * (1.0 + m[sc_idx:sc_idx + 1]) + m[sh_idx:sh_idx + 1]
    for g in range(h.shape[1] // (2 * LANE)):
        h_ref[0, :, g * LANE:(g + 1) * LANE] = _pack_bf16_pair(h[:, 2 * g * LANE:(2 * g + 1) * LANE],
                                                               h[:, (2 * g + 1) * LANE:(2 * g + 2) * LANE])
    logits = lax.dot_general(h, rw_ref[...], (((1,), (1,)), ((), ())), precision=HIGHEST,
                             preferred_element_type=F32) + rb_ref[...]
    tm = logits.shape[0]
    lane = lax.broadcasted_iota(jnp.int32, logits.shape, 1)
    big = jnp.int32(4 * LANE)
    neg = -jnp.inf

    def first_argmax(vals):
        mx = jnp.max(vals, axis=1, keepdims=True)
        idx = jnp.min(jnp.where(vals == mx, lane, big), axis=1, keepdims=True)
        return mx, idx

    lg = jnp.where(lane < N_GROUPS, logits, neg)
    gmax, gidx = first_argmax(lg)
    g_w = 1.0 / jnp.sum(jnp.exp(lg - gmax), axis=1, keepdims=True)
    lo = N_GROUPS + gidx * EXPERTS_PER_GROUP
    in_group = jnp.where(lane >= lo, jnp.where(lane < lo + EXPERTS_PER_GROUP, 1.0, 0.0), 0.0) > 0.0
    le = jnp.where(in_group, logits, neg)
    m1, i1 = first_argmax(le)
    le2 = jnp.where(lane == i1, neg, le)
    m2, i2 = first_argmax(le2)
    denom = jnp.sum(jnp.exp(le - m1), axis=1, keepdims=True)
    p1 = 1.0 / denom
    p2 = jnp.exp(m2 - m1) / denom
    w1 = p1 / (p1 + p2) * g_w
    w2 = p2 / (p1 + p2) * g_w
    e1 = i1 - N_GROUPS
    e2 = i2 - N_GROUPS

    oh1 = jnp.where(lane == e1, 1.0, 0.0)
    oh2 = jnp.where(lane == e2, 1.0, 0.0)
    both = oh1 + oh2
    r_i = lax.broadcasted_iota(jnp.int32, (tm, tm), 0)
    c_i = lax.broadcasted_iota(jnp.int32, (tm, tm), 1)
    strict = jnp.where(r_i > c_i, 1.0, 0.0).astype(BF16)
    before = jnp.dot(strict, both.astype(BF16), preferred_element_type=F32) + carry_scr[...]
    rank1 = jnp.sum(before * oh1, axis=1, keepdims=True).astype(jnp.int32)
    rank2 = jnp.sum(before * oh2, axis=1, keepdims=True).astype(jnp.int32)
    carry_scr[...] = carry_scr[...] + jnp.sum(both, axis=0, keepdims=True)
    cnt_ref[...] = carry_scr[...]
    ri_ref[0] = jnp.where(lane == 0, e1, jnp.where(lane == 1, e2, jnp.where(
        lane == 2, rank1, jnp.where(lane == 3, rank2, 0))))
    rwt_ref[0] = jnp.where(lane == 0, w1, jnp.where(lane == 1, w2, 0.0))


def _router(x, w, mod, wg, bg, wr, br, sh_idx, sc_idx, tm=256):
    b, s, d = x.shape
    rw = jnp.concatenate([wg.T, wr.T, jnp.zeros((LANE - N_GROUPS - N_EXPERTS, d), F32)], axis=0)
    rb = jnp.zeros((1, LANE), F32).at[0, :N_GROUPS].set(bg).at[0, N_GROUPS:N_GROUPS + N_EXPERTS].set(br)
    return pl.pallas_call(
        functools.partial(_router_kernel, sh_idx=sh_idx, sc_idx=sc_idx),
        grid=(b, s // tm),
        in_specs=[pl.BlockSpec((1, tm, d), lambda i, j: (i, j, 0)),
                  pl.BlockSpec((1, d), lambda i, j: (0, 0)),
                  pl.BlockSpec((1, 6, d), lambda i, j: (i, 0, 0)),
                  pl.BlockSpec((LANE, d), lambda i, j: (0, 0)),
                  pl.BlockSpec((1, LANE), lambda i, j: (0, 0))],
        out_specs=[pl.BlockSpec((1, tm, d // 2), lambda i, j: (i, j, 0)),
                   pl.BlockSpec((1, tm, LANE), lambda i, j: (i, j, 0)),
                   pl.BlockSpec((1, tm, LANE), lambda i, j: (i, j, 0)),
                   pl.BlockSpec((1, LANE), lambda i, j: (0, 0))],
        out_shape=[jax.ShapeDtypeStruct((b, s, d // 2), jnp.uint32),
                   jax.ShapeDtypeStruct((b, s, LANE), jnp.int32),
                   jax.ShapeDtypeStruct((b, s, LANE), F32),
                   jax.ShapeDtypeStruct((1, LANE), F32)],
        scratch_shapes=[pltpu.VMEM((1, LANE), F32)],
        compiler_params=_params(("arbitrary", "arbitrary"), 8 * tm * d * 4 + d * LANE * 8 + (8 << 20)),
        name="moe_router",
    )(x, w.reshape(1, d), mod, rw, rb)


def _slot_kernel(d1_ref, d2_ref, o_ref):
    def clear(i, carry):
        o_ref[i] = 0
        return carry

    lax.fori_loop(0, o_ref.shape[0], clear, 0, unroll=16)

    def place(t, carry):
        o_ref[d1_ref[t]] = t
        o_ref[d2_ref[t]] = t
        return carry

    lax.fori_loop(0, d1_ref.shape[0], place, 0, unroll=8)


def _slot_tokens(dest1, dest2, cap):
    smem = pl.BlockSpec(memory_space=pltpu.SMEM)
    return pl.pallas_call(
        _slot_kernel,
        in_specs=[smem, smem],
        out_specs=smem,
        out_shape=jax.ShapeDtypeStruct((cap,), jnp.int32),
        name="moe_slot_tokens",
    )(dest1, dest2)


def _expert_kernel(tok_ref, be_ref, nu_ref, first_ref, nxt_ref, h_hbm, wg_hbm, wu_hbm, wd_hbm, o_ref,
                   x_scr, xb_scr, sg_scr, su_scr, sd_scr, wg_scr, wu_scr, wd_scr, sem_x, sem_w, *, layer):
    blk = pl.program_id(0)
    n_used = nu_ref[0]
    rows = x_scr.shape[1]
    slot = lax.rem(blk, 2)

    def row_copy(b, s, i):
        tok = tok_ref[b * rows + i]
        return pltpu.make_async_copy(h_hbm.at[pl.ds(tok, 1)], x_scr.at[s, pl.ds(i, 1)], sem_x.at[s])

    def start_rows(b, s):
        def go(i, carry):
            row_copy(b, s, i).start()
            return carry

        lax.fori_loop(0, rows, go, 0, unroll=8)

    def weight_copies(e):
        return (pltpu.make_async_copy(wg_hbm.at[layer, e], sg_scr, sem_w.at[0]),
                pltpu.make_async_copy(wu_hbm.at[layer, e], su_scr, sem_w.at[1]),
                pltpu.make_async_copy(wd_hbm.at[layer, e], sd_scr, sem_w.at[2]))

    @pl.when(blk < n_used)
    def _():
        @pl.when(blk == 0)
        def _():
            start_rows(0, 0)
            for cp in weight_copies(be_ref[0]):
                cp.start()

        @pl.when(blk + 1 < n_used)
        def _():
            start_rows(blk + 1, 1 - slot)

        @pl.when(first_ref[blk] == 1)
        def _():
            for cp in weight_copies(be_ref[blk]):
                cp.wait()
            step = 512
            for r in range(0, sg_scr.shape[0], step):
                wg_scr[r:r + step, :] = sg_scr[r:r + step, :].astype(BF16)
                wu_scr[r:r + step, :] = su_scr[r:r + step, :].astype(BF16)
            for r in range(0, sd_scr.shape[1], step * 4):
                wd_scr[:, r:r + step * 4] = sd_scr[:, r:r + step * 4].astype(BF16)

            @pl.when(nxt_ref[blk] >= 0)
            def _():
                for cp in weight_copies(nxt_ref[blk]):
                    cp.start()

        def wait(i, carry):
            row_copy(blk, slot, i).wait()
            return carry

        lax.fori_loop(0, rows, wait, 0, unroll=8)
        groups = xb_scr.shape[1] // (2 * LANE)
        for g in range(groups):
            lo, hi = _unpack_bf16_pair(x_scr[slot, :, g * LANE:(g + 1) * LANE])
            xb_scr[:, 2 * g * LANE:(2 * g + 1) * LANE] = lo.astype(BF16)
            xb_scr[:, (2 * g + 1) * LANE:(2 * g + 2) * LANE] = hi.astype(BF16)
        xb = xb_scr[...]
        gate = jnp.dot(xb, wg_scr[...], preferred_element_type=F32)
        up = jnp.dot(xb, wu_scr[...], preferred_element_type=F32)
        act = (_silu(gate) * up).astype(BF16)
        for g in range(groups):
            y = jnp.dot(act, wd_scr[:, 2 * g * LANE:(2 * g + 2) * LANE], preferred_element_type=F32)
            o_ref[:, g * LANE:(g + 1) * LANE] = _pack_bf16_pair(y[:, :LANE], y[:, LANE:])

    @pl.when(blk >= n_used)
    def _():
        o_ref[...] = jnp.zeros(o_ref.shape, o_ref.dtype)


def _experts(slot_tok, block_expert, n_used, first, nxt, h2, wg_stack, wu_stack, wd_stack, layer):
    t, dw = h2.shape
    d = 2 * dw
    n_blocks = block_expert.shape[0]
    hid = wg_stack.shape[3]
    est = 3 * d * hid * 6 + EXP_BLOCK * d * (2 * 2 + 2 + 2 * 2 + 8) + (6 << 20)
    any_spec = pl.BlockSpec(memory_space=pl.ANY)
    return pl.pallas_call(
        functools.partial(_expert_kernel, layer=layer),
        grid_spec=pltpu.PrefetchScalarGridSpec(
            num_scalar_prefetch=5,
            grid=(n_blocks,),
            in_specs=[any_spec, any_spec, any_spec, any_spec],
            out_specs=pl.BlockSpec((EXP_BLOCK, dw), lambda i, *_: (i, 0)),
            scratch_shapes=[pltpu.VMEM((2, EXP_BLOCK, dw), jnp.uint32), pltpu.VMEM((EXP_BLOCK, d), BF16),
                            pltpu.VMEM((d, hid), F32), pltpu.VMEM((d, hid), F32), pltpu.VMEM((hid, d), F32),
                            pltpu.VMEM((d, hid), BF16), pltpu.VMEM((d, hid), BF16), pltpu.VMEM((hid, d), BF16),
                            pltpu.SemaphoreType.DMA((2,)), pltpu.SemaphoreType.DMA((3,))]),
        out_shape=jax.ShapeDtypeStruct((n_blocks * EXP_BLOCK, dw), jnp.uint32),
        compiler_params=_params(("arbitrary",), est),
        name="moe_experts",
    )(slot_tok, block_expert, n_used, first, nxt, h2, wg_stack, wu_stack, wd_stack)


def _combine_kernel(d1_ref, d2_ref, ys_hbm, x_ref, rw_ref, mod_ref, fw_ref, o_ref, a_scr, b_scr, sem,
                    *, g_idx, final):
    i = pl.program_id(0)
    rows = a_scr.shape[1]
    slot = lax.rem(i, 2)

    def copies(step, s, r):
        t = step * rows + r
        return (pltpu.make_async_copy(ys_hbm.at[pl.ds(d1_ref[t], 1)], a_scr.at[s, pl.ds(r, 1)], sem.at[0, s]),
                pltpu.make_async_copy(ys_hbm.at[pl.ds(d2_ref[t], 1)], b_scr.at[s, pl.ds(r, 1)], sem.at[1, s]))

    def start_rows(step, s):
        def go(r, carry):
            ca, cb = copies(step, s, r)
            ca.start()
            cb.start()
            return carry

        lax.fori_loop(0, rows, go, 0, unroll=8)

    @pl.when(i == 0)
    def _():
        start_rows(0, 0)

    @pl.when(i + 1 < pl.num_programs(0))
    def _():
        start_rows(i + 1, 1 - slot)

    def wait(r, carry):
        ca, cb = copies(i, slot, r)
        ca.wait()
        cb.wait()
        return carry

    lax.fori_loop(0, rows, wait, 0, unroll=8)
    rw = rw_ref[...]
    w1 = rw[:, 0:1]
    w2 = rw[:, 1:2]
    d = x_ref.shape[1]
    ssq = jnp.zeros((rows, 1), F32)
    for g in range(d // (2 * LANE)):
        halves_a = _unpack_bf16_pair(a_scr[slot, :, g * LANE:(g + 1) * LANE])
        halves_b = _unpack_bf16_pair(b_scr[slot, :, g * LANE:(g + 1) * LANE])
        for half in range(2):
            c0 = (2 * g + half) * LANE
            moe = w1 * halves_a[half] + w2 * halves_b[half]
            xn = x_ref[:, c0:c0 + LANE] + mod_ref[0, g_idx:g_idx + 1, c0:c0 + LANE] * moe
            o_ref[:, c0:c0 + LANE] = xn
            if final:
                ssq = ssq + jnp.sum(xn * xn, axis=-1, keepdims=True)
    if final:
        o_ref[...] = o_ref[...] * lax.rsqrt(ssq * (1.0 / d) + NORM_EPS) * fw_ref[...]


def _combine(dest1, dest2, ys, x2, route_w, mod, final_w, seq, g_idx, final, tm=128):
    t, d = x2.shape
    per_b = seq // tm
    est = 2 * (2 * tm * d * 4 + tm * LANE * 4) + 4 * tm * d * 4 + (6 << 20)
    return pl.pallas_call(
        functools.partial(_combine_kernel, g_idx=g_idx, final=final),
        grid_spec=pltpu.PrefetchScalarGridSpec(
            num_scalar_prefetch=2,
            grid=(t // tm,),
            in_specs=[pl.BlockSpec(memory_space=pl.ANY),
                      pl.BlockSpec((tm, d), lambda i, d1, d2: (i, 0)),
                      pl.BlockSpec((tm, LANE), lambda i, d1, d2: (i, 0)),
                      pl.BlockSpec((1, 6, d), lambda i, d1, d2: (i // per_b, 0, 0)),
                      pl.BlockSpec((1, d), lambda i, d1, d2: (0, 0))],
            out_specs=pl.BlockSpec((tm, d), lambda i, d1, d2: (i, 0)),
            scratch_shapes=[pltpu.VMEM((2, tm, d // 2), jnp.uint32), pltpu.VMEM((2, tm, d // 2), jnp.uint32),
                            pltpu.SemaphoreType.DMA((2, 2))]),
        out_shape=jax.ShapeDtypeStruct((t, d), F32),
        compiler_params=_params(("arbitrary",), est),
        name="moe_combine",
    )(dest1, dest2, ys, x2, route_w, mod, final_w.reshape(1, d))


def _small_in_weights(wt_stack, layer):
    d = wt_stack.shape[2]
    bounds = np.cumsum((0,) + IN_SIZES_HEAD)
    seg = [wt_stack[layer, bounds[i]:bounds[i + 1], :] for i in range(3, len(IN_SIZES_HEAD))]
    dt, q_lat, kv_lat, k_rope, sq, sk, sv = seg
    z64 = jnp.zeros((SWA_HEAD_DIM, d), wt_stack.dtype)

    def spread(w):
        h0, h1 = w[:SWA_HEAD_DIM], w[SWA_HEAD_DIM:]
        return jnp.concatenate([h0, z64, z64, h0, h1, z64, z64, h1], axis=0)

    half = MLA_ROPE // 2
    rot = jnp.concatenate([-k_rope[half:], k_rope[:half]], axis=0)
    small = jnp.concatenate([sq, spread(sk), spread(sv), kv_lat,
                             jnp.zeros((OFF_S["qlat"] - OFF_S["kvlat"] - MLA_KV_LORA, d), wt_stack.dtype),
                             q_lat], axis=0)
    tail = jnp.concatenate([k_rope, z64, rot, z64, dt,
                            jnp.zeros((LANE - SSM_HEADS, d), wt_stack.dtype)], axis=0)
    return small[None], tail[None]


def _extend_wq(wq_b):
    k = wq_b.shape[0]
    w = wq_b.reshape(k, MLA_HEADS, MLA_NOPE + MLA_ROPE)
    z = jnp.zeros((k, MLA_HEADS, LANE - MLA_ROPE), wq_b.dtype)
    rope = w[..., MLA_NOPE:]
    half = MLA_ROPE // 2
    rot = jnp.concatenate([-rope[..., half:], rope[..., :half]], axis=-1)
    return jnp.concatenate([w[..., :MLA_NOPE], rope, z, rot, z], axis=-1).reshape(k, -1).astype(BF16)


def _mixer(x, mod, pos_col, pos_row, cos_t, sin_t, norm_w, w_in_stack, layer, conv_w, conv_b, dt_bias, a_log,
           d_skip, ssm_norm, mla_q_norm, mla_wq_b, mla_kv_norm, mla_wkv_b, swa_sinks, w_branch, w_out):
    b, s, d = x.shape
    t = b * s
    swa_slopes, moba_slopes = _alibi_slopes()
    h = _norm_mod(x, norm_w, mod, sh_idx=0, sc_idx=1)
    h2 = h.reshape(t, d)
    w_small, w_tail = _small_in_weights(w_in_stack, layer)
    proj_a = _wproj(h2, w_in_stack, layer, 0, A_COLS, "in_proj_head")
    proj_g = _wproj(h2, w_in_stack, layer, GATE_COL0, N_BRANCH * d, "in_proj_gates")
    proj_s = _wproj(h2, w_small, 0, 0, S_COLS, "in_proj_small", tn=768)
    tail = _wproj(h2, w_tail, 0, 0, 3 * LANE, "in_proj_tail", tn=3 * LANE, out_dtype=F32)
    proj_a3 = proj_a.reshape(b, s, A_COLS)
    proj_s3 = proj_s.reshape(b, s, S_COLS)
    tail3 = tail.reshape(b, s, 3 * LANE)

    o_moba = _flash(proj_a3, proj_a3, proj_a3, heads=MOBA_HEADS, dq=MOBA_HEAD_DIM, dv=MOBA_HEAD_DIM,
                    q_off=OFF_A["mq"], k_off=OFF_A["mk"], v_off=OFF_A["mv"],
                    moba=True, q_scale=float(MOBA_HEAD_DIM ** -0.5) * LOG2E,
                    slopes=jnp.asarray(moba_slopes * np.float32(LOG2E)),
                    pos_col=pos_col, pos_row=pos_row, name="moba_attention")
    o_ssm = _ssd(proj_a3, tail3, OFF_A, conv_w, conv_b, dt_bias, a_log, d_skip, ssm_norm)
    q_m, k_m, v_m = _mla_project(proj_s, tail, OFF_S, mla_q_norm, _extend_wq(mla_wq_b), mla_kv_norm,
                                 mla_wkv_b.astype(BF16), cos_t, sin_t)
    o_mla = _flash(q_m.reshape(b, s, -1), k_m.reshape(b, s, -1), v_m.reshape(b, s, -1),
                   heads=MLA_HEADS, dq=MLA_QK, dv=MLA_V, q_off=0, k_off=0, v_off=0, name="mla_attention")
    o_swa = _swa(proj_s3, OFF_S, swa_sinks, pos_col, pos_row, swa_slopes)

    branches = [o.reshape(t, BRANCH_W) for o in (o_moba, o_ssm, o_mla, o_swa)]
    merged = _merge(branches, w_branch, layer, proj_g, d)
    return _out_proj(merged, w_out, layer, x.reshape(t, d), mod, s, g_idx=2)


def _moe(x2, seq, mod, norm_w, wg, bg, wr, br, w_gate, w_up, w_down, layer, final_w, final):
    t, d = x2.shape
    b = t // seq
    h, route_i, route_w, counts = _router(x2.reshape(b, seq, d), norm_w, mod, wg, bg, wr, br,
                                          sh_idx=3, sc_idx=4)
    route_i = route_i.reshape(t, LANE)
    counts = counts[0, :N_EXPERTS].astype(jnp.int32)
    padded = (counts + EXP_BLOCK - 1) // EXP_BLOCK * EXP_BLOCK
    pad_end = jnp.cumsum(padded)
    pad_start = pad_end - padded
    n_blocks = -(-(t * 2) // EXP_BLOCK) + N_EXPERTS
    dest1 = pad_start[route_i[:, 0]] + route_i[:, 2]
    dest2 = pad_start[route_i[:, 1]] + route_i[:, 3]
    block_row0 = jnp.arange(n_blocks, dtype=jnp.int32) * EXP_BLOCK
    block_expert = jnp.minimum(jnp.sum((pad_end[None, :] <= block_row0[:, None]).astype(jnp.int32), axis=1),
                               N_EXPERTS - 1)
    n_used = (pad_end[-1:] // EXP_BLOCK).astype(jnp.int32)
    first = (block_expert != jnp.concatenate([jnp.full((1,), -1, jnp.int32), block_expert[:-1]])).astype(jnp.int32)
    e_ids = jnp.arange(N_EXPERTS, dtype=jnp.int32)
    later = jnp.where((counts > 0)[None, :] & (e_ids[None, :] > e_ids[:, None]), e_ids[None, :], N_EXPERTS)
    nxt_e = jnp.min(later, axis=1)
    nxt = jnp.where(nxt_e == N_EXPERTS, -1, nxt_e)[block_expert].astype(jnp.int32)
    slot_tok = _slot_tokens(dest1, dest2, n_blocks * EXP_BLOCK)
    ys = _experts(slot_tok, block_expert, n_used, first, nxt, h.reshape(t, d // 2), w_gate, w_up, w_down, layer)
    return _combine(dest1, dest2, ys, x2, route_w.reshape(t, LANE), mod, final_w, seq, g_idx=5, final=final)


def kernel(x, c, positions, ada_w, ada_b, norm_mix, norm_ffn, w_in, conv_w, conv_b, dt_bias, a_log, d_skip,
           ssm_norm, mla_q_norm, mla_wq_b, mla_kv_norm, mla_wkv_b, swa_sinks, w_branch, w_out,
           router_group_w, router_group_b, router_w, router_b, exp_w_gate, exp_w_up, exp_w_down, final_norm):
    b, s, d = x.shape
    depth = ada_w.shape[0]
    mods = _ada_mod(c, ada_w, ada_b)
    pos_f = positions.astype(F32)
    pos_col = pos_f.reshape(b, s, 1)
    pos_row = pos_f.reshape(b, 1, s)
    cos_t, sin_t = _rope_tables(pos_f.reshape(b * s, 1))
    w_in_t = jnp.swapaxes(w_in, 1, 2)
    for l in range(depth):
        x2 = _mixer(x, mods[l], pos_col, pos_row, cos_t, sin_t, norm_mix[l], w_in_t, l, conv_w[l], conv_b[l],
                    dt_bias[l], a_log[l], d_skip[l], ssm_norm[l], mla_q_norm[l], mla_wq_b[l],
                    mla_kv_norm[l], mla_wkv_b[l], swa_sinks[l], w_branch, w_out)
        x2 = _moe(x2, s, mods[l], norm_ffn[l], router_group_w[l], router_group_b[l], router_w[l],
                  router_b[l], exp_w_gate, exp_w_up, exp_w_down, l, final_norm,
                  final=(l == depth - 1))
        x = x2.reshape(b, s, d)
    return x
```

```python
import functools
import math

import numpy as np
import jax
import jax.numpy as jnp
from jax import lax
from jax.experimental import pallas as pl
from jax.experimental.pallas import tpu as pltpu

F32 = jnp.float32
BF16 = jnp.bfloat16
HIGHEST = lax.Precision.HIGHEST

MOBA_HEADS = 8
MOBA_HEAD_DIM = 128
MOBA_BLOCK = 256
MOBA_TOPK = 3
SSM_D_INNER = 1024
SSM_HEAD_DIM = 64
SSM_HEADS = SSM_D_INNER // SSM_HEAD_DIM
SSM_GROUPS = 2
SSM_STATE = 128
SSM_CONV = 4
SSM_CHUNK = 128
SSM_BC = 2 * SSM_GROUPS * SSM_STATE
MLA_HEADS = 8
MLA_Q_LORA = 768
MLA_KV_LORA = 512
MLA_NOPE = 128
MLA_ROPE = 64
MLA_V = 128
ROPE_THETA = 10000.0
SWA_HEADS = 16
SWA_KV_HEADS = 2
SWA_HEAD_DIM = 64
SWA_WINDOW = 128
N_BRANCH = 4
BRANCH_W = 1024
N_GROUPS = 4
EXPERTS_PER_GROUP = 8
N_EXPERTS = N_GROUPS * EXPERTS_PER_GROUP
EXPERT_HIDDEN = 512
EXP_BLOCK = 256
NORM_EPS = 1e-6
N_ALIBI = MOBA_HEADS + SWA_HEADS

LOG2E = math.log2(math.e)
MASKED = -1e30
LANE = 128
VMEM_CAP = 60 * 1024 * 1024
ATTN_TILE = 512
FLASH_HEADS = 4
ONES_ROWS = 16
MLA_QK = 256


def _alibi_slopes():
    i = np.arange(1, N_ALIBI + 1, dtype=np.float64)
    s = np.exp2(-8.0 * i / N_ALIBI).astype(np.float32)
    return s[:SWA_HEADS], s[SWA_HEADS:]


IN_SIZES_HEAD = (3 * MOBA_HEADS * MOBA_HEAD_DIM, SSM_D_INNER, SSM_D_INNER + SSM_BC, SSM_HEADS, MLA_Q_LORA,
                 MLA_KV_LORA, MLA_ROPE, SWA_HEADS * SWA_HEAD_DIM, SWA_KV_HEADS * SWA_HEAD_DIM,
                 SWA_KV_HEADS * SWA_HEAD_DIM)
A_COLS = IN_SIZES_HEAD[0] + IN_SIZES_HEAD[1] + IN_SIZES_HEAD[2]
GATE_COL0 = sum(IN_SIZES_HEAD)
OFF_A = dict(mq=0, mk=1024, mv=2048, z=3072, xs=4096, bc=5120)
OFF_S = dict(sq=0, skx=1024, svx=1536, kvlat=2048, qlat=3072)
S_COLS = 3840


def _params(sem, est_bytes):
    limit = int(min(VMEM_CAP, max(est_bytes, 16 * 1024 * 1024)))
    return pltpu.CompilerParams(dimension_semantics=sem, vmem_limit_bytes=limit)


def _silu(v):
    return v * (1.0 / (1.0 + jnp.exp(-v)))


def _pack_bf16_pair(lo, hi):
    lo_bits = pltpu.bitcast(lo.astype(BF16).astype(F32), jnp.uint32)
    hi_bits = pltpu.bitcast(hi.astype(BF16).astype(F32), jnp.uint32)
    return hi_bits | lax.shift_right_logical(lo_bits, jnp.uint32(16))


def _unpack_bf16_pair(word):
    lo = pltpu.bitcast(lax.shift_left(word, jnp.uint32(16)), F32)
    hi = pltpu.bitcast(word & jnp.uint32(0xFFFF0000), F32)
    return lo, hi


def _ada_kernel(c_ref, w_ref, b_ref, o_ref):
    k = pl.program_id(1)

    @pl.when(k == 0)
    def _():
        o_ref[0] = jnp.broadcast_to(b_ref[0], o_ref.shape[1:])

    o_ref[0] += jnp.dot(c_ref[...].astype(BF16), w_ref[0].astype(BF16), preferred_element_type=F32)


def _ada_mod(c, ada_w, ada_b):
    depth, d, n = ada_w.shape
    b = c.shape[0]
    c8 = jnp.zeros((8, d), F32).at[:b].set(c)
    tk = LANE
    out = pl.pallas_call(
        _ada_kernel,
        grid=(depth, d // tk),
        in_specs=[pl.BlockSpec((8, tk), lambda l, k: (0, k)),
                  pl.BlockSpec((1, tk, n), lambda l, k: (l, k, 0)),
                  pl.BlockSpec((1, 1, n), lambda l, k: (l, 0, 0))],
        out_specs=pl.BlockSpec((1, 8, n), lambda l, k: (l, 0, 0)),
        out_shape=jax.ShapeDtypeStruct((depth, 8, n), F32),
        compiler_params=_params(("parallel", "arbitrary"), 3 * tk * n * 4 + 32 * n * 4 + (4 << 20)),
        name="ada_mod",
    )(c8, ada_w, ada_b.reshape(depth, 1, n))
    return out[:, :b].reshape(depth, b, 6, d)


def _norm_mod_kernel(x_ref, w_ref, mod_ref, o_ref, *, sh_idx, sc_idx):
    x = x_ref[0]
    y = x * lax.rsqrt(jnp.mean(x * x, axis=-1, keepdims=True) + NORM_EPS) * w_ref[...]
    m = mod_ref[0]
    o_ref[0] = (y * (1.0 + m[sc_idx:sc_idx + 1]) + m[sh_idx:sh_idx + 1]).astype(o_ref.dtype)


def _norm_mod(x, w, mod, sh_idx, sc_idx, tm=256):
    b, s, d = x.shape
    return pl.pallas_call(
        functools.partial(_norm_mod_kernel, sh_idx=sh_idx, sc_idx=sc_idx),
        grid=(b, s // tm),
        in_specs=[pl.BlockSpec((1, tm, d), lambda i, j: (i, j, 0)),
                  pl.BlockSpec((1, d), lambda i, j: (0, 0)),
                  pl.BlockSpec((1, 6, d), lambda i, j: (i, 0, 0))],
        out_specs=pl.BlockSpec((1, tm, d), lambda i, j: (i, j, 0)),
        out_shape=jax.ShapeDtypeStruct((b, s, d), BF16),
        compiler_params=_params(("parallel", "parallel"), 6 * tm * d * 4 + (4 << 20)),
        name="norm_mod",
    )(x, w.reshape(1, d), mod)


def _wproj_kernel(a_ref, w_ref, o_ref, wbf_scr):
    tn = wbf_scr.shape[0]
    rows = 64

    @pl.when(pl.program_id(1) == 0)
    def _():
        for r in range(0, tn, rows):
            wbf_scr[r:r + rows, :] = w_ref[0, r:r + rows, :].astype(BF16)

    o_ref[...] = lax.dot_general(a_ref[...], wbf_scr[...], (((1,), (1,)), ((), ())),
                                 preferred_element_type=F32).astype(o_ref.dtype)


def _wproj(a, wt_stack, layer, col0, n_cols, name, tm=1024, tn=512, out_dtype=BF16):
    m, k = a.shape
    assert n_cols % tn == 0 and m % tm == 0 and col0 % 8 == 0
    if col0 % tn == 0:
        w_spec = pl.BlockSpec((1, tn, k), lambda j, i: (layer, col0 // tn + j, 0))
    else:
        w_spec = pl.BlockSpec((pl.Element(1), pl.Element(tn), pl.Element(k)),
                              lambda j, i: (layer, pl.multiple_of(col0 + j * tn, 8), 0))
    est = 2 * (tm * k * 2 + tn * k * 4 + tm * tn * 2) + k * tn * 2 + tm * tn * 4 + (6 << 20)
    return pl.pallas_call(
        _wproj_kernel,
        grid=(n_cols // tn, m // tm),
        in_specs=[pl.BlockSpec((tm, k), lambda j, i: (i, 0)), w_spec],
        out_specs=pl.BlockSpec((tm, tn), lambda j, i: (i, j)),
        out_shape=jax.ShapeDtypeStruct((m, n_cols), out_dtype),
        scratch_shapes=[pltpu.VMEM((tn, k), BF16)],
        compiler_params=_params(("parallel", "arbitrary"), est),
        name=name,
    )(a, wt_stack)


def _rope_kernel(pos_ref, freq_ref, cos_ref, sin_ref):
    ang = pos_ref[...] * freq_ref[...]
    lane = lax.broadcasted_iota(jnp.int32, ang.shape, 1)
    live = lane < MLA_ROPE
    cos_ref[...] = jnp.where(live, jnp.cos(ang), 0.0)
    sin_ref[...] = jnp.where(live, jnp.sin(ang), 0.0)


def _rope_tables(pos_col):
    t = pos_col.shape[0]
    half = MLA_ROPE // 2
    inv = ROPE_THETA ** (-np.arange(half, dtype=np.float32) / half)
    freq = np.zeros((1, LANE), np.float32)
    freq[0, :half] = inv
    freq[0, half:2 * half] = inv
    tm = 512
    return pl.pallas_call(
        _rope_kernel,
        grid=(t // tm,),
        in_specs=[pl.BlockSpec((tm, 1), lambda i: (i, 0)),
                  pl.BlockSpec((1, LANE), lambda i: (0, 0))],
        out_specs=[pl.BlockSpec((tm, LANE), lambda i: (i, 0))] * 2,
        out_shape=[jax.ShapeDtypeStruct((t, LANE), F32)] * 2,
        compiler_params=_params(("parallel",), 16 << 20),
        name="rope_tables",
    )(pos_col, jnp.asarray(freq))


def _mla_q_kernel(x_ref, nw_ref, w_ref, cos_ref, sin_ref, o_ref, *, scale):
    x = x_ref[...].astype(F32)
    y = x * lax.rsqrt(jnp.mean(x * x, axis=-1, keepdims=True) + NORM_EPS) * nw_ref[...]
    r = jnp.dot(y.astype(BF16), w_ref[...], preferred_element_type=F32)
    cos = cos_ref[...]
    sin = sin_ref[...]
    for h in range(MLA_HEADS):
        base = h * 3 * LANE
        nope = r[:, base:base + LANE]
        rope = r[:, base + LANE:base + 2 * LANE]
        rot = r[:, base + 2 * LANE:base + 3 * LANE]
        o_ref[:, h * MLA_QK:h * MLA_QK + LANE] = (nope * scale).astype(o_ref.dtype)
        o_ref[:, h * MLA_QK + LANE:(h + 1) * MLA_QK] = (
            (rope * cos + rot * sin) * scale).astype(o_ref.dtype)


def _mla_kv_kernel(x_ref, nw_ref, w_ref, kr_ref, krot_ref, cos_ref, sin_ref, k_ref, v_ref):
    x = x_ref[...].astype(F32)
    y = x * lax.rsqrt(jnp.mean(x * x, axis=-1, keepdims=True) + NORM_EPS) * nw_ref[...]
    r = jnp.dot(y.astype(BF16), w_ref[...], preferred_element_type=F32)
    kr = (kr_ref[...] * cos_ref[...] + krot_ref[...] * sin_ref[...]).astype(k_ref.dtype)
    for h in range(MLA_HEADS):
        base = h * (MLA_NOPE + MLA_V)
        k_ref[:, h * MLA_QK:h * MLA_QK + LANE] = r[:, base:base + MLA_NOPE].astype(k_ref.dtype)
        k_ref[:, h * MLA_QK + LANE:(h + 1) * MLA_QK] = kr
        v_ref[:, h * MLA_V:(h + 1) * MLA_V] = r[:, base + MLA_NOPE:base + MLA_NOPE + MLA_V].astype(
            v_ref.dtype)


def _mla_project(proj, tail, off, q_norm, wq_ext, kv_norm, wkv, cos_t, sin_t, tm=512):
    t = proj.shape[0]
    scale = float((MLA_NOPE + MLA_ROPE) ** -0.5) * LOG2E
    q = pl.pallas_call(
        functools.partial(_mla_q_kernel, scale=scale),
        grid=(t // tm,),
        in_specs=[pl.BlockSpec((tm, MLA_Q_LORA), lambda i: (i, off["qlat"] // MLA_Q_LORA)),
                  pl.BlockSpec((1, MLA_Q_LORA), lambda i: (0, 0)),
                  pl.BlockSpec(wq_ext.shape, lambda i: (0, 0)),
                  pl.BlockSpec((tm, LANE), lambda i: (i, 0)),
                  pl.BlockSpec((tm, LANE), lambda i: (i, 0))],
        out_specs=pl.BlockSpec((tm, MLA_HEADS * MLA_QK), lambda i: (i, 0)),
        out_shape=jax.ShapeDtypeStruct((t, MLA_HEADS * MLA_QK), BF16),
        compiler_params=_params(("parallel",), 40 << 20),
        name="mla_q_proj",
    )(proj, q_norm.reshape(1, -1), wq_ext, cos_t, sin_t)
    k, v = pl.pallas_call(
        _mla_kv_kernel,
        grid=(t // tm,),
        in_specs=[pl.BlockSpec((tm, MLA_KV_LORA), lambda i: (i, off["kvlat"] // MLA_KV_LORA)),
                  pl.BlockSpec((1, MLA_KV_LORA), lambda i: (0, 0)),
                  pl.BlockSpec(wkv.shape, lambda i: (0, 0)),
                  pl.BlockSpec((tm, LANE), lambda i: (i, 0)),
                  pl.BlockSpec((tm, LANE), lambda i: (i, 1)),
                  pl.BlockSpec((tm, LANE), lambda i: (i, 0)),
                  pl.BlockSpec((tm, LANE), lambda i: (i, 0))],
        out_specs=[pl.BlockSpec((tm, MLA_HEADS * MLA_QK), lambda i: (i, 0)),
                   pl.BlockSpec((tm, MLA_HEADS * MLA_V), lambda i: (i, 0))],
        out_shape=[jax.ShapeDtypeStruct((t, MLA_HEADS * MLA_QK), BF16),
                   jax.ShapeDtypeStruct((t, MLA_HEADS * MLA_V), BF16)],
        compiler_params=_params(("parallel",), 40 << 20),
        name="mla_kv_proj",
    )(proj, kv_norm.reshape(1, -1), wkv, tail, tail, cos_t, sin_t)
    return q, k, v


def _flash_kernel(*refs, moba, q_scale, dq, dv):
    if moba:
        slope_ref, q_ref, k_ref, v_ref, posk_ref, posq_ref, o_ref, vt_scr, kmean_scr, sel_scr = refs
    else:
        q_ref, k_ref, v_ref, o_ref, vt_scr = refs
    tile = ATTN_TILE
    seq = k_ref.shape[1]
    sub = tile // MOBA_BLOCK
    nblk = seq // MOBA_BLOCK
    qi = pl.program_id(2)
    heads = range(FLASH_HEADS)

    @pl.when(qi == 0)
    def _():
        for j in heads:
            for c in range(seq // tile):
                vt_scr[j, c, 0:dv, :] = (
                    v_ref[0, c * tile:(c + 1) * tile, j * dv:(j + 1) * dv].astype(F32).T.astype(BF16))
                vt_scr[j, c, dv:dv + ONES_ROWS, :] = jnp.ones((ONES_ROWS, tile), BF16)
            if moba:
                for n in range(nblk):
                    kblk = k_ref[0, n * MOBA_BLOCK:(n + 1) * MOBA_BLOCK, j * dq:(j + 1) * dq].astype(F32)
                    kmean_scr[j, n:n + 1, :] = jnp.mean(kblk, axis=0, keepdims=True)

    q = []
    for j in heads:
        q_raw = q_ref[0, :, j * dq:(j + 1) * dq]
        q.append(q_raw if q_scale is None else (q_raw.astype(F32) * q_scale).astype(BF16))
        if moba:
            gate = lax.dot_general(kmean_scr[j], q_raw.astype(F32), (((1,), (1,)), ((), ())),
                                   precision=HIGHEST, preferred_element_type=F32)
            n_iota = lax.broadcasted_iota(jnp.int32, gate.shape, 0)
            q_blk = qi * sub + lax.broadcasted_iota(jnp.int32, gate.shape, 1) // MOBA_BLOCK
            beaten = jnp.zeros(gate.shape, F32)
            for m in range(nblk):
                gm = gate[m:m + 1, :]
                wins = jnp.where(gm > gate, 1.0, jnp.where(gm == gate, jnp.where(m < n_iota, 1.0, 0.0), 0.0))
                beaten = beaten + jnp.where(m < q_blk, wins, 0.0)
            sel_scr[j] = jnp.where(beaten < MOBA_TOPK, jnp.where(n_iota < q_blk, 0.0, MASKED), MASKED)
    if moba:
        slope = [slope_ref[pl.program_id(1) * FLASH_HEADS + j] for j in heads]
        pq = [posq_ref[0] * slope[j] for j in heads]

    def scores(j, c):
        start = pl.multiple_of(c * tile, tile)
        kb = k_ref[0, pl.ds(start, tile), j * dq:(j + 1) * dq]
        st = lax.dot_general(kb, q[j], (((1,), (1,)), ((), ())), preferred_element_type=F32)
        if moba:
            pk = posk_ref[0, pl.ds(start, tile), :] * slope[j]
            st = st - jnp.abs(pq[j] - pk)
        return st

    def block_bias(j, c):
        rows = [jnp.broadcast_to(sel_scr[j, pl.ds(c * sub + r, 1), :], (MOBA_BLOCK, tile)) for r in range(sub)]
        return jnp.concatenate(rows, axis=0)

    key_i = lax.broadcasted_iota(jnp.int32, (tile, tile), 0)
    qry_i = lax.broadcasted_iota(jnp.int32, (tile, tile), 1)
    causal_bias = jnp.where(key_i <= qry_i, 0.0, MASKED)
    init = []
    for j in heads:
        st = scores(j, qi)
        if moba:
            same_blk = (key_i // MOBA_BLOCK) == (qry_i // MOBA_BLOCK)
            st = st + jnp.where(same_blk, causal_bias, block_bias(j, qi))
        else:
            st = st + causal_bias
        m0 = jnp.max(st, axis=0, keepdims=True)
        p = jnp.exp2(st - m0)
        init.append((m0, jnp.dot(vt_scr[j, qi], p.astype(BF16), preferred_element_type=F32)))

    def body(c, carry):
        s_c = [scores(j, c) for j in heads]
        if moba:
            s_c = [s_c[j] + block_bias(j, c) for j in heads]
        m_new = [jnp.maximum(carry[j][0], jnp.max(s_c[j], axis=0, keepdims=True)) for j in heads]
        p_c = [jnp.exp2(s_c[j] - m_new[j]).astype(BF16) for j in heads]
        out = []
        for j in heads:
            m_prev, acc = carry[j]
            alpha = jnp.exp2(m_prev - m_new[j])
            acc_new = alpha * acc + jnp.dot(vt_scr[j, c], p_c[j], preferred_element_type=F32)
            out.append((m_new[j], acc_new))
        return tuple(out)

    final = lax.fori_loop(0, qi, body, tuple(init))
    for j in heads:
        _, acc = final[j]
        o_ref[0, :, j * dv:(j + 1) * dv] = (acc[0:dv] * (1.0 / acc[dv:dv + 1])).T.astype(o_ref.dtype)


def _flash(q, k, v, *, heads, dq, dv, q_off, k_off, v_off, moba=False, q_scale=None,
           slopes=None, pos_col=None, pos_row=None, name="flash"):
    b, s, _ = q.shape
    tile = ATTN_TILE
    hp = FLASH_HEADS
    assert heads % hp == 0 and q_off % (hp * dq) == 0 and k_off % (hp * dq) == 0 and v_off % (hp * dv) == 0
    qb, kb, vb = q_off // (hp * dq), k_off // (hp * dq), v_off // (hp * dv)
    in_specs = [pl.BlockSpec((1, tile, hp * dq), lambda bi, h, i: (bi, i, qb + h)),
                pl.BlockSpec((1, s, hp * dq), lambda bi, h, i: (bi, 0, kb + h)),
                pl.BlockSpec((1, s, hp * dv), lambda bi, h, i: (bi, 0, vb + h))]
    args = [q, k, v]
    scratch = [pltpu.VMEM((hp, s // tile, dv + ONES_ROWS, tile), BF16)]
    if moba:
        in_specs = [pl.BlockSpec(memory_space=pltpu.SMEM)] + in_specs + [
            pl.BlockSpec((1, s, 1), lambda bi, h, i: (bi, 0, 0)),
            pl.BlockSpec((1, 1, tile), lambda bi, h, i: (bi, 0, i))]
        args = [slopes] + args + [pos_col, pos_row]
        scratch += [pltpu.VMEM((hp, s // MOBA_BLOCK, dq), F32), pltpu.VMEM((hp, s // MOBA_BLOCK, tile), F32)]
    est = hp * (2 * (s * dq * 2 + s * dv * 2) + s * dv * 2) + s * LANE * 4 * 2 + (20 << 20)
    return pl.pallas_call(
        functools.partial(_flash_kernel, moba=moba, q_scale=q_scale, dq=dq, dv=dv),
        grid=(b, heads // hp, s // tile),
        in_specs=in_specs,
        out_specs=pl.BlockSpec((1, tile, hp * dv), lambda bi, h, i: (bi, i, h)),
        out_shape=jax.ShapeDtypeStruct((b, s, heads * dv), BF16),
        scratch_shapes=scratch,
        compiler_params=_params(("parallel", "parallel", "arbitrary"), est),
        name=name,
    )(*args)


def _swa_kernel(sink_ref, q_ref, kp_ref, kc_ref, vp_ref, vc_ref, pq_ref, pkp_ref, pkc_ref, o_ref,
                *, slopes):
    w = SWA_WINDOW
    n = pl.program_id(1)
    kx = jnp.concatenate([kp_ref[0], kc_ref[0]], axis=0)
    vx = jnp.concatenate([vp_ref[0], vc_ref[0]], axis=0)
    pk = jnp.concatenate([pkp_ref[0], pkc_ref[0]], axis=1)
    dist = jnp.abs(pq_ref[0] - pk)
    qi = lax.broadcasted_iota(jnp.int32, dist.shape, 0)
    kk = lax.broadcasted_iota(jnp.int32, dist.shape, 1)
    allowed = jnp.where(kk > qi, jnp.where(kk <= qi + w, 0.0, MASKED), MASKED)
    mask_bias = jnp.where(kk >= w, allowed, jnp.where(n > 0, allowed, MASKED))
    scale = float(SWA_HEAD_DIM ** -0.5)
    rep = SWA_HEADS // SWA_KV_HEADS
    heads = range(SWA_HEADS)
    col = [(2 * (h // rep) + h % 2) * LANE for h in heads]
    s = [lax.dot_general(q_ref[0, :, (h // 2) * LANE:(h // 2 + 1) * LANE], kx[:, col[h]:col[h] + LANE],
                         (((1,), (1,)), ((), ())), preferred_element_type=F32) for h in heads]
    s = [s[h] * scale + (mask_bias - float(slopes[h]) * dist) for h in heads]
    m = [jnp.maximum(jnp.max(s[h], axis=1, keepdims=True), sink_ref[h]) for h in heads]
    p = [jnp.exp(s[h] - m[h]) for h in heads]
    inv = [1.0 / (jnp.sum(p[h], axis=1, keepdims=True) + jnp.exp(sink_ref[h] - m[h])) for h in heads]
    p = [(p[h] * inv[h]).astype(BF16) for h in heads]
    for pair in range(SWA_HEADS // 2):
        h0, h1 = 2 * pair, 2 * pair + 1
        acc = (jnp.dot(p[h0], vx[:, col[h0]:col[h0] + LANE], preferred_element_type=F32)
               + jnp.dot(p[h1], vx[:, col[h1]:col[h1] + LANE], preferred_element_type=F32))
        o_ref[0, :, pair * LANE:(pair + 1) * LANE] = acc.astype(o_ref.dtype)


def _swa(proj3, off, sinks, pos_col, pos_row, slopes):
    b, s, _ = proj3.shape
    w = SWA_WINDOW
    qw = SWA_HEADS * SWA_HEAD_DIM
    kw = 4 * LANE
    prev = lambda j: jnp.maximum(j - 1, 0)
    return pl.pallas_call(
        functools.partial(_swa_kernel, slopes=tuple(float(v) for v in slopes)),
        grid=(b, s // w),
        in_specs=[pl.BlockSpec(memory_space=pltpu.SMEM),
                  pl.BlockSpec((1, w, qw), lambda i, j: (i, j, off["sq"] // qw)),
                  pl.BlockSpec((1, w, kw), lambda i, j: (i, prev(j), off["skx"] // kw)),
                  pl.BlockSpec((1, w, kw), lambda i, j: (i, j, off["skx"] // kw)),
                  pl.BlockSpec((1, w, kw), lambda i, j: (i, prev(j), off["svx"] // kw)),
                  pl.BlockSpec((1, w, kw), lambda i, j: (i, j, off["svx"] // kw)),
                  pl.BlockSpec((1, w, 1), lambda i, j: (i, j, 0)),
                  pl.BlockSpec((1, 1, w), lambda i, j: (i, 0, prev(j))),
                  pl.BlockSpec((1, 1, w), lambda i, j: (i, 0, j))],
        out_specs=pl.BlockSpec((1, w, qw), lambda i, j: (i, j, 0)),
        out_shape=jax.ShapeDtypeStruct((b, s, qw), BF16),
        compiler_params=_params(("parallel", "parallel"), 24 << 20),
        name="swa_attention",
    )(sinks, proj3, proj3, proj3, proj3, proj3, pos_col, pos_row, pos_row)


def _ssd_kernel(xs_ref, bc_ref, z_ref, dt_ref, cwx_ref, cwb_ref, cbx_ref, cbb_ref, dtb_ref,
                alog_ref, dsk_ref, nw_ref, exp_ref, o_ref, padx_scr, padb_scr, st_scr):
    ch = SSM_CHUNK
    c = pl.program_id(1)

    @pl.when(c == 0)
    def _():
        padx_scr[0:8, :] = jnp.zeros((8, padx_scr.shape[1]), F32)
        padb_scr[0:8, :] = jnp.zeros((8, padb_scr.shape[1]), F32)
        st_scr[...] = jnp.zeros(st_scr.shape, F32)

    padx_scr[8:8 + ch, :] = xs_ref[0].astype(F32)
    padb_scr[8:8 + ch, :] = bc_ref[0].astype(F32)

    def conv(pad_scr, w_ref, b_ref):
        acc = b_ref[...] + w_ref[0:1, :] * pad_scr[5:5 + ch, :]
        for k in range(1, SSM_CONV):
            acc = acc + w_ref[k:k + 1, :] * pad_scr[5 + k:5 + k + ch, :]
        return _silu(acc)

    xs = conv(padx_scr, cwx_ref, cbx_ref)
    bcv = conv(padb_scr, cwb_ref, cbb_ref)
    padx_scr[0:8, :] = padx_scr[ch:ch + 8, :]
    padb_scr[0:8, :] = padb_scr[ch:ch + 8, :]

    gn = SSM_GROUPS * SSM_STATE
    dtr = dt_ref[0] + dtb_ref[...]
    dt = jnp.maximum(dtr, 0.0) + jnp.log(1.0 + jnp.exp(-jnp.abs(dtr)))
    a = dt * (-jnp.exp(alog_ref[...]))
    expand = exp_ref[...]
    row = lax.broadcasted_iota(jnp.int32, (ch, ch), 0)
    colm = lax.broadcasted_iota(jnp.int32, (ch, ch), 1)
    tril = row >= colm
    tri = jnp.where(tril, 1.0, 0.0)
    dt_e = jnp.dot(dt, expand, precision=HIGHEST, preferred_element_type=F32)
    acs = jnp.dot(tri, a, precision=HIGHEST, preferred_element_type=F32)
    acs_e = jnp.dot(acs, expand, precision=HIGHEST, preferred_element_type=F32)
    acs_t = acs.T
    x_dt = xs * dt_e
    last = acs_e[ch - 1:ch, :]
    x_dec = (x_dt * jnp.exp(last - acs_e)).astype(BF16)
    lane = lax.broadcasted_iota(jnp.int32, (ch, LANE), 1)
    lo = lane < SSM_HEAD_DIM
    half_w = SSM_D_INNER // SSM_GROUPS
    heads_per_group = SSM_HEADS // SSM_GROUPS
    groups = range(SSM_GROUPS)
    heads = range(SSM_HEADS)
    bg = [bcv[:, g * SSM_STATE:(g + 1) * SSM_STATE] for g in groups]
    cg = [bcv[:, gn + g * SSM_STATE:gn + (g + 1) * SSM_STATE].astype(BF16) for g in groups]
    gmat = [lax.dot_general(cg[g], bg[g].astype(BF16), (((1,), (1,)), ((), ())),
                            preferred_element_type=F32) for g in groups]
    st_in = [st_scr[:, g * half_w:(g + 1) * half_w] for g in groups]
    yoff = [jnp.dot(cg[g], st_in[g].astype(BF16), preferred_element_type=F32) for g in groups]
    upd = [jnp.dot(bg[g].T.astype(BF16), x_dec[:, g * half_w:(g + 1) * half_w],
                   preferred_element_type=F32) for g in groups]
    seg = [jnp.exp(jnp.where(tril, acs[:, h:h + 1] - acs_t[h:h + 1, :], -jnp.inf)) for h in heads]
    mmat = [(gmat[h // heads_per_group] * seg[h]).astype(BF16) for h in heads]
    xh = []
    for h in heads:
        xp = x_dt[:, (h // 2) * LANE:(h // 2 + 1) * LANE]
        xh.append((jnp.where(lo, xp, 0.0) if h % 2 == 0 else jnp.where(lo, 0.0, xp)).astype(BF16))
    ydiag = [jnp.dot(mmat[2 * pr], xh[2 * pr], preferred_element_type=F32)
             + jnp.dot(mmat[2 * pr + 1], xh[2 * pr + 1], preferred_element_type=F32)
             for pr in range(SSM_HEADS // 2)]
    for g in groups:
        st_scr[:, g * half_w:(g + 1) * half_w] = (
            st_in[g] * jnp.exp(last[:, g * half_w:(g + 1) * half_w]) + upd[g])
    y = (jnp.concatenate(ydiag, axis=1) + jnp.concatenate(yoff, axis=1) * jnp.exp(acs_e)
         + dsk_ref[...] * xs)
    gz = y * _silu(z_ref[0].astype(F32))
    outs = []
    for g in range(SSM_GROUPS):
        gg = gz[:, g * half_w:(g + 1) * half_w]
        outs.append(gg * lax.rsqrt(jnp.mean(gg * gg, axis=-1, keepdims=True) + NORM_EPS))
    o_ref[0] = (jnp.concatenate(outs, axis=1) * nw_ref[...]).astype(o_ref.dtype)


def _ssd(proj3, tail3, off, conv_w, conv_b, dt_bias, a_log, d_skip, norm_w):
    b, s, _ = proj3.shape
    ch = SSM_CHUNK
    di = SSM_D_INNER
    pad16 = lambda v: jnp.zeros((1, LANE), F32).at[0, :SSM_HEADS].set(v)
    expand = np.zeros((LANE, di), np.float32)
    for h in range(SSM_HEADS):
        expand[h, h * SSM_HEAD_DIM:(h + 1) * SSM_HEAD_DIM] = 1.0
    full = lambda shape: pl.BlockSpec(shape, lambda i, j: (0,) * len(shape))
    return pl.pallas_call(
        _ssd_kernel,
        grid=(b, s // ch),
        in_specs=[pl.BlockSpec((1, ch, di), lambda i, j: (i, j, off["xs"] // di)),
                  pl.BlockSpec((1, ch, SSM_BC), lambda i, j: (i, j, off["bc"] // SSM_BC)),
                  pl.BlockSpec((1, ch, di), lambda i, j: (i, j, off["z"] // di)),
                  pl.BlockSpec((1, ch, LANE), lambda i, j: (i, j, 2)),
                  full((SSM_CONV, di)), full((SSM_CONV, SSM_BC)), full((1, di)), full((1, SSM_BC)),
                  full((1, LANE)), full((1, LANE)), full((1, di)), full((1, di)), full((LANE, di))],
        out_specs=pl.BlockSpec((1, ch, di), lambda i, j: (i, j, 0)),
        out_shape=jax.ShapeDtypeStruct((b, s, di), BF16),
        scratch_shapes=[pltpu.VMEM((ch + 8, di), F32), pltpu.VMEM((ch + 8, SSM_BC), F32),
                        pltpu.VMEM((SSM_STATE, di), F32)],
        compiler_params=_params(("parallel", "arbitrary"), 32 << 20),
        name="ssd_mixer",
    )(proj3, proj3, proj3, tail3, conv_w[:, :di], conv_w[:, di:], conv_b[:di].reshape(1, di),
      conv_b[di:].reshape(1, SSM_BC), pad16(dt_bias), pad16(a_log),
      jnp.repeat(d_skip, SSM_HEAD_DIM).reshape(1, di), norm_w.reshape(1, di), jnp.asarray(expand))


def _merge_kernel(o0_ref, o1_ref, o2_ref, o3_ref, w_ref, g0_ref, g1_ref, g2_ref, g3_ref, out_ref, wbf_scr):
    @pl.when(pl.program_id(1) == 0)
    def _():
        for r in range(N_BRANCH):
            wbf_scr[r] = w_ref[0, r].astype(BF16)

    acc = None
    for r, (o_ref, g_ref) in enumerate(((o0_ref, g0_ref), (o1_ref, g1_ref), (o2_ref, g2_ref),
                                        (o3_ref, g3_ref))):
        y = jnp.dot(o_ref[...], wbf_scr[r], preferred_element_type=F32)
        gate = 1.0 / (1.0 + jnp.exp(-g_ref[...].astype(F32)))
        acc = gate * y if acc is None else acc + gate * y
    out_ref[...] = acc.astype(out_ref.dtype)


def _merge(branches, w_branch, layer, proj, d, tm=1024, tn=512):
    t = proj.shape[0]
    nj = d // tn
    o_spec = pl.BlockSpec((tm, BRANCH_W), lambda j, i: (i, 0))
    g_specs = [pl.BlockSpec((tm, tn), functools.partial(lambda j, i, r: (i, r * nj + j), r=r))
               for r in range(N_BRANCH)]
    est = (2 * (4 * tm * BRANCH_W * 2 + 4 * BRANCH_W * tn * 4 + 4 * tm * tn * 2 + tm * tn * 2)
           + 4 * BRANCH_W * tn * 2 + 3 * tm * tn * 4 + (6 << 20))
    return pl.pallas_call(
        _merge_kernel,
        grid=(nj, t // tm),
        in_specs=[o_spec] * 4 + [pl.BlockSpec((1, N_BRANCH, BRANCH_W, tn), lambda j, i: (layer, 0, 0, j))]
        + g_specs,
        out_specs=pl.BlockSpec((tm, tn), lambda j, i: (i, j)),
        out_shape=jax.ShapeDtypeStruct((t, d), BF16),
        scratch_shapes=[pltpu.VMEM((N_BRANCH, BRANCH_W, tn), BF16)],
        compiler_params=_params(("parallel", "arbitrary"), est),
        name="branch_merge",
    )(*branches, w_branch, proj, proj, proj, proj)


def _out_proj_kernel(a_ref, w_ref, x_ref, mod_ref, o_ref, wbf_scr, *, g_idx):
    @pl.when(pl.program_id(1) == 0)
    def _():
        rows = 512
        for r in range(0, wbf_scr.shape[0], rows):
            wbf_scr[r:r + rows, :] = w_ref[0, r:r + rows, :].astype(BF16)

    y = jnp.dot(a_ref[...], wbf_scr[...], preferred_element_type=F32)
    o_ref[...] = x_ref[...] + mod_ref[0, g_idx:g_idx + 1, :] * y


def _out_proj(merged, w_out, layer, x2, mod, seq, g_idx, tm=1024, tn=512):
    t, d = x2.shape
    per_b = seq // tm
    est = 2 * (tm * d * 2 + d * tn * 4 + 2 * tm * tn * 4) + d * tn * 2 + tm * tn * 4 + (4 << 20)
    return pl.pallas_call(
        functools.partial(_out_proj_kernel, g_idx=g_idx),
        grid=(d // tn, t // tm),
        in_specs=[pl.BlockSpec((tm, d), lambda j, i: (i, 0)),
                  pl.BlockSpec((1, d, tn), lambda j, i: (layer, 0, j)),
                  pl.BlockSpec((tm, tn), lambda j, i: (i, j)),
                  pl.BlockSpec((1, 6, tn), lambda j, i: (i // per_b, 0, j))],
        out_specs=pl.BlockSpec((tm, tn), lambda j, i: (i, j)),
        out_shape=jax.ShapeDtypeStruct((t, d), F32),
        scratch_shapes=[pltpu.VMEM((d, tn), BF16)],
        compiler_params=_params(("parallel", "arbitrary"), est),
        name="out_proj",
    )(merged, w_out, x2, mod)


def _router_kernel(x_ref, w_ref, mod_ref, rw_ref, rb_ref, h_ref, ri_ref, rwt_ref, cnt_ref, carry_scr,
                   *, sh_idx, sc_idx):
    step = pl.program_id(0) * pl.num_programs(1) + pl.program_id(1)

    @pl.when(step == 0)
    def _():
        carry_scr[...] = jnp.zeros(carry_scr.shape, F32)

    x = x_ref[0]
    y = x * lax.rsqrt(jnp.mean(x * x, axis=-1, keepdims=True) + NORM_EPS) * w_ref[...]
    m = mod_ref[0]
    h = y * (1.0 + m[sc_idx:sc_idx + 1]) + m[sh_idx:sh_idx + 1]
    for g in range(h.shape[1] // (2 * LANE)):
        h_ref[0, :, g * LANE:(g + 1) * LANE] = _pack_bf16_pair(h[:, 2 * g * LANE:(2 * g + 1) * LANE],
                                                               h[:, (2 * g + 1) * LANE:(2 * g + 2) * LANE])
    logits = lax.dot_general(h, rw_ref[...], (((1,), (1,)), ((), ())), precision=HIGHEST,
                             preferred_element_type=F32) + rb_ref[...]
    tm = logits.shape[0]
    lane = lax.broadcasted_iota(jnp.int32, logits.shape, 1)
    big = jnp.int32(4 * LANE)
    neg = -jnp.inf

    def first_argmax(vals):
        mx = jnp.max(vals, axis=1, keepdims=True)
        idx = jnp.min(jnp.where(vals == mx, lane, big), axis=1, keepdims=True)
        return mx, idx

    lg = jnp.where(lane < N_GROUPS, logits, neg)
    gmax, gidx = first_argmax(lg)
    g_w = 1.0 / jnp.sum(jnp.exp(lg - gmax), axis=1, keepdims=True)
    lo = N_GROUPS + gidx * EXPERTS_PER_GROUP
    in_group = jnp.where(lane >= lo, jnp.where(lane < lo + EXPERTS_PER_GROUP, 1.0, 0.0), 0.0) > 0.0
    le = jnp.where(in_group, logits, neg)
    m1, i1 = first_argmax(le)
    le2 = jnp.where(lane == i1, neg, le)
    m2, i2 = first_argmax(le2)
    denom = jnp.sum(jnp.exp(le - m1), axis=1, keepdims=True)
    p1 = 1.0 / denom
    p2 = jnp.exp(m2 - m1) / denom
    w1 = p1 / (p1 + p2) * g_w
    w2 = p2 / (p1 + p2) * g_w
    e1 = i1 - N_GROUPS
    e2 = i2 - N_GROUPS

    oh1 = jnp.where(lane == e1, 1.0, 0.0)
    oh2 = jnp.where(lane == e2, 1.0, 0.0)
    both = oh1 + oh2
    r_i = lax.broadcasted_iota(jnp.int32, (tm, tm), 0)
    c_i = lax.broadcasted_iota(jnp.int32, (tm, tm), 1)
    strict = jnp.where(r_i > c_i, 1.0, 0.0).astype(BF16)
    before = jnp.dot(strict, both.astype(BF16), preferred_element_type=F32) + carry_scr[...]
    rank1 = jnp.sum(before * oh1, axis=1, keepdims=True).astype(jnp.int32)
    rank2 = jnp.sum(before * oh2, axis=1, keepdims=True).astype(jnp.int32)
    carry_scr[...] = carry_scr[...] + jnp.sum(both, axis=0, keepdims=True)
    cnt_ref[...] = carry_scr[...]
    ri_ref[0] = jnp.where(lane == 0, e1, jnp.where(lane == 1, e2, jnp.where(
        lane == 2, rank1, jnp.where(lane == 3, rank2, 0))))
    rwt_ref[0] = jnp.where(lane == 0, w1, jnp.where(lane == 1, w2, 0.0))


def _router(x, w, mod, wg, bg, wr, br, sh_idx, sc_idx, tm=256):
    b, s, d = x.shape
    rw = jnp.concatenate([wg.T, wr.T, jnp.zeros((LANE - N_GROUPS - N_EXPERTS, d), F32)], axis=0)
    rb = jnp.zeros((1, LANE), F32).at[0, :N_GROUPS].set(bg).at[0, N_GROUPS:N_GROUPS + N_EXPERTS].set(br)
    return pl.pallas_call(
        functools.partial(_router_kernel, sh_idx=sh_idx, sc_idx=sc_idx),
        grid=(b, s // tm),
        in_specs=[pl.BlockSpec((1, tm, d), lambda i, j: (i, j, 0)),
                  pl.BlockSpec((1, d), lambda i, j: (0, 0)),
                  pl.BlockSpec((1, 6, d), lambda i, j: (i, 0, 0)),
                  pl.BlockSpec((LANE, d), lambda i, j: (0, 0)),
                  pl.BlockSpec((1, LANE), lambda i, j: (0, 0))],
        out_specs=[pl.BlockSpec((1, tm, d // 2), lambda i, j: (i, j, 0)),
                   pl.BlockSpec((1, tm, LANE), lambda i, j: (i, j, 0)),
                   pl.BlockSpec((1, tm, LANE), lambda i, j: (i, j, 0)),
                   pl.BlockSpec((1, LANE), lambda i, j: (0, 0))],
        out_shape=[jax.ShapeDtypeStruct((b, s, d // 2), jnp.uint32),
                   jax.ShapeDtypeStruct((b, s, LANE), jnp.int32),
                   jax.ShapeDtypeStruct((b, s, LANE), F32),
                   jax.ShapeDtypeStruct((1, LANE), F32)],
        scratch_shapes=[pltpu.VMEM((1, LANE), F32)],
        compiler_params=_params(("arbitrary", "arbitrary"), 8 * tm * d * 4 + d * LANE * 8 + (8 << 20)),
        name="moe_router",
    )(x, w.reshape(1, d), mod, rw, rb)


def _slot_kernel(d1_ref, d2_ref, o_ref):
    def clear(i, carry):
        o_ref[i] = 0
        return carry

    lax.fori_loop(0, o_ref.shape[0], clear, 0, unroll=16)

    def place(t, carry):
        o_ref[d1_ref[t]] = t
        o_ref[d2_ref[t]] = t
        return carry

    lax.fori_loop(0, d1_ref.shape[0], place, 0, unroll=8)


def _slot_tokens(dest1, dest2, cap):
    smem = pl.BlockSpec(memory_space=pltpu.SMEM)
    return pl.pallas_call(
        _slot_kernel,
        in_specs=[smem, smem],
        out_specs=smem,
        out_shape=jax.ShapeDtypeStruct((cap,), jnp.int32),
        name="moe_slot_tokens",
    )(dest1, dest2)


def _expert_kernel(tok_ref, be_ref, nu_ref, first_ref, nxt_ref, h_hbm, wg_hbm, wu_hbm, wd_hbm, o_ref,
                   x_scr, xb_scr, sg_scr, su_scr, sd_scr, wg_scr, wu_scr, wd_scr, sem_x, sem_w, *, layer):
    blk = pl.program_id(0)
    n_used = nu_ref[0]
    rows = x_scr.shape[1]
    slot = lax.rem(blk, 2)

    def row_copy(b, s, i):
        tok = tok_ref[b * rows + i]
        return pltpu.make_async_copy(h_hbm.at[pl.ds(tok, 1)], x_scr.at[s, pl.ds(i, 1)], sem_x.at[s])

    def start_rows(b, s):
        def go(i, carry):
            row_copy(b, s, i).start()
            return carry

        lax.fori_loop(0, rows, go, 0, unroll=8)

    def weight_copies(e):
        return (pltpu.make_async_copy(wg_hbm.at[layer, e], sg_scr, sem_w.at[0]),
                pltpu.make_async_copy(wu_hbm.at[layer, e], su_scr, sem_w.at[1]),
                pltpu.make_async_copy(wd_hbm.at[layer, e], sd_scr, sem_w.at[2]))

    @pl.when(blk < n_used)
    def _():
        @pl.when(blk == 0)
        def _():
            start_rows(0, 0)
            for cp in weight_copies(be_ref[0]):
                cp.start()

        @pl.when(first_ref[blk] == 1)
        def _():
            for cp in weight_copies(be_ref[blk]):
                cp.wait()
            step = 512
            for r in range(0, sg_scr.shape[0], step):
                wg_scr[r:r + step, :] = sg_scr[r:r + step, :].astype(BF16)
                wu_scr[r:r + step, :] = su_scr[r:r + step, :].astype(BF16)
            for r in range(0, sd_scr.shape[1], step * 4):
                wd_scr[:, r:r + step * 4] = sd_scr[:, r:r + step * 4].astype(BF16)

            @pl.when(nxt_ref[blk] >= 0)
            def _():
                for cp in weight_copies(nxt_ref[blk]):
                    cp.start()

        def wait(i, carry):
            row_copy(blk, slot, i).wait()
            return carry

        lax.fori_loop(0, rows, wait, 0, unroll=8)
        nxt_blk = jnp.minimum(blk + 1, n_used - 1)
        for i in range(rows):
            row_copy(nxt_blk, 1 - slot, i).start()
        groups = xb_scr.shape[1] // (2 * LANE)
        for g in range(groups):
            lo, hi = _unpack_bf16_pair(x_scr[slot, :, g * LANE:(g + 1) * LANE])
            xb_scr[:, 2 * g * LANE:(2 * g + 1) * LANE] = lo.astype(BF16)
            xb_scr[:, (2 * g + 1) * LANE:(2 * g + 2) * LANE] = hi.astype(BF16)
        xb = xb_scr[...]
        gate = jnp.dot(xb, wg_scr[...], preferred_element_type=F32)
        up = jnp.dot(xb, wu_scr[...], preferred_element_type=F32)
        act = (_silu(gate) * up).astype(BF16)
        for g in range(groups):
            y = jnp.dot(act, wd_scr[:, 2 * g * LANE:(2 * g + 2) * LANE], preferred_element_type=F32)
            o_ref[:, g * LANE:(g + 1) * LANE] = _pack_bf16_pair(y[:, :LANE], y[:, LANE:])

        @pl.when(blk == n_used - 1)
        def _():
            def drain(i, carry):
                row_copy(blk, 1 - slot, i).wait()
                return carry

            lax.fori_loop(0, rows, drain, 0, unroll=8)

    @pl.when(blk >= n_used)
    def _():
        o_ref[...] = jnp.zeros(o_ref.shape, o_ref.dtype)


def _experts(slot_tok, block_expert, n_used, first, nxt, h2, wg_stack, wu_stack, wd_stack, layer):
    t, dw = h2.shape
    d = 2 * dw
    n_blocks = block_expert.shape[0]
    hid = wg_stack.shape[3]
    est = 3 * d * hid * 6 + EXP_BLOCK * d * (2 * 2 + 2 + 2 * 2 + 8) + (6 << 20)
    any_spec = pl.BlockSpec(memory_space=pl.ANY)
    return pl.pallas_call(
        functools.partial(_expert_kernel, layer=layer),
        grid_spec=pltpu.PrefetchScalarGridSpec(
            num_scalar_prefetch=5,
            grid=(n_blocks,),
            in_specs=[any_spec, any_spec, any_spec, any_spec],
            out_specs=pl.BlockSpec((EXP_BLOCK, dw), lambda i, *_: (i, 0)),
            scratch_shapes=[pltpu.VMEM((2, EXP_BLOCK, dw), jnp.uint32), pltpu.VMEM((EXP_BLOCK, d), BF16),
                            pltpu.VMEM((d, hid), F32), pltpu.VMEM((d, hid), F32), pltpu.VMEM((hid, d), F32),
                            pltpu.VMEM((d, hid), BF16), pltpu.VMEM((d, hid), BF16), pltpu.VMEM((hid, d), BF16),
                            pltpu.SemaphoreType.DMA((2,)), pltpu.SemaphoreType.DMA((3,))]),
        out_shape=jax.ShapeDtypeStruct((n_blocks * EXP_BLOCK, dw), jnp.uint32),
        compiler_params=_params(("arbitrary",), est),
        name="moe_experts",
    )(slot_tok, block_expert, n_used, first, nxt, h2, wg_stack, wu_stack, wd_stack)


def _combine_kernel(d1_ref, d2_ref, ys_hbm, x_ref, rw_ref, mod_ref, fw_ref, o_ref, a_scr, b_scr, sem,
                    *, g_idx, final):
    i = pl.program_id(0)
    rows = a_scr.shape[1]
    slot = lax.rem(i, 2)

    def copies(step, s, r):
        t = step * rows + r
        return (pltpu.make_async_copy(ys_hbm.at[pl.ds(d1_ref[t], 1)], a_scr.at[s, pl.ds(r, 1)], sem.at[0, s]),
                pltpu.make_async_copy(ys_hbm.at[pl.ds(d2_ref[t], 1)], b_scr.at[s, pl.ds(r, 1)], sem.at[1, s]))

    def start_rows(step, s):
        def go(r, carry):
            ca, cb = copies(step, s, r)
            ca.start()
            cb.start()
            return carry

        lax.fori_loop(0, rows, go, 0, unroll=8)

    @pl.when(i == 0)
    def _():
        start_rows(0, 0)

    def wait_rows(s):
        def wait(r, carry):
            ca, cb = copies(i, s, r)
            ca.wait()
            cb.wait()
            return carry

        lax.fori_loop(0, rows, wait, 0, unroll=8)

    wait_rows(slot)
    last = pl.num_programs(0) - 1
    nxt = jnp.minimum(i + 1, last)
    for r in range(rows):
        ca, cb = copies(nxt, 1 - slot, r)
        ca.start()
        cb.start()
    rw = rw_ref[...]
    w1 = rw[:, 0:1]
    w2 = rw[:, 1:2]
    d = x_ref.shape[1]
    ssq = jnp.zeros((rows, 1), F32)
    for g in range(d // (2 * LANE)):
        halves_a = _unpack_bf16_pair(a_scr[slot, :, g * LANE:(g + 1) * LANE])
        halves_b = _unpack_bf16_pair(b_scr[slot, :, g * LANE:(g + 1) * LANE])
        for half in range(2):
            c0 = (2 * g + half) * LANE
            moe = w1 * halves_a[half] + w2 * halves_b[half]
            xn = x_ref[:, c0:c0 + LANE] + mod_ref[0, g_idx:g_idx + 1, c0:c0 + LANE] * moe
            o_ref[:, c0:c0 + LANE] = xn
            if final:
                ssq = ssq + jnp.sum(xn * xn, axis=-1, keepdims=True)
    if final:
        o_ref[...] = o_ref[...] * lax.rsqrt(ssq * (1.0 / d) + NORM_EPS) * fw_ref[...]

    @pl.when(i == last)
    def _():
        wait_rows(1 - slot)


def _combine(dest1, dest2, ys, x2, route_w, mod, final_w, seq, g_idx, final, tm=128):
    t, d = x2.shape
    per_b = seq // tm
    est = 2 * (2 * tm * d * 4 + tm * LANE * 4) + 4 * tm * d * 4 + (6 << 20)
    return pl.pallas_call(
        functools.partial(_combine_kernel, g_idx=g_idx, final=final),
        grid_spec=pltpu.PrefetchScalarGridSpec(
            num_scalar_prefetch=2,
            grid=(t // tm,),
            in_specs=[pl.BlockSpec(memory_space=pl.ANY),
                      pl.BlockSpec((tm, d), lambda i, d1, d2: (i, 0)),
                      pl.BlockSpec((tm, LANE), lambda i, d1, d2: (i, 0)),
                      pl.BlockSpec((1, 6, d), lambda i, d1, d2: (i // per_b, 0, 0)),
                      pl.BlockSpec((1, d), lambda i, d1, d2: (0, 0))],
            out_specs=pl.BlockSpec((tm, d), lambda i, d1, d2: (i, 0)),
            scratch_shapes=[pltpu.VMEM((2, tm, d // 2), jnp.uint32), pltpu.VMEM((2, tm, d // 2), jnp.uint32),
                            pltpu.SemaphoreType.DMA((2, 2))]),
        out_shape=jax.ShapeDtypeStruct((t, d), F32),
        compiler_params=_params(("arbitrary",), est),
        name="moe_combine",
    )(dest1, dest2, ys, x2, route_w, mod, final_w.reshape(1, d))


def _small_in_weights(wt_stack, layer):
    d = wt_stack.shape[2]
    bounds = np.cumsum((0,) + IN_SIZES_HEAD)
    seg = [wt_stack[layer, bounds[i]:bounds[i + 1], :] for i in range(3, len(IN_SIZES_HEAD))]
    dt, q_lat, kv_lat, k_rope, sq, sk, sv = seg
    z64 = jnp.zeros((SWA_HEAD_DIM, d), wt_stack.dtype)

    def spread(w):
        h0, h1 = w[:SWA_HEAD_DIM], w[SWA_HEAD_DIM:]
        return jnp.concatenate([h0, z64, z64, h0, h1, z64, z64, h1], axis=0)

    half = MLA_ROPE // 2
    rot = jnp.concatenate([-k_rope[half:], k_rope[:half]], axis=0)
    small = jnp.concatenate([sq, spread(sk), spread(sv), kv_lat,
                             jnp.zeros((OFF_S["qlat"] - OFF_S["kvlat"] - MLA_KV_LORA, d), wt_stack.dtype),
                             q_lat], axis=0)
    tail = jnp.concatenate([k_rope, z64, rot, z64, dt,
                            jnp.zeros((LANE - SSM_HEADS, d), wt_stack.dtype)], axis=0)
    return small[None], tail[None]


def _extend_wq(wq_b):
    k = wq_b.shape[0]
    w = wq_b.reshape(k, MLA_HEADS, MLA_NOPE + MLA_ROPE)
    z = jnp.zeros((k, MLA_HEADS, LANE - MLA_ROPE), wq_b.dtype)
    rope = w[..., MLA_NOPE:]
    half = MLA_ROPE // 2
    rot = jnp.concatenate([-rope[..., half:], rope[..., :half]], axis=-1)
    return jnp.concatenate([w[..., :MLA_NOPE], rope, z, rot, z], axis=-1).reshape(k, -1).astype(BF16)


def _mixer(x, mod, pos_col, pos_row, cos_t, sin_t, norm_w, w_in_stack, layer, conv_w, conv_b, dt_bias, a_log,
           d_skip, ssm_norm, mla_q_norm, mla_wq_b, mla_kv_norm, mla_wkv_b, swa_sinks, w_branch, w_out):
    b, s, d = x.shape
    t = b * s
    swa_slopes, moba_slopes = _alibi_slopes()
    h = _norm_mod(x, norm_w, mod, sh_idx=0, sc_idx=1)
    h2 = h.reshape(t, d)
    w_small, w_tail = _small_in_weights(w_in_stack, layer)
    proj_a = _wproj(h2, w_in_stack, layer, 0, A_COLS, "in_proj_head")
    proj_g = _wproj(h2, w_in_stack, layer, GATE_COL0, N_BRANCH * d, "in_proj_gates")
    proj_s = _wproj(h2, w_small, 0, 0, S_COLS, "in_proj_small", tn=768)
    tail = _wproj(h2, w_tail, 0, 0, 3 * LANE, "in_proj_tail", tn=3 * LANE, out_dtype=F32)
    proj_a3 = proj_a.reshape(b, s, A_COLS)
    proj_s3 = proj_s.reshape(b, s, S_COLS)
    tail3 = tail.reshape(b, s, 3 * LANE)

    o_moba = _flash(proj_a3, proj_a3, proj_a3, heads=MOBA_HEADS, dq=MOBA_HEAD_DIM, dv=MOBA_HEAD_DIM,
                    q_off=OFF_A["mq"], k_off=OFF_A["mk"], v_off=OFF_A["mv"],
                    moba=True, q_scale=float(MOBA_HEAD_DIM ** -0.5) * LOG2E,
                    slopes=jnp.asarray(moba_slopes * np.float32(LOG2E)),
                    pos_col=pos_col, pos_row=pos_row, name="moba_attention")
    o_ssm = _ssd(proj_a3, tail3, OFF_A, conv_w, conv_b, dt_bias, a_log, d_skip, ssm_norm)
    q_m, k_m, v_m = _mla_project(proj_s, tail, OFF_S, mla_q_norm, _extend_wq(mla_wq_b), mla_kv_norm,
                                 mla_wkv_b.astype(BF16), cos_t, sin_t)
    o_mla = _flash(q_m.reshape(b, s, -1), k_m.reshape(b, s, -1), v_m.reshape(b, s, -1),
                   heads=MLA_HEADS, dq=MLA_QK, dv=MLA_V, q_off=0, k_off=0, v_off=0, name="mla_attention")
    o_swa = _swa(proj_s3, OFF_S, swa_sinks, pos_col, pos_row, swa_slopes)

    branches = [o.reshape(t, BRANCH_W) for o in (o_moba, o_ssm, o_mla, o_swa)]
    merged = _merge(branches, w_branch, layer, proj_g, d)
    return _out_proj(merged, w_out, layer, x.reshape(t, d), mod, s, g_idx=2)


def _moe(x2, seq, mod, norm_w, wg, bg, wr, br, w_gate, w_up, w_down, layer, final_w, final):
    t, d = x2.shape
    b = t // seq
    h, route_i, route_w, counts = _router(x2.reshape(b, seq, d), norm_w, mod, wg, bg, wr, br,
                                          sh_idx=3, sc_idx=4)
    route_i = route_i.reshape(t, LANE)
    counts = counts[0, :N_EXPERTS].astype(jnp.int32)
    padded = (counts + EXP_BLOCK - 1) // EXP_BLOCK * EXP_BLOCK
    pad_end = jnp.cumsum(padded)
    pad_start = pad_end - padded
    n_blocks = -(-(t * 2) // EXP_BLOCK) + N_EXPERTS
    dest1 = pad_start[route_i[:, 0]] + route_i[:, 2]
    dest2 = pad_start[route_i[:, 1]] + route_i[:, 3]
    block_row0 = jnp.arange(n_blocks, dtype=jnp.int32) * EXP_BLOCK
    block_expert = jnp.minimum(jnp.sum((pad_end[None, :] <= block_row0[:, None]).astype(jnp.int32), axis=1),
                               N_EXPERTS - 1)
    n_used = (pad_end[-1:] // EXP_BLOCK).astype(jnp.int32)
    first = (block_expert != jnp.concatenate([jnp.full((1,), -1, jnp.int32), block_expert[:-1]])).astype(jnp.int32)
    e_ids = jnp.arange(N_EXPERTS, dtype=jnp.int32)
    later = jnp.where((counts > 0)[None, :] & (e_ids[None, :] > e_ids[:, None]), e_ids[None, :], N_EXPERTS)
    nxt_e = jnp.min(later, axis=1)
    nxt = jnp.where(nxt_e == N_EXPERTS, -1, nxt_e)[block_expert].astype(jnp.int32)
    slot_tok = _slot_tokens(dest1, dest2, n_blocks * EXP_BLOCK)
    ys = _experts(slot_tok, block_expert, n_used, first, nxt, h.reshape(t, d // 2), w_gate, w_up, w_down, layer)
    return _combine(dest1, dest2, ys, x2, route_w.reshape(t, LANE), mod, final_w, seq, g_idx=5, final=final)


def kernel(x, c, positions, ada_w, ada_b, norm_mix, norm_ffn, w_in, conv_w, conv_b, dt_bias, a_log, d_skip,
           ssm_norm, mla_q_norm, mla_wq_b, mla_kv_norm, mla_wkv_b, swa_sinks, w_branch, w_out,
           router_group_w, router_group_b, router_w, router_b, exp_w_gate, exp_w_up, exp_w_down, final_norm):
    b, s, d = x.shape
    depth = ada_w.shape[0]
    mods = _ada_mod(c, ada_w, ada_b)
    pos_f = positions.astype(F32)
    pos_col = pos_f.reshape(b, s, 1)
    pos_row = pos_f.reshape(b, 1, s)
    cos_t, sin_t = _rope_tables(pos_f.reshape(b * s, 1))
    w_in_t = jnp.swapaxes(w_in, 1, 2)
    for l in range(depth):
        x2 = _mixer(x, mods[l], pos_col, pos_row, cos_t, sin_t, norm_mix[l], w_in_t, l, conv_w[l], conv_b[l],
                    dt_bias[l], a_log[l], d_skip[l], ssm_norm[l], mla_q_norm[l], mla_wq_b[l],
                    mla_kv_norm[l], mla_wkv_b[l], swa_sinks[l], w_branch, w_out)
        x2 = _moe(x2, s, mods[l], norm_ffn[l], router_group_w[l], router_group_b[l], router_w[l],
                  router_b[l], exp_w_gate, exp_w_up, exp_w_down, l, final_norm,
                  final=(l == depth - 1))
        x = x2.reshape(b, s, d)
    return x
```

```python
import functools
import math

import numpy as np
import jax
import jax.numpy as jnp
from jax import lax
from jax.experimental import pallas as pl
from jax.experimental.pallas import tpu as pltpu

F32 = jnp.float32
BF16 = jnp.bfloat16
HIGHEST = lax.Precision.HIGHEST

MOBA_HEADS = 8
MOBA_HEAD_DIM = 128
MOBA_BLOCK = 256
MOBA_TOPK = 3
SSM_D_INNER = 1024
SSM_HEAD_DIM = 64
SSM_HEADS = SSM_D_INNER // SSM_HEAD_DIM
SSM_GROUPS = 2
SSM_STATE = 128
SSM_CONV = 4
SSM_CHUNK = 128
SSM_BC = 2 * SSM_GROUPS * SSM_STATE
MLA_HEADS = 8
MLA_Q_LORA = 768
MLA_KV_LORA = 512
MLA_NOPE = 128
MLA_ROPE = 64
MLA_V = 128
ROPE_THETA = 10000.0
SWA_HEADS = 16
SWA_KV_HEADS = 2
SWA_HEAD_DIM = 64
SWA_WINDOW = 128
N_BRANCH = 4
BRANCH_W = 1024
N_GROUPS = 4
EXPERTS_PER_GROUP = 8
N_EXPERTS = N_GROUPS * EXPERTS_PER_GROUP
EXPERT_HIDDEN = 512
EXP_BLOCK = 256
NORM_EPS = 1e-6
N_ALIBI = MOBA_HEADS + SWA_HEADS

LOG2E = math.log2(math.e)
MASKED = -1e30
LANE = 128
VMEM_CAP = 60 * 1024 * 1024
ATTN_TILE = 512
FLASH_HEADS = 4
ONES_ROWS = 16
MLA_QK = 256


def _alibi_slopes():
    i = np.arange(1, N_ALIBI + 1, dtype=np.float64)
    s = np.exp2(-8.0 * i / N_ALIBI).astype(np.float32)
    return s[:SWA_HEADS], s[SWA_HEADS:]


IN_SIZES_HEAD = (3 * MOBA_HEADS * MOBA_HEAD_DIM, SSM_D_INNER, SSM_D_INNER + SSM_BC, SSM_HEADS, MLA_Q_LORA,
                 MLA_KV_LORA, MLA_ROPE, SWA_HEADS * SWA_HEAD_DIM, SWA_KV_HEADS * SWA_HEAD_DIM,
                 SWA_KV_HEADS * SWA_HEAD_DIM)
A_COLS = IN_SIZES_HEAD[0] + IN_SIZES_HEAD[1] + IN_SIZES_HEAD[2]
GATE_COL0 = sum(IN_SIZES_HEAD)
OFF_A = dict(mq=0, mk=1024, mv=2048, z=3072, xs=4096, bc=5120)
OFF_S = dict(sq=0, skx=1024, svx=1536, kvlat=2048, qlat=3072)
S_COLS = 3840


def _params(sem, est_bytes):
    limit = int(min(VMEM_CAP, max(est_bytes, 16 * 1024 * 1024)))
    return pltpu.CompilerParams(dimension_semantics=sem, vmem_limit_bytes=limit)


def _silu(v):
    return v * (1.0 / (1.0 + jnp.exp(-v)))


def _pack_bf16_pair(lo, hi):
    lo_bits = pltpu.bitcast(lo.astype(BF16).astype(F32), jnp.uint32)
    hi_bits = pltpu.bitcast(hi.astype(BF16).astype(F32), jnp.uint32)
    return hi_bits | lax.shift_right_logical(lo_bits, jnp.uint32(16))


def _split3(v):
    hi = v.astype(BF16)
    rest = v - hi.astype(F32)
    mid = rest.astype(BF16)
    lo = (rest - mid.astype(F32)).astype(BF16)
    return hi, mid, lo


def _unpack_bf16_pair(word):
    lo = pltpu.bitcast(lax.shift_left(word, jnp.uint32(16)), F32)
    hi = pltpu.bitcast(word & jnp.uint32(0xFFFF0000), F32)
    return lo, hi


def _ada_kernel(c_ref, w_ref, b_ref, o_ref):
    k = pl.program_id(1)

    @pl.when(k == 0)
    def _():
        o_ref[0] = jnp.broadcast_to(b_ref[0], o_ref.shape[1:])

    o_ref[0] += jnp.dot(c_ref[...].astype(BF16), w_ref[0].astype(BF16), preferred_element_type=F32)


def _ada_mod(c, ada_w, ada_b):
    depth, d, n = ada_w.shape
    b = c.shape[0]
    c8 = jnp.zeros((8, d), F32).at[:b].set(c)
    tk = LANE
    out = pl.pallas_call(
        _ada_kernel,
        grid=(depth, d // tk),
        in_specs=[pl.BlockSpec((8, tk), lambda l, k: (0, k)),
                  pl.BlockSpec((1, tk, n), lambda l, k: (l, k, 0)),
                  pl.BlockSpec((1, 1, n), lambda l, k: (l, 0, 0))],
        out_specs=pl.BlockSpec((1, 8, n), lambda l, k: (l, 0, 0)),
        out_shape=jax.ShapeDtypeStruct((depth, 8, n), F32),
        compiler_params=_params(("parallel", "arbitrary"), 3 * tk * n * 4 + 32 * n * 4 + (4 << 20)),
        name="ada_mod",
    )(c8, ada_w, ada_b.reshape(depth, 1, n))
    return out[:, :b].reshape(depth, b, 6, d)


def _norm_mod_kernel(x_ref, w_ref, mod_ref, o_ref, *, sh_idx, sc_idx):
    x = x_ref[0]
    y = x * lax.rsqrt(jnp.mean(x * x, axis=-1, keepdims=True) + NORM_EPS) * w_ref[...]
    m = mod_ref[0]
    o_ref[0] = (y * (1.0 + m[sc_idx:sc_idx + 1]) + m[sh_idx:sh_idx + 1]).astype(o_ref.dtype)


def _norm_mod(x, w, mod, sh_idx, sc_idx, tm=256):
    b, s, d = x.shape
    return pl.pallas_call(
        functools.partial(_norm_mod_kernel, sh_idx=sh_idx, sc_idx=sc_idx),
        grid=(b, s // tm),
        in_specs=[pl.BlockSpec((1, tm, d), lambda i, j: (i, j, 0)),
                  pl.BlockSpec((1, d), lambda i, j: (0, 0)),
                  pl.BlockSpec((1, 6, d), lambda i, j: (i, 0, 0))],
        out_specs=pl.BlockSpec((1, tm, d), lambda i, j: (i, j, 0)),
        out_shape=jax.ShapeDtypeStruct((b, s, d), BF16),
        compiler_params=_params(("parallel", "parallel"), 6 * tm * d * 4 + (4 << 20)),
        name="norm_mod",
    )(x, w.reshape(1, d), mod)


def _wproj_kernel(a_ref, w_ref, o_ref, wbf_scr):
    tn = wbf_scr.shape[0]
    rows = 64

    @pl.when(pl.program_id(1) == 0)
    def _():
        for r in range(0, tn, rows):
            wbf_scr[r:r + rows, :] = w_ref[0, r:r + rows, :].astype(BF16)

    o_ref[...] = lax.dot_general(a_ref[...], wbf_scr[...], (((1,), (1,)), ((), ())),
                                 preferred_element_type=F32).astype(o_ref.dtype)


def _wproj(a, wt_stack, layer, col0, n_cols, name, tm=1024, tn=512, out_dtype=BF16):
    m, k = a.shape
    assert n_cols % tn == 0 and m % tm == 0 and col0 % 8 == 0
    if col0 % tn == 0:
        w_spec = pl.BlockSpec((1, tn, k), lambda j, i: (layer, col0 // tn + j, 0))
    else:
        w_spec = pl.BlockSpec((pl.Element(1), pl.Element(tn), pl.Element(k)),
                              lambda j, i: (layer, pl.multiple_of(col0 + j * tn, 8), 0))
    est = 2 * (tm * k * 2 + tn * k * 4 + tm * tn * 2) + k * tn * 2 + tm * tn * 4 + (6 << 20)
    return pl.pallas_call(
        _wproj_kernel,
        grid=(n_cols // tn, m // tm),
        in_specs=[pl.BlockSpec((tm, k), lambda j, i: (i, 0)), w_spec],
        out_specs=pl.BlockSpec((tm, tn), lambda j, i: (i, j)),
        out_shape=jax.ShapeDtypeStruct((m, n_cols), out_dtype),
        scratch_shapes=[pltpu.VMEM((tn, k), BF16)],
        compiler_params=_params(("parallel", "arbitrary"), est),
        name=name,
    )(a, wt_stack)


def _rope_kernel(pos_ref, freq_ref, cos_ref, sin_ref):
    ang = pos_ref[...] * freq_ref[...]
    lane = lax.broadcasted_iota(jnp.int32, ang.shape, 1)
    live = lane < MLA_ROPE
    cos_ref[...] = jnp.where(live, jnp.cos(ang), 0.0)
    sin_ref[...] = jnp.where(live, jnp.sin(ang), 0.0)


def _rope_tables(pos_col):
    t = pos_col.shape[0]
    half = MLA_ROPE // 2
    inv = ROPE_THETA ** (-np.arange(half, dtype=np.float32) / half)
    freq = np.zeros((1, LANE), np.float32)
    freq[0, :half] = inv
    freq[0, half:2 * half] = inv
    tm = 512
    return pl.pallas_call(
        _rope_kernel,
        grid=(t // tm,),
        in_specs=[pl.BlockSpec((tm, 1), lambda i: (i, 0)),
                  pl.BlockSpec((1, LANE), lambda i: (0, 0))],
        out_specs=[pl.BlockSpec((tm, LANE), lambda i: (i, 0))] * 2,
        out_shape=[jax.ShapeDtypeStruct((t, LANE), F32)] * 2,
        compiler_params=_params(("parallel",), 16 << 20),
        name="rope_tables",
    )(pos_col, jnp.asarray(freq))


def _mla_q_kernel(x_ref, nw_ref, w_ref, cos_ref, sin_ref, o_ref, *, scale):
    x = x_ref[...].astype(F32)
    y = x * lax.rsqrt(jnp.mean(x * x, axis=-1, keepdims=True) + NORM_EPS) * nw_ref[...]
    r = jnp.dot(y.astype(BF16), w_ref[...], preferred_element_type=F32)
    cos = cos_ref[...]
    sin = sin_ref[...]
    for h in range(MLA_HEADS):
        base = h * 3 * LANE
        nope = r[:, base:base + LANE]
        rope = r[:, base + LANE:base + 2 * LANE]
        rot = r[:, base + 2 * LANE:base + 3 * LANE]
        o_ref[:, h * MLA_QK:h * MLA_QK + LANE] = (nope * scale).astype(o_ref.dtype)
        o_ref[:, h * MLA_QK + LANE:(h + 1) * MLA_QK] = (
            (rope * cos + rot * sin) * scale).astype(o_ref.dtype)


def _mla_kv_kernel(x_ref, nw_ref, w_ref, kr_ref, krot_ref, cos_ref, sin_ref, k_ref, v_ref):
    x = x_ref[...].astype(F32)
    y = x * lax.rsqrt(jnp.mean(x * x, axis=-1, keepdims=True) + NORM_EPS) * nw_ref[...]
    r = jnp.dot(y.astype(BF16), w_ref[...], preferred_element_type=F32)
    kr = (kr_ref[...] * cos_ref[...] + krot_ref[...] * sin_ref[...]).astype(k_ref.dtype)
    for h in range(MLA_HEADS):
        base = h * (MLA_NOPE + MLA_V)
        k_ref[:, h * MLA_QK:h * MLA_QK + LANE] = r[:, base:base + MLA_NOPE].astype(k_ref.dtype)
        k_ref[:, h * MLA_QK + LANE:(h + 1) * MLA_QK] = kr
        v_ref[:, h * MLA_V:(h + 1) * MLA_V] = r[:, base + MLA_NOPE:base + MLA_NOPE + MLA_V].astype(
            v_ref.dtype)


def _mla_project(proj, tail, off, q_norm, wq_ext, kv_norm, wkv, cos_t, sin_t, tm=512):
    t = proj.shape[0]
    scale = float((MLA_NOPE + MLA_ROPE) ** -0.5) * LOG2E
    q = pl.pallas_call(
        functools.partial(_mla_q_kernel, scale=scale),
        grid=(t // tm,),
        in_specs=[pl.BlockSpec((tm, MLA_Q_LORA), lambda i: (i, off["qlat"] // MLA_Q_LORA)),
                  pl.BlockSpec((1, MLA_Q_LORA), lambda i: (0, 0)),
                  pl.BlockSpec(wq_ext.shape, lambda i: (0, 0)),
                  pl.BlockSpec((tm, LANE), lambda i: (i, 0)),
                  pl.BlockSpec((tm, LANE), lambda i: (i, 0))],
        out_specs=pl.BlockSpec((tm, MLA_HEADS * MLA_QK), lambda i: (i, 0)),
        out_shape=jax.ShapeDtypeStruct((t, MLA_HEADS * MLA_QK), BF16),
        compiler_params=_params(("parallel",), 40 << 20),
        name="mla_q_proj",
    )(proj, q_norm.reshape(1, -1), wq_ext, cos_t, sin_t)
    k, v = pl.pallas_call(
        _mla_kv_kernel,
        grid=(t // tm,),
        in_specs=[pl.BlockSpec((tm, MLA_KV_LORA), lambda i: (i, off["kvlat"] // MLA_KV_LORA)),
                  pl.BlockSpec((1, MLA_KV_LORA), lambda i: (0, 0)),
                  pl.BlockSpec(wkv.shape, lambda i: (0, 0)),
                  pl.BlockSpec((tm, LANE), lambda i: (i, 0)),
                  pl.BlockSpec((tm, LANE), lambda i: (i, 1)),
                  pl.BlockSpec((tm, LANE), lambda i: (i, 0)),
                  pl.BlockSpec((tm, LANE), lambda i: (i, 0))],
        out_specs=[pl.BlockSpec((tm, MLA_HEADS * MLA_QK), lambda i: (i, 0)),
                   pl.BlockSpec((tm, MLA_HEADS * MLA_V), lambda i: (i, 0))],
        out_shape=[jax.ShapeDtypeStruct((t, MLA_HEADS * MLA_QK), BF16),
                   jax.ShapeDtypeStruct((t, MLA_HEADS * MLA_V), BF16)],
        compiler_params=_params(("parallel",), 40 << 20),
        name="mla_kv_proj",
    )(proj, kv_norm.reshape(1, -1), wkv, tail, tail, cos_t, sin_t)
    return q, k, v


def _flash_kernel(*refs, moba, q_scale, dq, dv):
    if moba:
        slope_ref, q_ref, k_ref, v_ref, posk_ref, posq_ref, o_ref, vt_scr, kmean_scr, sel_scr = refs
    else:
        q_ref, k_ref, v_ref, o_ref, vt_scr = refs
    tile = ATTN_TILE
    seq = k_ref.shape[1]
    sub = tile // MOBA_BLOCK
    nblk = seq // MOBA_BLOCK
    qi = pl.program_id(2)
    heads = range(FLASH_HEADS)

    @pl.when(qi == 0)
    def _():
        for j in heads:
            for c in range(seq // tile):
                vt_scr[j, c, 0:dv, :] = (
                    v_ref[0, c * tile:(c + 1) * tile, j * dv:(j + 1) * dv].astype(F32).T.astype(BF16))
                vt_scr[j, c, dv:dv + ONES_ROWS, :] = jnp.ones((ONES_ROWS, tile), BF16)
            if moba:
                for n in range(nblk):
                    kblk = k_ref[0, n * MOBA_BLOCK:(n + 1) * MOBA_BLOCK, j * dq:(j + 1) * dq].astype(F32)
                    kmean_scr[j, n:n + 1, :] = jnp.mean(kblk, axis=0, keepdims=True)

    q = []
    for j in heads:
        q_raw = q_ref[0, :, j * dq:(j + 1) * dq]
        q.append(q_raw if q_scale is None else (q_raw.astype(F32) * q_scale).astype(BF16))
        if moba:
            gate = lax.dot_general(kmean_scr[j], q_raw.astype(F32), (((1,), (1,)), ((), ())),
                                   precision=HIGHEST, preferred_element_type=F32)
            n_iota = lax.broadcasted_iota(jnp.int32, gate.shape, 0)
            q_blk = qi * sub + lax.broadcasted_iota(jnp.int32, gate.shape, 1) // MOBA_BLOCK
            beaten = jnp.zeros(gate.shape, F32)
            for m in range(nblk):
                gm = gate[m:m + 1, :]
                wins = jnp.where(gm > gate, 1.0, jnp.where(gm == gate, jnp.where(m < n_iota, 1.0, 0.0), 0.0))
                beaten = beaten + jnp.where(m < q_blk, wins, 0.0)
            sel_scr[j] = jnp.where(beaten < MOBA_TOPK, jnp.where(n_iota < q_blk, 0.0, MASKED), MASKED)
    if moba:
        slope = [slope_ref[pl.program_id(1) * FLASH_HEADS + j] for j in heads]
        pq = [posq_ref[0] * slope[j] for j in heads]

    def scores(j, c):
        start = pl.multiple_of(c * tile, tile)
        kb = k_ref[0, pl.ds(start, tile), j * dq:(j + 1) * dq]
        st = lax.dot_general(kb, q[j], (((1,), (1,)), ((), ())), preferred_element_type=F32)
        if moba:
            pk = posk_ref[0, pl.ds(start, tile), :] * slope[j]
            st = st - jnp.abs(pq[j] - pk)
        return st

    def block_bias(j, c):
        rows = [jnp.broadcast_to(sel_scr[j, pl.ds(c * sub + r, 1), :], (MOBA_BLOCK, tile)) for r in range(sub)]
        return jnp.concatenate(rows, axis=0)

    key_i = lax.broadcasted_iota(jnp.int32, (tile, tile), 0)
    qry_i = lax.broadcasted_iota(jnp.int32, (tile, tile), 1)
    causal_bias = jnp.where(key_i <= qry_i, 0.0, MASKED)
    init = []
    for j in heads:
        st = scores(j, qi)
        if moba:
            same_blk = (key_i // MOBA_BLOCK) == (qry_i // MOBA_BLOCK)
            st = st + jnp.where(same_blk, causal_bias, block_bias(j, qi))
        else:
            st = st + causal_bias
        m0 = jnp.max(st, axis=0, keepdims=True)
        p = jnp.exp2(st - m0)
        init.append((m0, jnp.dot(vt_scr[j, qi], p.astype(BF16), preferred_element_type=F32)))

    def body(c, carry):
        s_c = [scores(j, c) for j in heads]
        if moba:
            s_c = [s_c[j] + block_bias(j, c) for j in heads]
        m_new = [jnp.maximum(carry[j][0], jnp.max(s_c[j], axis=0, keepdims=True)) for j in heads]
        p_c = [jnp.exp2(s_c[j] - m_new[j]).astype(BF16) for j in heads]
        out = []
        for j in heads:
            m_prev, acc = carry[j]
            alpha = jnp.exp2(m_prev - m_new[j])
            acc_new = alpha * acc + jnp.dot(vt_scr[j, c], p_c[j], preferred_element_type=F32)
            out.append((m_new[j], acc_new))
        return tuple(out)

    final = lax.fori_loop(0, qi, body, tuple(init))
    for j in heads:
        _, acc = final[j]
        o_ref[0, :, j * dv:(j + 1) * dv] = (acc[0:dv] * (1.0 / acc[dv:dv + 1])).T.astype(o_ref.dtype)


def _flash(q, k, v, *, heads, dq, dv, q_off, k_off, v_off, moba=False, q_scale=None,
           slopes=None, pos_col=None, pos_row=None, name="flash"):
    b, s, _ = q.shape
    tile = ATTN_TILE
    hp = FLASH_HEADS
    assert heads % hp == 0 and q_off % (hp * dq) == 0 and k_off % (hp * dq) == 0 and v_off % (hp * dv) == 0
    qb, kb, vb = q_off // (hp * dq), k_off // (hp * dq), v_off // (hp * dv)
    in_specs = [pl.BlockSpec((1, tile, hp * dq), lambda bi, h, i: (bi, i, qb + h)),
                pl.BlockSpec((1, s, hp * dq), lambda bi, h, i: (bi, 0, kb + h)),
                pl.BlockSpec((1, s, hp * dv), lambda bi, h, i: (bi, 0, vb + h))]
    args = [q, k, v]
    scratch = [pltpu.VMEM((hp, s // tile, dv + ONES_ROWS, tile), BF16)]
    if moba:
        in_specs = [pl.BlockSpec(memory_space=pltpu.SMEM)] + in_specs + [
            pl.BlockSpec((1, s, 1), lambda bi, h, i: (bi, 0, 0)),
            pl.BlockSpec((1, 1, tile), lambda bi, h, i: (bi, 0, i))]
        args = [slopes] + args + [pos_col, pos_row]
        scratch += [pltpu.VMEM((hp, s // MOBA_BLOCK, dq), F32), pltpu.VMEM((hp, s // MOBA_BLOCK, tile), F32)]
    est = hp * (2 * (s * dq * 2 + s * dv * 2) + s * dv * 2) + s * LANE * 4 * 2 + (20 << 20)
    return pl.pallas_call(
        functools.partial(_flash_kernel, moba=moba, q_scale=q_scale, dq=dq, dv=dv),
        grid=(b, heads // hp, s // tile),
        in_specs=in_specs,
        out_specs=pl.BlockSpec((1, tile, hp * dv), lambda bi, h, i: (bi, i, h)),
        out_shape=jax.ShapeDtypeStruct((b, s, heads * dv), BF16),
        scratch_shapes=scratch,
        compiler_params=_params(("parallel", "parallel", "arbitrary"), est),
        name=name,
    )(*args)


def _swa_kernel(sink_ref, q_ref, kp_ref, kc_ref, vp_ref, vc_ref, pq_ref, pkp_ref, pkc_ref, o_ref,
                *, slopes):
    w = SWA_WINDOW
    n = pl.program_id(1)
    kx = jnp.concatenate([kp_ref[0], kc_ref[0]], axis=0)
    vx = jnp.concatenate([vp_ref[0], vc_ref[0]], axis=0)
    pk = jnp.concatenate([pkp_ref[0], pkc_ref[0]], axis=1)
    dist = jnp.abs(pq_ref[0] - pk)
    qi = lax.broadcasted_iota(jnp.int32, dist.shape, 0)
    kk = lax.broadcasted_iota(jnp.int32, dist.shape, 1)
    allowed = jnp.where(kk > qi, jnp.where(kk <= qi + w, 0.0, MASKED), MASKED)
    mask_bias = jnp.where(kk >= w, allowed, jnp.where(n > 0, allowed, MASKED))
    scale = float(SWA_HEAD_DIM ** -0.5)
    rep = SWA_HEADS // SWA_KV_HEADS
    heads = range(SWA_HEADS)
    col = [(2 * (h // rep) + h % 2) * LANE for h in heads]
    s = [lax.dot_general(q_ref[0, :, (h // 2) * LANE:(h // 2 + 1) * LANE], kx[:, col[h]:col[h] + LANE],
                         (((1,), (1,)), ((), ())), preferred_element_type=F32) for h in heads]
    s = [s[h] * scale + (mask_bias - float(slopes[h]) * dist) for h in heads]
    m = [jnp.maximum(jnp.max(s[h], axis=1, keepdims=True), sink_ref[h]) for h in heads]
    p = [jnp.exp(s[h] - m[h]) for h in heads]
    inv = [1.0 / (jnp.sum(p[h], axis=1, keepdims=True) + jnp.exp(sink_ref[h] - m[h])) for h in heads]
    p = [(p[h] * inv[h]).astype(BF16) for h in heads]
    for pair in range(SWA_HEADS // 2):
        h0, h1 = 2 * pair, 2 * pair + 1
        acc = (jnp.dot(p[h0], vx[:, col[h0]:col[h0] + LANE], preferred_element_type=F32)
               + jnp.dot(p[h1], vx[:, col[h1]:col[h1] + LANE], preferred_element_type=F32))
        o_ref[0, :, pair * LANE:(pair + 1) * LANE] = acc.astype(o_ref.dtype)


def _swa(proj3, off, sinks, pos_col, pos_row, slopes):
    b, s, _ = proj3.shape
    w = SWA_WINDOW
    qw = SWA_HEADS * SWA_HEAD_DIM
    kw = 4 * LANE
    prev = lambda j: jnp.maximum(j - 1, 0)
    return pl.pallas_call(
        functools.partial(_swa_kernel, slopes=tuple(float(v) for v in slopes)),
        grid=(b, s // w),
        in_specs=[pl.BlockSpec(memory_space=pltpu.SMEM),
                  pl.BlockSpec((1, w, qw), lambda i, j: (i, j, off["sq"] // qw)),
                  pl.BlockSpec((1, w, kw), lambda i, j: (i, prev(j), off["skx"] // kw)),
                  pl.BlockSpec((1, w, kw), lambda i, j: (i, j, off["skx"] // kw)),
                  pl.BlockSpec((1, w, kw), lambda i, j: (i, prev(j), off["svx"] // kw)),
                  pl.BlockSpec((1, w, kw), lambda i, j: (i, j, off["svx"] // kw)),
                  pl.BlockSpec((1, w, 1), lambda i, j: (i, j, 0)),
                  pl.BlockSpec((1, 1, w), lambda i, j: (i, 0, prev(j))),
                  pl.BlockSpec((1, 1, w), lambda i, j: (i, 0, j))],
        out_specs=pl.BlockSpec((1, w, qw), lambda i, j: (i, j, 0)),
        out_shape=jax.ShapeDtypeStruct((b, s, qw), BF16),
        compiler_params=_params(("parallel", "parallel"), 24 << 20),
        name="swa_attention",
    )(sinks, proj3, proj3, proj3, proj3, proj3, pos_col, pos_row, pos_row)


def _ssd_kernel(xs_ref, bc_ref, z_ref, dt_ref, cwx_ref, cwb_ref, cbx_ref, cbb_ref, dtb_ref,
                alog_ref, dsk_ref, nw_ref, exp_ref, o_ref, padx_scr, padb_scr, st_scr):
    ch = SSM_CHUNK
    c = pl.program_id(1)

    @pl.when(c == 0)
    def _():
        padx_scr[0:8, :] = jnp.zeros((8, padx_scr.shape[1]), F32)
        padb_scr[0:8, :] = jnp.zeros((8, padb_scr.shape[1]), F32)
        st_scr[...] = jnp.zeros(st_scr.shape, F32)

    padx_scr[8:8 + ch, :] = xs_ref[0].astype(F32)
    padb_scr[8:8 + ch, :] = bc_ref[0].astype(F32)

    def conv(pad_scr, w_ref, b_ref):
        acc = b_ref[...] + w_ref[0:1, :] * pad_scr[5:5 + ch, :]
        for k in range(1, SSM_CONV):
            acc = acc + w_ref[k:k + 1, :] * pad_scr[5 + k:5 + k + ch, :]
        return _silu(acc)

    xs = conv(padx_scr, cwx_ref, cbx_ref)
    bcv = conv(padb_scr, cwb_ref, cbb_ref)
    padx_scr[0:8, :] = padx_scr[ch:ch + 8, :]
    padb_scr[0:8, :] = padb_scr[ch:ch + 8, :]

    gn = SSM_GROUPS * SSM_STATE
    dtr = dt_ref[0] + dtb_ref[...]
    dt = jnp.maximum(dtr, 0.0) + jnp.log(1.0 + jnp.exp(-jnp.abs(dtr)))
    a = dt * (-jnp.exp(alog_ref[...]))
    expand = exp_ref[...]
    row = lax.broadcasted_iota(jnp.int32, (ch, ch), 0)
    colm = lax.broadcasted_iota(jnp.int32, (ch, ch), 1)
    tril = row >= colm
    tri = jnp.where(tril, 1.0, 0.0).astype(BF16)

    def times_select(v, sel):
        return sum(jnp.dot(part, sel, preferred_element_type=F32) for part in _split3(v))

    dt_e = times_select(dt, expand)
    acs = sum(jnp.dot(tri, part, preferred_element_type=F32) for part in _split3(a))
    acs_e = times_select(acs, expand)
    acs_t = acs.T
    x_dt = xs * dt_e
    last = acs_e[ch - 1:ch, :]
    x_dec = (x_dt * jnp.exp(last - acs_e)).astype(BF16)
    lane = lax.broadcasted_iota(jnp.int32, (ch, LANE), 1)
    lo = lane < SSM_HEAD_DIM
    half_w = SSM_D_INNER // SSM_GROUPS
    heads_per_group = SSM_HEADS // SSM_GROUPS
    groups = range(SSM_GROUPS)
    heads = range(SSM_HEADS)
    bg = [bcv[:, g * SSM_STATE:(g + 1) * SSM_STATE] for g in groups]
    cg = [bcv[:, gn + g * SSM_STATE:gn + (g + 1) * SSM_STATE].astype(BF16) for g in groups]
    gmat = [lax.dot_general(cg[g], bg[g].astype(BF16), (((1,), (1,)), ((), ())),
                            preferred_element_type=F32) for g in groups]
    st_in = [st_scr[:, g * half_w:(g + 1) * half_w] for g in groups]
    yoff = [jnp.dot(cg[g], st_in[g].astype(BF16), preferred_element_type=F32) for g in groups]
    upd = [jnp.dot(bg[g].T.astype(BF16), x_dec[:, g * half_w:(g + 1) * half_w],
                   preferred_element_type=F32) for g in groups]
    seg = [jnp.exp(jnp.where(tril, acs[:, h:h + 1] - acs_t[h:h + 1, :], -jnp.inf)) for h in heads]
    mmat = [(gmat[h // heads_per_group] * seg[h]).astype(BF16) for h in heads]
    xh = []
    for h in heads:
        xp = x_dt[:, (h // 2) * LANE:(h // 2 + 1) * LANE]
        xh.append((jnp.where(lo, xp, 0.0) if h % 2 == 0 else jnp.where(lo, 0.0, xp)).astype(BF16))
    ydiag = [jnp.dot(mmat[2 * pr], xh[2 * pr], preferred_element_type=F32)
             + jnp.dot(mmat[2 * pr + 1], xh[2 * pr + 1], preferred_element_type=F32)
             for pr in range(SSM_HEADS // 2)]
    for g in groups:
        st_scr[:, g * half_w:(g + 1) * half_w] = (
            st_in[g] * jnp.exp(last[:, g * half_w:(g + 1) * half_w]) + upd[g])
    y = (jnp.concatenate(ydiag, axis=1) + jnp.concatenate(yoff, axis=1) * jnp.exp(acs_e)
         + dsk_ref[...] * xs)
    gz = y * _silu(z_ref[0].astype(F32))
    outs = []
    for g in range(SSM_GROUPS):
        gg = gz[:, g * half_w:(g + 1) * half_w]
        outs.append(gg * lax.rsqrt(jnp.mean(gg * gg, axis=-1, keepdims=True) + NORM_EPS))
    o_ref[0] = (jnp.concatenate(outs, axis=1) * nw_ref[...]).astype(o_ref.dtype)


def _ssd(proj3, tail3, off, conv_w, conv_b, dt_bias, a_log, d_skip, norm_w):
    b, s, _ = proj3.shape
    ch = SSM_CHUNK
    di = SSM_D_INNER
    pad16 = lambda v: jnp.zeros((1, LANE), F32).at[0, :SSM_HEADS].set(v)
    expand = np.zeros((LANE, di), np.float32)
    for h in range(SSM_HEADS):
        expand[h, h * SSM_HEAD_DIM:(h + 1) * SSM_HEAD_DIM] = 1.0
    full = lambda shape: pl.BlockSpec(shape, lambda i, j: (0,) * len(shape))
    return pl.pallas_call(
        _ssd_kernel,
        grid=(b, s // ch),
        in_specs=[pl.BlockSpec((1, ch, di), lambda i, j: (i, j, off["xs"] // di)),
                  pl.BlockSpec((1, ch, SSM_BC), lambda i, j: (i, j, off["bc"] // SSM_BC)),
                  pl.BlockSpec((1, ch, di), lambda i, j: (i, j, off["z"] // di)),
                  pl.BlockSpec((1, ch, LANE), lambda i, j: (i, j, 2)),
                  full((SSM_CONV, di)), full((SSM_CONV, SSM_BC)), full((1, di)), full((1, SSM_BC)),
                  full((1, LANE)), full((1, LANE)), full((1, di)), full((1, di)), full((LANE, di))],
        out_specs=pl.BlockSpec((1, ch, di), lambda i, j: (i, j, 0)),
        out_shape=jax.ShapeDtypeStruct((b, s, di), BF16),
        scratch_shapes=[pltpu.VMEM((ch + 8, di), F32), pltpu.VMEM((ch + 8, SSM_BC), F32),
                        pltpu.VMEM((SSM_STATE, di), F32)],
        compiler_params=_params(("parallel", "arbitrary"), 32 << 20),
        name="ssd_mixer",
    )(proj3, proj3, proj3, tail3, conv_w[:, :di], conv_w[:, di:], conv_b[:di].reshape(1, di),
      conv_b[di:].reshape(1, SSM_BC), pad16(dt_bias), pad16(a_log),
      jnp.repeat(d_skip, SSM_HEAD_DIM).reshape(1, di), norm_w.reshape(1, di), jnp.asarray(expand, BF16))


def _merge_kernel(o0_ref, o1_ref, o2_ref, o3_ref, w_ref, g0_ref, g1_ref, g2_ref, g3_ref, out_ref, wbf_scr):
    @pl.when(pl.program_id(1) == 0)
    def _():
        for r in range(N_BRANCH):
            wbf_scr[r] = w_ref[0, r].astype(BF16)

    acc = None
    for r, (o_ref, g_ref) in enumerate(((o0_ref, g0_ref), (o1_ref, g1_ref), (o2_ref, g2_ref),
                                        (o3_ref, g3_ref))):
        y = jnp.dot(o_ref[...], wbf_scr[r], preferred_element_type=F32)
        gate = 1.0 / (1.0 + jnp.exp(-g_ref[...].astype(F32)))
        acc = gate * y if acc is None else acc + gate * y
    out_ref[...] = acc.astype(out_ref.dtype)


def _merge(branches, w_branch, layer, proj, d, tm=1024, tn=512):
    t = proj.shape[0]
    nj = d // tn
    o_spec = pl.BlockSpec((tm, BRANCH_W), lambda j, i: (i, 0))
    g_specs = [pl.BlockSpec((tm, tn), functools.partial(lambda j, i, r: (i, r * nj + j), r=r))
               for r in range(N_BRANCH)]
    est = (2 * (4 * tm * BRANCH_W * 2 + 4 * BRANCH_W * tn * 4 + 4 * tm * tn * 2 + tm * tn * 2)
           + 4 * BRANCH_W * tn * 2 + 3 * tm * tn * 4 + (6 << 20))
    return pl.pallas_call(
        _merge_kernel,
        grid=(nj, t // tm),
        in_specs=[o_spec] * 4 + [pl.BlockSpec((1, N_BRANCH, BRANCH_W, tn), lambda j, i: (layer, 0, 0, j))]
        + g_specs,
        out_specs=pl.BlockSpec((tm, tn), lambda j, i: (i, j)),
        out_shape=jax.ShapeDtypeStruct((t, d), BF16),
        scratch_shapes=[pltpu.VMEM((N_BRANCH, BRANCH_W, tn), BF16)],
        compiler_params=_params(("parallel", "arbitrary"), est),
        name="branch_merge",
    )(*branches, w_branch, proj, proj, proj, proj)


def _out_proj_kernel(a_ref, w_ref, x_ref, mod_ref, o_ref, wbf_scr, *, g_idx):
    @pl.when(pl.program_id(1) == 0)
    def _():
        rows = 512
        for r in range(0, wbf_scr.shape[0], rows):
            wbf_scr[r:r + rows, :] = w_ref[0, r:r + rows, :].astype(BF16)

    y = jnp.dot(a_ref[...], wbf_scr[...], preferred_element_type=F32)
    o_ref[...] = x_ref[...] + mod_ref[0, g_idx:g_idx + 1, :] * y


def _out_proj(merged, w_out, layer, x2, mod, seq, g_idx, tm=1024, tn=512):
    t, d = x2.shape
    per_b = seq // tm
    est = 2 * (tm * d * 2 + d * tn * 4 + 2 * tm * tn * 4) + d * tn * 2 + tm * tn * 4 + (4 << 20)
    return pl.pallas_call(
        functools.partial(_out_proj_kernel, g_idx=g_idx),
        grid=(d // tn, t // tm),
        in_specs=[pl.BlockSpec((tm, d), lambda j, i: (i, 0)),
                  pl.BlockSpec((1, d, tn), lambda j, i: (layer, 0, j)),
                  pl.BlockSpec((tm, tn), lambda j, i: (i, j)),
                  pl.BlockSpec((1, 6, tn), lambda j, i: (i // per_b, 0, j))],
        out_specs=pl.BlockSpec((tm, tn), lambda j, i: (i, j)),
        out_shape=jax.ShapeDtypeStruct((t, d), F32),
        scratch_shapes=[pltpu.VMEM((d, tn), BF16)],
        compiler_params=_params(("parallel", "arbitrary"), est),
        name="out_proj",
    )(merged, w_out, x2, mod)


def _router_kernel(x_ref, w_ref, mod_ref, rwh_ref, rwl_ref, rb_ref, h_ref, ri_ref, rwt_ref, cnt_ref, carry_scr,
                   *, sh_idx, sc_idx):
    step = pl.program_id(0) * pl.num_programs(1) + pl.program_id(1)

    @pl.when(step == 0)
    def _():
        carry_scr[...] = jnp.zeros(carry_scr.shape, F32)

    x = x_ref[0]
    y = x * lax.rsqrt(jnp.mean(x * x, axis=-1, keepdims=True) + NORM_EPS) * w_ref[...]
    m = mod_ref[0]
    h = y * (1.0 + m[sc_idx:sc_idx + 1]) + m[sh_idx:sh_idx + 1]
    for g in range(h.shape[1] // (2 * LANE)):
        h_ref[0, :, g * LANE:(g + 1) * LANE] = _pack_bf16_pair(h[:, 2 * g * LANE:(2 * g + 1) * LANE],
                                                               h[:, (2 * g + 1) * LANE:(2 * g + 2) * LANE])
    h_hi = h.astype(BF16)
    h_lo = (h - h_hi.astype(F32)).astype(BF16)
    nt = (((1,), (1,)), ((), ()))
    logits = (lax.dot_general(h_hi, rwh_ref[...], nt, preferred_element_type=F32)
              + lax.dot_general(h_hi, rwl_ref[...], nt, preferred_element_type=F32)
              + lax.dot_general(h_lo, rwh_ref[...], nt, preferred_element_type=F32)) + rb_ref[...]
    tm = logits.shape[0]
    lane = lax.broadcasted_iota(jnp.int32, logits.shape, 1)
    big = jnp.int32(4 * LANE)
    neg = -jnp.inf

    def first_argmax(vals):
        mx = jnp.max(vals, axis=1, keepdims=True)
        idx = jnp.min(jnp.where(vals == mx, lane, big), axis=1, keepdims=True)
        return mx, idx

    lg = jnp.where(lane < N_GROUPS, logits, neg)
    gmax, gidx = first_argmax(lg)
    g_w = 1.0 / jnp.sum(jnp.exp(lg - gmax), axis=1, keepdims=True)
    lo = N_GROUPS + gidx * EXPERTS_PER_GROUP
    in_group = jnp.where(lane >= lo, jnp.where(lane < lo + EXPERTS_PER_GROUP, 1.0, 0.0), 0.0) > 0.0
    le = jnp.where(in_group, logits, neg)
    m1, i1 = first_argmax(le)
    le2 = jnp.where(lane == i1, neg, le)
    m2, i2 = first_argmax(le2)
    denom = jnp.sum(jnp.exp(le - m1), axis=1, keepdims=True)
    p1 = 1.0 / denom
    p2 = jnp.exp(m2 - m1) / denom
    w1 = p1 / (p1 + p2) * g_w
    w2 = p2 / (p1 + p2) * g_w
    e1 = i1 - N_GROUPS
    e2 = i2 - N_GROUPS

    oh1 = jnp.where(lane == e1, 1.0, 0.0)
    oh2 = jnp.where(lane == e2, 1.0, 0.0)
    both = oh1 + oh2
    r_i = lax.broadcasted_iota(jnp.int32, (tm, tm), 0)
    c_i = lax.broadcasted_iota(jnp.int32, (tm, tm), 1)
    strict = jnp.where(r_i > c_i, 1.0, 0.0).astype(BF16)
    before = jnp.dot(strict, both.astype(BF16), preferred_element_type=F32) + carry_scr[...]
    rank1 = jnp.sum(before * oh1, axis=1, keepdims=True).astype(jnp.int32)
    rank2 = jnp.sum(before * oh2, axis=1, keepdims=True).astype(jnp.int32)
    carry_scr[...] = carry_scr[...] + jnp.sum(both, axis=0, keepdims=True)
    cnt_ref[...] = carry_scr[...]
    ri_ref[0] = jnp.where(lane == 0, e1, jnp.where(lane == 1, e2, jnp.where(
        lane == 2, rank1, jnp.where(lane == 3, rank2, 0))))
    rwt_ref[0] = jnp.where(lane == 0, w1, jnp.where(lane == 1, w2, 0.0))


def _router(x, w, mod, wg, bg, wr, br, sh_idx, sc_idx, tm=256):
    b, s, d = x.shape
    rw = jnp.concatenate([wg.T, wr.T, jnp.zeros((LANE - N_GROUPS - N_EXPERTS, d), F32)], axis=0)
    rb = jnp.zeros((1, LANE), F32).at[0, :N_GROUPS].set(bg).at[0, N_GROUPS:N_GROUPS + N_EXPERTS].set(br)
    rw_hi = rw.astype(BF16)
    rw_lo = (rw - rw_hi.astype(F32)).astype(BF16)
    return pl.pallas_call(
        functools.partial(_router_kernel, sh_idx=sh_idx, sc_idx=sc_idx),
        grid=(b, s // tm),
        in_specs=[pl.BlockSpec((1, tm, d), lambda i, j: (i, j, 0)),
                  pl.BlockSpec((1, d), lambda i, j: (0, 0)),
                  pl.BlockSpec((1, 6, d), lambda i, j: (i, 0, 0)),
                  pl.BlockSpec((LANE, d), lambda i, j: (0, 0)),
                  pl.BlockSpec((LANE, d), lambda i, j: (0, 0)),
                  pl.BlockSpec((1, LANE), lambda i, j: (0, 0))],
        out_specs=[pl.BlockSpec((1, tm, d // 2), lambda i, j: (i, j, 0)),
                   pl.BlockSpec((1, tm, LANE), lambda i, j: (i, j, 0)),
                   pl.BlockSpec((1, tm, LANE), lambda i, j: (i, j, 0)),
                   pl.BlockSpec((1, LANE), lambda i, j: (0, 0))],
        out_shape=[jax.ShapeDtypeStruct((b, s, d // 2), jnp.uint32),
                   jax.ShapeDtypeStruct((b, s, LANE), jnp.int32),
                   jax.ShapeDtypeStruct((b, s, LANE), F32),
                   jax.ShapeDtypeStruct((1, LANE), F32)],
        scratch_shapes=[pltpu.VMEM((1, LANE), F32)],
        compiler_params=_params(("arbitrary", "arbitrary"), 8 * tm * d * 4 + d * LANE * 8 + (8 << 20)),
        name="moe_router",
    )(x, w.reshape(1, d), mod, rw_hi, rw_lo, rb)


def _slot_kernel(d1_ref, d2_ref, o_ref):
    def clear(i, carry):
        o_ref[i] = 0
        return carry

    lax.fori_loop(0, o_ref.shape[0], clear, 0, unroll=16)

    def place(t, carry):
        o_ref[d1_ref[t]] = t
        o_ref[d2_ref[t]] = t
        return carry

    lax.fori_loop(0, d1_ref.shape[0], place, 0, unroll=8)


def _slot_tokens(dest1, dest2, cap):
    smem = pl.BlockSpec(memory_space=pltpu.SMEM)
    return pl.pallas_call(
        _slot_kernel,
        in_specs=[smem, smem],
        out_specs=smem,
        out_shape=jax.ShapeDtypeStruct((cap,), jnp.int32),
        name="moe_slot_tokens",
    )(dest1, dest2)


def _expert_kernel(tok_ref, be_ref, nu_ref, first_ref, nxt_ref, h_hbm, wg_hbm, wu_hbm, wd_hbm, o_ref,
                   x_scr, xb_scr, sg_scr, su_scr, sd_scr, wg_scr, wu_scr, wd_scr, sem_x, sem_w, *, layer):
    blk = pl.program_id(0)
    n_used = nu_ref[0]
    rows = x_scr.shape[1]
    slot = lax.rem(blk, 2)

    def row_copy(b, s, i):
        tok = tok_ref[b * rows + i]
        return pltpu.make_async_copy(h_hbm.at[pl.ds(tok, 1)], x_scr.at[s, pl.ds(i, 1)], sem_x.at[s])

    def start_rows(b, s):
        def go(i, carry):
            row_copy(b, s, i).start()
            return carry

        lax.fori_loop(0, rows, go, 0, unroll=8)

    def weight_copies(e):
        return (pltpu.make_async_copy(wg_hbm.at[layer, e], sg_scr, sem_w.at[0]),
                pltpu.make_async_copy(wu_hbm.at[layer, e], su_scr, sem_w.at[1]),
                pltpu.make_async_copy(wd_hbm.at[layer, e], sd_scr, sem_w.at[2]))

    @pl.when(blk < n_used)
    def _():
        @pl.when(blk == 0)
        def _():
            start_rows(0, 0)
            for cp in weight_copies(be_ref[0]):
                cp.start()

        @pl.when(blk + 1 < n_used)
        def _():
            start_rows(blk + 1, 1 - slot)

        @pl.when(first_ref[blk] == 1)
        def _():
            for cp in weight_copies(be_ref[blk]):
                cp.wait()
            step = 512
            for r in range(0, sg_scr.shape[0], step):
                wg_scr[r:r + step, :] = sg_scr[r:r + step, :].astype(BF16)
                wu_scr[r:r + step, :] = su_scr[r:r + step, :].astype(BF16)
            for r in range(0, sd_scr.shape[1], step * 4):
                wd_scr[:, r:r + step * 4] = sd_scr[:, r:r + step * 4].astype(BF16)

            @pl.when(nxt_ref[blk] >= 0)
            def _():
                for cp in weight_copies(nxt_ref[blk]):
                    cp.start()

        def wait(i, carry):
            row_copy(blk, slot, i).wait()
            return carry

        lax.fori_loop(0, rows, wait, 0, unroll=8)
        groups = xb_scr.shape[1] // (2 * LANE)
        for g in range(groups):
            lo, hi = _unpack_bf16_pair(x_scr[slot, :, g * LANE:(g + 1) * LANE])
            xb_scr[:, 2 * g * LANE:(2 * g + 1) * LANE] = lo.astype(BF16)
            xb_scr[:, (2 * g + 1) * LANE:(2 * g + 2) * LANE] = hi.astype(BF16)
        xb = xb_scr[...]
        gate = jnp.dot(xb, wg_scr[...], preferred_element_type=F32)
        up = jnp.dot(xb, wu_scr[...], preferred_element_type=F32)
        act = (_silu(gate) * up).astype(BF16)
        for g in range(groups):
            y = jnp.dot(act, wd_scr[:, 2 * g * LANE:(2 * g + 2) * LANE], preferred_element_type=F32)
            o_ref[:, g * LANE:(g + 1) * LANE] = _pack_bf16_pair(y[:, :LANE], y[:, LANE:])

    @pl.when(blk >= n_used)
    def _():
        o_ref[...] = jnp.zeros(o_ref.shape, o_ref.dtype)


def _experts(slot_tok, block_expert, n_used, first, nxt, h2, wg_stack, wu_stack, wd_stack, layer):
    t, dw = h2.shape
    d = 2 * dw
    n_blocks = block_expert.shape[0]
    hid = wg_stack.shape[3]
    est = 3 * d * hid * 6 + EXP_BLOCK * d * (2 * 2 + 2 + 2 * 2 + 8) + (6 << 20)
    any_spec = pl.BlockSpec(memory_space=pl.ANY)
    return pl.pallas_call(
        functools.partial(_expert_kernel, layer=layer),
        grid_spec=pltpu.PrefetchScalarGridSpec(
            num_scalar_prefetch=5,
            grid=(n_blocks,),
            in_specs=[any_spec, any_spec, any_spec, any_spec],
            out_specs=pl.BlockSpec((EXP_BLOCK, dw), lambda i, *_: (i, 0)),
            scratch_shapes=[pltpu.VMEM((2, EXP_BLOCK, dw), jnp.uint32), pltpu.VMEM((EXP_BLOCK, d), BF16),
                            pltpu.VMEM((d, hid), F32), pltpu.VMEM((d, hid), F32), pltpu.VMEM((hid, d), F32),
                            pltpu.VMEM((d, hid), BF16), pltpu.VMEM((d, hid), BF16), pltpu.VMEM((hid, d), BF16),
                            pltpu.SemaphoreType.DMA((2,)), pltpu.SemaphoreType.DMA((3,))]),
        out_shape=jax.ShapeDtypeStruct((n_blocks * EXP_BLOCK, dw), jnp.uint32),
        compiler_params=_params(("arbitrary",), est),
        name="moe_experts",
    )(slot_tok, block_expert, n_used, first, nxt, h2, wg_stack, wu_stack, wd_stack)


def _combine_kernel(d1_ref, d2_ref, ys_hbm, x_ref, rw_ref, mod_ref, fw_ref, o_ref, a_scr, b_scr, sem,
                    *, g_idx, final):
    i = pl.program_id(0)
    rows = a_scr.shape[1]
    slot = lax.rem(i, 2)

    def copies(step, s, r):
        t = step * rows + r
        return (pltpu.make_async_copy(ys_hbm.at[pl.ds(d1_ref[t], 1)], a_scr.at[s, pl.ds(r, 1)], sem.at[0, s]),
                pltpu.make_async_copy(ys_hbm.at[pl.ds(d2_ref[t], 1)], b_scr.at[s, pl.ds(r, 1)], sem.at[1, s]))

    def start_rows(step, s):
        def go(r, carry):
            ca, cb = copies(step, s, r)
            ca.start()
            cb.start()
            return carry

        lax.fori_loop(0, rows, go, 0, unroll=8)

    @pl.when(i == 0)
    def _():
        start_rows(0, 0)

    def wait_rows(s):
        def wait(r, carry):
            ca, cb = copies(i, s, r)
            ca.wait()
            cb.wait()
            return carry

        lax.fori_loop(0, rows, wait, 0, unroll=8)

    wait_rows(slot)
    last = pl.num_programs(0) - 1
    nxt = jnp.minimum(i + 1, last)
    for r in range(rows):
        ca, cb = copies(nxt, 1 - slot, r)
        ca.start()
        cb.start()
    rw = rw_ref[...]
    w1 = rw[:, 0:1]
    w2 = rw[:, 1:2]
    d = x_ref.shape[1]
    ssq = jnp.zeros((rows, 1), F32)
    for g in range(d // (2 * LANE)):
        halves_a = _unpack_bf16_pair(a_scr[slot, :, g * LANE:(g + 1) * LANE])
        halves_b = _unpack_bf16_pair(b_scr[slot, :, g * LANE:(g + 1) * LANE])
        for half in range(2):
            c0 = (2 * g + half) * LANE
            moe = w1 * halves_a[half] + w2 * halves_b[half]
            xn = x_ref[:, c0:c0 + LANE] + mod_ref[0, g_idx:g_idx + 1, c0:c0 + LANE] * moe
            o_ref[:, c0:c0 + LANE] = xn
            if final:
                ssq = ssq + jnp.sum(xn * xn, axis=-1, keepdims=True)
    if final:
        o_ref[...] = o_ref[...] * lax.rsqrt(ssq * (1.0 / d) + NORM_EPS) * fw_ref[...]

    @pl.when(i == last)
    def _():
        wait_rows(1 - slot)


def _combine(dest1, dest2, ys, x2, route_w, mod, final_w, seq, g_idx, final, tm=128):
    t, d = x2.shape
    per_b = seq // tm
    est = 2 * (2 * tm * d * 4 + tm * LANE * 4) + 4 * tm * d * 4 + (6 << 20)
    return pl.pallas_call(
        functools.partial(_combine_kernel, g_idx=g_idx, final=final),
        grid_spec=pltpu.PrefetchScalarGridSpec(
            num_scalar_prefetch=2,
            grid=(t // tm,),
            in_specs=[pl.BlockSpec(memory_space=pl.ANY),
                      pl.BlockSpec((tm, d), lambda i, d1, d2: (i, 0)),
                      pl.BlockSpec((tm, LANE), lambda i, d1, d2: (i, 0)),
                      pl.BlockSpec((1, 6, d), lambda i, d1, d2: (i // per_b, 0, 0)),
                      pl.BlockSpec((1, d), lambda i, d1, d2: (0, 0))],
            out_specs=pl.BlockSpec((tm, d), lambda i, d1, d2: (i, 0)),
            scratch_shapes=[pltpu.VMEM((2, tm, d // 2), jnp.uint32), pltpu.VMEM((2, tm, d // 2), jnp.uint32),
                            pltpu.SemaphoreType.DMA((2, 2))]),
        out_shape=jax.ShapeDtypeStruct((t, d), F32),
        compiler_params=_params(("arbitrary",), est),
        name="moe_combine",
    )(dest1, dest2, ys, x2, route_w, mod, final_w.reshape(1, d))


def _small_in_weights(wt_stack, layer):
    d = wt_stack.shape[2]
    bounds = np.cumsum((0,) + IN_SIZES_HEAD)
    seg = [wt_stack[layer, bounds[i]:bounds[i + 1], :] for i in range(3, len(IN_SIZES_HEAD))]
    dt, q_lat, kv_lat, k_rope, sq, sk, sv = seg
    z64 = jnp.zeros((SWA_HEAD_DIM, d), wt_stack.dtype)

    def spread(w):
        h0, h1 = w[:SWA_HEAD_DIM], w[SWA_HEAD_DIM:]
        return jnp.concatenate([h0, z64, z64, h0, h1, z64, z64, h1], axis=0)

    half = MLA_ROPE // 2
    rot = jnp.concatenate([-k_rope[half:], k_rope[:half]], axis=0)
    small = jnp.concatenate([sq, spread(sk), spread(sv), kv_lat,
                             jnp.zeros((OFF_S["qlat"] - OFF_S["kvlat"] - MLA_KV_LORA, d), wt_stack.dtype),
                             q_lat], axis=0)
    tail = jnp.concatenate([k_rope, z64, rot, z64, dt,
                            jnp.zeros((LANE - SSM_HEADS, d), wt_stack.dtype)], axis=0)
    return small[None], tail[None]


def _extend_wq(wq_b):
    k = wq_b.shape[0]
    w = wq_b.reshape(k, MLA_HEADS, MLA_NOPE + MLA_ROPE)
    z = jnp.zeros((k, MLA_HEADS, LANE - MLA_ROPE), wq_b.dtype)
    rope = w[..., MLA_NOPE:]
    half = MLA_ROPE // 2
    rot = jnp.concatenate([-rope[..., half:], rope[..., :half]], axis=-1)
    return jnp.concatenate([w[..., :MLA_NOPE], rope, z, rot, z], axis=-1).reshape(k, -1).astype(BF16)


def _mixer(x, mod, pos_col, pos_row, cos_t, sin_t, norm_w, w_in_stack, layer, conv_w, conv_b, dt_bias, a_log,
           d_skip, ssm_norm, mla_q_norm, mla_wq_b, mla_kv_norm, mla_wkv_b, swa_sinks, w_branch, w_out):
    b, s, d = x.shape
    t = b * s
    swa_slopes, moba_slopes = _alibi_slopes()
    h = _norm_mod(x, norm_w, mod, sh_idx=0, sc_idx=1)
    h2 = h.reshape(t, d)
    w_small, w_tail = _small_in_weights(w_in_stack, layer)
    proj_a = _wproj(h2, w_in_stack, layer, 0, A_COLS, "in_proj_head")
    proj_g = _wproj(h2, w_in_stack, layer, GATE_COL0, N_BRANCH * d, "in_proj_gates")
    proj_s = _wproj(h2, w_small, 0, 0, S_COLS, "in_proj_small", tn=768)
    tail = _wproj(h2, w_tail, 0, 0, 3 * LANE, "in_proj_tail", tn=3 * LANE, out_dtype=F32)
    proj_a3 = proj_a.reshape(b, s, A_COLS)
    proj_s3 = proj_s.reshape(b, s, S_COLS)
    tail3 = tail.reshape(b, s, 3 * LANE)

    o_moba = _flash(proj_a3, proj_a3, proj_a3, heads=MOBA_HEADS, dq=MOBA_HEAD_DIM, dv=MOBA_HEAD_DIM,
                    q_off=OFF_A["mq"], k_off=OFF_A["mk"], v_off=OFF_A["mv"],
                    moba=True, q_scale=float(MOBA_HEAD_DIM ** -0.5) * LOG2E,
                    slopes=jnp.asarray(moba_slopes * np.float32(LOG2E)),
                    pos_col=pos_col, pos_row=pos_row, name="moba_attention")
    o_ssm = _ssd(proj_a3, tail3, OFF_A, conv_w, conv_b, dt_bias, a_log, d_skip, ssm_norm)
    q_m, k_m, v_m = _mla_project(proj_s, tail, OFF_S, mla_q_norm, _extend_wq(mla_wq_b), mla_kv_norm,
                                 mla_wkv_b.astype(BF16), cos_t, sin_t)
    o_mla = _flash(q_m.reshape(b, s, -1), k_m.reshape(b, s, -1), v_m.reshape(b, s, -1),
                   heads=MLA_HEADS, dq=MLA_QK, dv=MLA_V, q_off=0, k_off=0, v_off=0, name="mla_attention")
    o_swa = _swa(proj_s3, OFF_S, swa_sinks, pos_col, pos_row, swa_slopes)

    branches = [o.reshape(t, BRANCH_W) for o in (o_moba, o_ssm, o_mla, o_swa)]
    merged = _merge(branches, w_branch, layer, proj_g, d)
    return _out_proj(merged, w_out, layer, x.reshape(t, d), mod, s, g_idx=2)


def _moe(x2, seq, mod, norm_w, wg, bg, wr, br, w_gate, w_up, w_down, layer, final_w, final):
    t, d = x2.shape
    b = t // seq
    h, route_i, route_w, counts = _router(x2.reshape(b, seq, d), norm_w, mod, wg, bg, wr, br,
                                          sh_idx=3, sc_idx=4)
    route_i = route_i.reshape(t, LANE)
    counts = counts[0, :N_EXPERTS].astype(jnp.int32)
    padded = (counts + EXP_BLOCK - 1) // EXP_BLOCK * EXP_BLOCK
    pad_end = jnp.cumsum(padded)
    pad_start = pad_end - padded
    n_blocks = -(-(t * 2) // EXP_BLOCK) + N_EXPERTS
    dest1 = pad_start[route_i[:, 0]] + route_i[:, 2]
    dest2 = pad_start[route_i[:, 1]] + route_i[:, 3]
    block_row0 = jnp.arange(n_blocks, dtype=jnp.int32) * EXP_BLOCK
    block_expert = jnp.minimum(jnp.sum((pad_end[None, :] <= block_row0[:, None]).astype(jnp.int32), axis=1),
                               N_EXPERTS - 1)
    n_used = (pad_end[-1:] // EXP_BLOCK).astype(jnp.int32)
    first = (block_expert != jnp.concatenate([jnp.full((1,), -1, jnp.int32), block_expert[:-1]])).astype(jnp.int32)
    e_ids = jnp.arange(N_EXPERTS, dtype=jnp.int32)
    later = jnp.where((counts > 0)[None, :] & (e_ids[None, :] > e_ids[:, None]), e_ids[None, :], N_EXPERTS)
    nxt_e = jnp.min(later, axis=1)
    nxt = jnp.where(nxt_e == N_EXPERTS, -1, nxt_e)[block_expert].astype(jnp.int32)
    slot_tok = _slot_tokens(dest1, dest2, n_blocks * EXP_BLOCK)
    ys = _experts(slot_tok, block_expert, n_used, first, nxt, h.reshape(t, d // 2), w_gate, w_up, w_down, layer)
    return _combine(dest1, dest2, ys, x2, route_w.reshape(t, LANE), mod, final_w, seq, g_idx=5, final=final)


def kernel(x, c, positions, ada_w, ada_b, norm_mix, norm_ffn, w_in, conv_w, conv_b, dt_bias, a_log, d_skip,
           ssm_norm, mla_q_norm, mla_wq_b, mla_kv_norm, mla_wkv_b, swa_sinks, w_branch, w_out,
           router_group_w, router_group_b, router_w, router_b, exp_w_gate, exp_w_up, exp_w_down, final_norm):
    b, s, d = x.shape
    depth = ada_w.shape[0]
    mods = _ada_mod(c, ada_w, ada_b)
    pos_f = positions.astype(F32)
    pos_col = pos_f.reshape(b, s, 1)
    pos_row = pos_f.reshape(b, 1, s)
    cos_t, sin_t = _rope_tables(pos_f.reshape(b * s, 1))
    w_in_t = jnp.swapaxes(w_in, 1, 2)
    for l in range(depth):
        x2 = _mixer(x, mods[l], pos_col, pos_row, cos_t, sin_t, norm_mix[l], w_in_t, l, conv_w[l], conv_b[l],
                    dt_bias[l], a_log[l], d_skip[l], ssm_norm[l], mla_q_norm[l], mla_wq_b[l],
                    mla_kv_norm[l], mla_wkv_b[l], swa_sinks[l], w_branch, w_out)
        x2 = _moe(x2, s, mods[l], norm_ffn[l], router_group_w[l], router_group_b[l], router_w[l],
                  router_b[l], exp_w_gate, exp_w_up, exp_w_down, l, final_norm,
                  final=(l == depth - 1))
        x = x2.reshape(b, s, d)
    return x
```

```python
import functools
import math

import numpy as np
import jax
import jax.numpy as jnp
from jax import lax
from jax.experimental import pallas as pl
from jax.experimental.pallas import tpu as pltpu

F32 = jnp.float32
BF16 = jnp.bfloat16
HIGHEST = lax.Precision.HIGHEST

MOBA_HEADS = 8
MOBA_HEAD_DIM = 128
MOBA_BLOCK = 256
MOBA_TOPK = 3
SSM_D_INNER = 1024
SSM_HEAD_DIM = 64
SSM_HEADS = SSM_D_INNER // SSM_HEAD_DIM
SSM_GROUPS = 2
SSM_STATE = 128
SSM_CONV = 4
SSM_CHUNK = 128
SSM_BC = 2 * SSM_GROUPS * SSM_STATE
MLA_HEADS = 8
MLA_Q_LORA = 768
MLA_KV_LORA = 512
MLA_NOPE = 128
MLA_ROPE = 64
MLA_V = 128
ROPE_THETA = 10000.0
SWA_HEADS = 16
SWA_KV_HEADS = 2
SWA_HEAD_DIM = 64
SWA_WINDOW = 128
N_BRANCH = 4
BRANCH_W = 1024
N_GROUPS = 4
EXPERTS_PER_GROUP = 8
N_EXPERTS = N_GROUPS * EXPERTS_PER_GROUP
EXPERT_HIDDEN = 512
EXP_BLOCK = 256
NORM_EPS = 1e-6
N_ALIBI = MOBA_HEADS + SWA_HEADS

LOG2E = math.log2(math.e)
MASKED = -1e30
LANE = 128
VMEM_CAP = 60 * 1024 * 1024
ATTN_TILE = 512
FLASH_HEADS = 4
ONES_ROWS = 16
MLA_QK = 256


def _alibi_slopes():
    i = np.arange(1, N_ALIBI + 1, dtype=np.float64)
    s = np.exp2(-8.0 * i / N_ALIBI).astype(np.float32)
    return s[:SWA_HEADS], s[SWA_HEADS:]


IN_SIZES_HEAD = (3 * MOBA_HEADS * MOBA_HEAD_DIM, SSM_D_INNER, SSM_D_INNER + SSM_BC, SSM_HEADS, MLA_Q_LORA,
                 MLA_KV_LORA, MLA_ROPE, SWA_HEADS * SWA_HEAD_DIM, SWA_KV_HEADS * SWA_HEAD_DIM,
                 SWA_KV_HEADS * SWA_HEAD_DIM)
A_COLS = IN_SIZES_HEAD[0] + IN_SIZES_HEAD[1] + IN_SIZES_HEAD[2]
GATE_COL0 = sum(IN_SIZES_HEAD)
OFF_A = dict(mq=0, mk=1024, mv=2048, z=3072, xs=4096, bc=5120)
OFF_S = dict(sq=0, skx=1024, svx=1536, kvlat=2048, qlat=3072)
S_COLS = 3840


def _params(sem, est_bytes):
    limit = int(min(VMEM_CAP, max(est_bytes, 16 * 1024 * 1024)))
    return pltpu.CompilerParams(dimension_semantics=sem, vmem_limit_bytes=limit)


def _silu(v):
    return v * (1.0 / (1.0 + jnp.exp(-v)))


def _pack_bf16_pair(lo, hi):
    lo_bits = pltpu.bitcast(lo.astype(BF16).astype(F32), jnp.uint32)
    hi_bits = pltpu.bitcast(hi.astype(BF16).astype(F32), jnp.uint32)
    return hi_bits | lax.shift_right_logical(lo_bits, jnp.uint32(16))


def _split3(v):
    hi = v.astype(BF16)
    rest = v - hi.astype(F32)
    mid = rest.astype(BF16)
    lo = (rest - mid.astype(F32)).astype(BF16)
    return hi, mid, lo


def _unpack_bf16_pair(word):
    lo = pltpu.bitcast(lax.shift_left(word, jnp.uint32(16)), F32)
    hi = pltpu.bitcast(word & jnp.uint32(0xFFFF0000), F32)
    return lo, hi


def _ada_kernel(c_ref, w_ref, b_ref, o_ref):
    k = pl.program_id(1)

    @pl.when(k == 0)
    def _():
        o_ref[0] = jnp.broadcast_to(b_ref[0], o_ref.shape[1:])

    o_ref[0] += jnp.dot(c_ref[...].astype(BF16), w_ref[0].astype(BF16), preferred_element_type=F32)


def _ada_mod(c, ada_w, ada_b):
    depth, d, n = ada_w.shape
    b = c.shape[0]
    c8 = jnp.zeros((8, d), F32).at[:b].set(c)
    tk = LANE
    out = pl.pallas_call(
        _ada_kernel,
        grid=(depth, d // tk),
        in_specs=[pl.BlockSpec((8, tk), lambda l, k: (0, k)),
                  pl.BlockSpec((1, tk, n), lambda l, k: (l, k, 0)),
                  pl.BlockSpec((1, 1, n), lambda l, k: (l, 0, 0))],
        out_specs=pl.BlockSpec((1, 8, n), lambda l, k: (l, 0, 0)),
        out_shape=jax.ShapeDtypeStruct((depth, 8, n), F32),
        compiler_params=_params(("parallel", "arbitrary"), 3 * tk * n * 4 + 32 * n * 4 + (4 << 20)),
        name="ada_mod",
    )(c8, ada_w, ada_b.reshape(depth, 1, n))
    return out[:, :b].reshape(depth, b, 6, d)


def _norm_mod_kernel(x_ref, w_ref, mod_ref, o_ref, *, sh_idx, sc_idx):
    x = x_ref[0]
    y = x * lax.rsqrt(jnp.mean(x * x, axis=-1, keepdims=True) + NORM_EPS) * w_ref[...]
    m = mod_ref[0]
    o_ref[0] = (y * (1.0 + m[sc_idx:sc_idx + 1]) + m[sh_idx:sh_idx + 1]).astype(o_ref.dtype)


def _norm_mod(x, w, mod, sh_idx, sc_idx, tm=256):
    b, s, d = x.shape
    return pl.pallas_call(
        functools.partial(_norm_mod_kernel, sh_idx=sh_idx, sc_idx=sc_idx),
        grid=(b, s // tm),
        in_specs=[pl.BlockSpec((1, tm, d), lambda i, j: (i, j, 0)),
                  pl.BlockSpec((1, d), lambda i, j: (0, 0)),
                  pl.BlockSpec((1, 6, d), lambda i, j: (i, 0, 0))],
        out_specs=pl.BlockSpec((1, tm, d), lambda i, j: (i, j, 0)),
        out_shape=jax.ShapeDtypeStruct((b, s, d), BF16),
        compiler_params=_params(("parallel", "parallel"), 6 * tm * d * 4 + (4 << 20)),
        name="norm_mod",
    )(x, w.reshape(1, d), mod)


def _wproj_kernel(a_ref, w_ref, o_ref, wbf_scr):
    tn = wbf_scr.shape[0]
    rows = 64

    @pl.when(pl.program_id(1) == 0)
    def _():
        for r in range(0, tn, rows):
            wbf_scr[r:r + rows, :] = w_ref[0, r:r + rows, :].astype(BF16)

    o_ref[...] = lax.dot_general(a_ref[...], wbf_scr[...], (((1,), (1,)), ((), ())),
                                 preferred_element_type=F32).astype(o_ref.dtype)


def _wproj(a, wt_stack, layer, col0, n_cols, name, tm=1024, tn=512, out_dtype=BF16):
    m, k = a.shape
    assert n_cols % tn == 0 and m % tm == 0 and col0 % 8 == 0
    if col0 % tn == 0:
        w_spec = pl.BlockSpec((1, tn, k), lambda j, i: (layer, col0 // tn + j, 0))
    else:
        w_spec = pl.BlockSpec((pl.Element(1), pl.Element(tn), pl.Element(k)),
                              lambda j, i: (layer, pl.multiple_of(col0 + j * tn, 8), 0))
    est = 2 * (tm * k * 2 + tn * k * 4 + tm * tn * 2) + k * tn * 2 + tm * tn * 4 + (6 << 20)
    return pl.pallas_call(
        _wproj_kernel,
        grid=(n_cols // tn, m // tm),
        in_specs=[pl.BlockSpec((tm, k), lambda j, i: (i, 0)), w_spec],
        out_specs=pl.BlockSpec((tm, tn), lambda j, i: (i, j)),
        out_shape=jax.ShapeDtypeStruct((m, n_cols), out_dtype),
        scratch_shapes=[pltpu.VMEM((tn, k), BF16)],
        compiler_params=_params(("parallel", "arbitrary"), est),
        name=name,
    )(a, wt_stack)


def _rope_kernel(pos_ref, freq_ref, cos_ref, sin_ref):
    ang = pos_ref[...] * freq_ref[...]
    lane = lax.broadcasted_iota(jnp.int32, ang.shape, 1)
    live = lane < MLA_ROPE
    cos_ref[...] = jnp.where(live, jnp.cos(ang), 0.0)
    sin_ref[...] = jnp.where(live, jnp.sin(ang), 0.0)


def _rope_tables(pos_col):
    t = pos_col.shape[0]
    half = MLA_ROPE // 2
    inv = ROPE_THETA ** (-np.arange(half, dtype=np.float32) / half)
    freq = np.zeros((1, LANE), np.float32)
    freq[0, :half] = inv
    freq[0, half:2 * half] = inv
    tm = 512
    return pl.pallas_call(
        _rope_kernel,
        grid=(t // tm,),
        in_specs=[pl.BlockSpec((tm, 1), lambda i: (i, 0)),
                  pl.BlockSpec((1, LANE), lambda i: (0, 0))],
        out_specs=[pl.BlockSpec((tm, LANE), lambda i: (i, 0))] * 2,
        out_shape=[jax.ShapeDtypeStruct((t, LANE), F32)] * 2,
        compiler_params=_params(("parallel",), 16 << 20),
        name="rope_tables",
    )(pos_col, jnp.asarray(freq))


def _mla_q_kernel(x_ref, nw_ref, w_ref, cos_ref, sin_ref, o_ref, *, scale):
    x = x_ref[...].astype(F32)
    y = x * lax.rsqrt(jnp.mean(x * x, axis=-1, keepdims=True) + NORM_EPS) * nw_ref[...]
    r = jnp.dot(y.astype(BF16), w_ref[...], preferred_element_type=F32)
    cos = cos_ref[...]
    sin = sin_ref[...]
    for h in range(MLA_HEADS):
        base = h * 3 * LANE
        nope = r[:, base:base + LANE]
        rope = r[:, base + LANE:base + 2 * LANE]
        rot = r[:, base + 2 * LANE:base + 3 * LANE]
        o_ref[:, h * MLA_QK:h * MLA_QK + LANE] = (nope * scale).astype(o_ref.dtype)
        o_ref[:, h * MLA_QK + LANE:(h + 1) * MLA_QK] = (
            (rope * cos + rot * sin) * scale).astype(o_ref.dtype)


def _mla_kv_kernel(x_ref, nw_ref, w_ref, kr_ref, krot_ref, cos_ref, sin_ref, k_ref, v_ref):
    x = x_ref[...].astype(F32)
    y = x * lax.rsqrt(jnp.mean(x * x, axis=-1, keepdims=True) + NORM_EPS) * nw_ref[...]
    r = jnp.dot(y.astype(BF16), w_ref[...], preferred_element_type=F32)
    kr = (kr_ref[...] * cos_ref[...] + krot_ref[...] * sin_ref[...]).astype(k_ref.dtype)
    for h in range(MLA_HEADS):
        base = h * (MLA_NOPE + MLA_V)
        k_ref[:, h * MLA_QK:h * MLA_QK + LANE] = r[:, base:base + MLA_NOPE].astype(k_ref.dtype)
        k_ref[:, h * MLA_QK + LANE:(h + 1) * MLA_QK] = kr
        v_ref[:, h * MLA_V:(h + 1) * MLA_V] = r[:, base + MLA_NOPE:base + MLA_NOPE + MLA_V].astype(
            v_ref.dtype)


def _mla_project(proj, tail, off, q_norm, wq_ext, kv_norm, wkv, cos_t, sin_t, tm=512):
    t = proj.shape[0]
    scale = float((MLA_NOPE + MLA_ROPE) ** -0.5) * LOG2E
    q = pl.pallas_call(
        functools.partial(_mla_q_kernel, scale=scale),
        grid=(t // tm,),
        in_specs=[pl.BlockSpec((tm, MLA_Q_LORA), lambda i: (i, off["qlat"] // MLA_Q_LORA)),
                  pl.BlockSpec((1, MLA_Q_LORA), lambda i: (0, 0)),
                  pl.BlockSpec(wq_ext.shape, lambda i: (0, 0)),
                  pl.BlockSpec((tm, LANE), lambda i: (i, 0)),
                  pl.BlockSpec((tm, LANE), lambda i: (i, 0))],
        out_specs=pl.BlockSpec((tm, MLA_HEADS * MLA_QK), lambda i: (i, 0)),
        out_shape=jax.ShapeDtypeStruct((t, MLA_HEADS * MLA_QK), BF16),
        compiler_params=_params(("parallel",), 40 << 20),
        name="mla_q_proj",
    )(proj, q_norm.reshape(1, -1), wq_ext, cos_t, sin_t)
    k, v = pl.pallas_call(
        _mla_kv_kernel,
        grid=(t // tm,),
        in_specs=[pl.BlockSpec((tm, MLA_KV_LORA), lambda i: (i, off["kvlat"] // MLA_KV_LORA)),
                  pl.BlockSpec((1, MLA_KV_LORA), lambda i: (0, 0)),
                  pl.BlockSpec(wkv.shape, lambda i: (0, 0)),
                  pl.BlockSpec((tm, LANE), lambda i: (i, 0)),
                  pl.BlockSpec((tm, LANE), lambda i: (i, 1)),
                  pl.BlockSpec((tm, LANE), lambda i: (i, 0)),
                  pl.BlockSpec((tm, LANE), lambda i: (i, 0))],
        out_specs=[pl.BlockSpec((tm, MLA_HEADS * MLA_QK), lambda i: (i, 0)),
                   pl.BlockSpec((tm, MLA_HEADS * MLA_V), lambda i: (i, 0))],
        out_shape=[jax.ShapeDtypeStruct((t, MLA_HEADS * MLA_QK), BF16),
                   jax.ShapeDtypeStruct((t, MLA_HEADS * MLA_V), BF16)],
        compiler_params=_params(("parallel",), 40 << 20),
        name="mla_kv_proj",
    )(proj, kv_norm.reshape(1, -1), wkv, tail, tail, cos_t, sin_t)
    return q, k, v


def _flash_kernel(*refs, moba, q_scale, dq, dv):
    if moba:
        slope_ref, q_ref, k_ref, v_ref, posk_ref, posq_ref, o_ref, vt_scr, kmean_scr, sel_scr = refs
    else:
        q_ref, k_ref, v_ref, o_ref, vt_scr = refs
    tile = ATTN_TILE
    seq = k_ref.shape[1]
    sub = tile // MOBA_BLOCK
    nblk = seq // MOBA_BLOCK
    qi = pl.program_id(2)
    heads = range(FLASH_HEADS)

    @pl.when(qi == 0)
    def _():
        for j in heads:
            for c in range(seq // tile):
                vt_scr[j, c, 0:dv, :] = (
                    v_ref[0, c * tile:(c + 1) * tile, j * dv:(j + 1) * dv].astype(F32).T.astype(BF16))
                vt_scr[j, c, dv:dv + ONES_ROWS, :] = jnp.ones((ONES_ROWS, tile), BF16)
            if moba:
                for n in range(nblk):
                    kblk = k_ref[0, n * MOBA_BLOCK:(n + 1) * MOBA_BLOCK, j * dq:(j + 1) * dq].astype(F32)
                    kmean_scr[j, n:n + 1, :] = jnp.mean(kblk, axis=0, keepdims=True)

    q = []
    for j in heads:
        q_raw = q_ref[0, :, j * dq:(j + 1) * dq]
        q.append(q_raw if q_scale is None else (q_raw.astype(F32) * q_scale).astype(BF16))
        if moba:
            gate = lax.dot_general(kmean_scr[j], q_raw.astype(F32), (((1,), (1,)), ((), ())),
                                   precision=HIGHEST, preferred_element_type=F32)
            n_iota = lax.broadcasted_iota(jnp.int32, gate.shape, 0)
            q_blk = qi * sub + lax.broadcasted_iota(jnp.int32, gate.shape, 1) // MOBA_BLOCK
            beaten = jnp.zeros(gate.shape, F32)
            for m in range(nblk):
                gm = gate[m:m + 1, :]
                wins = jnp.where(gm > gate, 1.0, jnp.where(gm == gate, jnp.where(m < n_iota, 1.0, 0.0), 0.0))
                beaten = beaten + jnp.where(m < q_blk, wins, 0.0)
            sel_scr[j] = jnp.where(beaten < MOBA_TOPK, jnp.where(n_iota < q_blk, 0.0, MASKED), MASKED)
    if moba:
        slope = [slope_ref[pl.program_id(1) * FLASH_HEADS + j] for j in heads]
        pq = [posq_ref[0] * slope[j] for j in heads]

    def scores(j, c):
        start = pl.multiple_of(c * tile, tile)
        kb = k_ref[0, pl.ds(start, tile), j * dq:(j + 1) * dq]
        st = lax.dot_general(kb, q[j], (((1,), (1,)), ((), ())), preferred_element_type=F32)
        if moba:
            pk = posk_ref[0, pl.ds(start, tile), :] * slope[j]
            st = st - jnp.abs(pq[j] - pk)
        return st

    def block_bias(j, c):
        rows = [jnp.broadcast_to(sel_scr[j, pl.ds(c * sub + r, 1), :], (MOBA_BLOCK, tile)) for r in range(sub)]
        return jnp.concatenate(rows, axis=0)

    key_i = lax.broadcasted_iota(jnp.int32, (tile, tile), 0)
    qry_i = lax.broadcasted_iota(jnp.int32, (tile, tile), 1)
    causal_bias = jnp.where(key_i <= qry_i, 0.0, MASKED)
    init = []
    for j in heads:
        st = scores(j, qi)
        if moba:
            same_blk = (key_i // MOBA_BLOCK) == (qry_i // MOBA_BLOCK)
            st = st + jnp.where(same_blk, causal_bias, block_bias(j, qi))
        else:
            st = st + causal_bias
        m0 = jnp.max(st, axis=0, keepdims=True)
        p = jnp.exp2(st - m0)
        init.append((m0, jnp.dot(vt_scr[j, qi], p.astype(BF16), preferred_element_type=F32)))

    def body(c, carry):
        s_c = [scores(j, c) for j in heads]
        if moba:
            s_c = [s_c[j] + block_bias(j, c) for j in heads]
        m_new = [jnp.maximum(carry[j][0], jnp.max(s_c[j], axis=0, keepdims=True)) for j in heads]
        p_c = [jnp.exp2(s_c[j] - m_new[j]).astype(BF16) for j in heads]
        out = []
        for j in heads:
            m_prev, acc = carry[j]
            alpha = jnp.exp2(m_prev - m_new[j])
            acc_new = alpha * acc + jnp.dot(vt_scr[j, c], p_c[j], preferred_element_type=F32)
            out.append((m_new[j], acc_new))
        return tuple(out)

    final = lax.fori_loop(0, qi, body, tuple(init))
    for j in heads:
        _, acc = final[j]
        o_ref[0, :, j * dv:(j + 1) * dv] = (acc[0:dv] * (1.0 / acc[dv:dv + 1])).T.astype(o_ref.dtype)


def _flash(q, k, v, *, heads, dq, dv, q_off, k_off, v_off, moba=False, q_scale=None,
           slopes=None, pos_col=None, pos_row=None, name="flash"):
    b, s, _ = q.shape
    tile = ATTN_TILE
    hp = FLASH_HEADS
    assert heads % hp == 0 and q_off % (hp * dq) == 0 and k_off % (hp * dq) == 0 and v_off % (hp * dv) == 0
    qb, kb, vb = q_off // (hp * dq), k_off // (hp * dq), v_off // (hp * dv)
    in_specs = [pl.BlockSpec((1, tile, hp * dq), lambda bi, h, i: (bi, i, qb + h)),
                pl.BlockSpec((1, s, hp * dq), lambda bi, h, i: (bi, 0, kb + h)),
                pl.BlockSpec((1, s, hp * dv), lambda bi, h, i: (bi, 0, vb + h))]
    args = [q, k, v]
    scratch = [pltpu.VMEM((hp, s // tile, dv + ONES_ROWS, tile), BF16)]
    if moba:
        in_specs = [pl.BlockSpec(memory_space=pltpu.SMEM)] + in_specs + [
            pl.BlockSpec((1, s, 1), lambda bi, h, i: (bi, 0, 0)),
            pl.BlockSpec((1, 1, tile), lambda bi, h, i: (bi, 0, i))]
        args = [slopes] + args + [pos_col, pos_row]
        scratch += [pltpu.VMEM((hp, s // MOBA_BLOCK, dq), F32), pltpu.VMEM((hp, s // MOBA_BLOCK, tile), F32)]
    est = hp * (2 * (s * dq * 2 + s * dv * 2) + s * dv * 2) + s * LANE * 4 * 2 + (20 << 20)
    return pl.pallas_call(
        functools.partial(_flash_kernel, moba=moba, q_scale=q_scale, dq=dq, dv=dv),
        grid=(b, heads // hp, s // tile),
        in_specs=in_specs,
        out_specs=pl.BlockSpec((1, tile, hp * dv), lambda bi, h, i: (bi, i, h)),
        out_shape=jax.ShapeDtypeStruct((b, s, heads * dv), BF16),
        scratch_shapes=scratch,
        compiler_params=_params(("parallel", "parallel", "arbitrary"), est),
        name=name,
    )(*args)


def _swa_kernel(sink_ref, q_ref, kp_ref, kc_ref, vp_ref, vc_ref, pq_ref, pkp_ref, pkc_ref, o_ref,
                *, slopes):
    w = SWA_WINDOW
    n = pl.program_id(1)
    kx = jnp.concatenate([kp_ref[0], kc_ref[0]], axis=0)
    vx = jnp.concatenate([vp_ref[0], vc_ref[0]], axis=0)
    pk = jnp.concatenate([pkp_ref[0], pkc_ref[0]], axis=1)
    dist = jnp.abs(pq_ref[0] - pk)
    qi = lax.broadcasted_iota(jnp.int32, dist.shape, 0)
    kk = lax.broadcasted_iota(jnp.int32, dist.shape, 1)
    allowed = jnp.where(kk > qi, jnp.where(kk <= qi + w, 0.0, MASKED), MASKED)
    mask_bias = jnp.where(kk >= w, allowed, jnp.where(n > 0, allowed, MASKED))
    scale = float(SWA_HEAD_DIM ** -0.5)
    rep = SWA_HEADS // SWA_KV_HEADS
    heads = range(SWA_HEADS)
    col = [(2 * (h // rep) + h % 2) * LANE for h in heads]
    s = [lax.dot_general(q_ref[0, :, (h // 2) * LANE:(h // 2 + 1) * LANE], kx[:, col[h]:col[h] + LANE],
                         (((1,), (1,)), ((), ())), preferred_element_type=F32) for h in heads]
    s = [s[h] * scale + (mask_bias - float(slopes[h]) * dist) for h in heads]
    m = [jnp.maximum(jnp.max(s[h], axis=1, keepdims=True), sink_ref[h]) for h in heads]
    p = [jnp.exp(s[h] - m[h]) for h in heads]
    inv = [1.0 / (jnp.sum(p[h], axis=1, keepdims=True) + jnp.exp(sink_ref[h] - m[h])) for h in heads]
    p = [(p[h] * inv[h]).astype(BF16) for h in heads]
    for pair in range(SWA_HEADS // 2):
        h0, h1 = 2 * pair, 2 * pair + 1
        acc = (jnp.dot(p[h0], vx[:, col[h0]:col[h0] + LANE], preferred_element_type=F32)
               + jnp.dot(p[h1], vx[:, col[h1]:col[h1] + LANE], preferred_element_type=F32))
        o_ref[0, :, pair * LANE:(pair + 1) * LANE] = acc.astype(o_ref.dtype)


def _swa(proj3, off, sinks, pos_col, pos_row, slopes):
    b, s, _ = proj3.shape
    w = SWA_WINDOW
    qw = SWA_HEADS * SWA_HEAD_DIM
    kw = 4 * LANE
    prev = lambda j: jnp.maximum(j - 1, 0)
    return pl.pallas_call(
        functools.partial(_swa_kernel, slopes=tuple(float(v) for v in slopes)),
        grid=(b, s // w),
        in_specs=[pl.BlockSpec(memory_space=pltpu.SMEM),
                  pl.BlockSpec((1, w, qw), lambda i, j: (i, j, off["sq"] // qw)),
                  pl.BlockSpec((1, w, kw), lambda i, j: (i, prev(j), off["skx"] // kw)),
                  pl.BlockSpec((1, w, kw), lambda i, j: (i, j, off["skx"] // kw)),
                  pl.BlockSpec((1, w, kw), lambda i, j: (i, prev(j), off["svx"] // kw)),
                  pl.BlockSpec((1, w, kw), lambda i, j: (i, j, off["svx"] // kw)),
                  pl.BlockSpec((1, w, 1), lambda i, j: (i, j, 0)),
                  pl.BlockSpec((1, 1, w), lambda i, j: (i, 0, prev(j))),
                  pl.BlockSpec((1, 1, w), lambda i, j: (i, 0, j))],
        out_specs=pl.BlockSpec((1, w, qw), lambda i, j: (i, j, 0)),
        out_shape=jax.ShapeDtypeStruct((b, s, qw), BF16),
        compiler_params=_params(("parallel", "parallel"), 24 << 20),
        name="swa_attention",
    )(sinks, proj3, proj3, proj3, proj3, proj3, pos_col, pos_row, pos_row)


def _ssd_kernel(xs_ref, bc_ref, z_ref, dt_ref, cwx_ref, cwb_ref, cbx_ref, cbb_ref, dtb_ref,
                alog_ref, dsk_ref, nw_ref, exp_ref, o_ref, padx_scr, padb_scr, st_scr):
    ch = SSM_CHUNK
    c = pl.program_id(1)

    @pl.when(c == 0)
    def _():
        padx_scr[0:8, :] = jnp.zeros((8, padx_scr.shape[1]), F32)
        padb_scr[0:8, :] = jnp.zeros((8, padb_scr.shape[1]), F32)
        st_scr[...] = jnp.zeros(st_scr.shape, F32)

    padx_scr[8:8 + ch, :] = xs_ref[0].astype(F32)
    padb_scr[8:8 + ch, :] = bc_ref[0].astype(F32)

    def conv(pad_scr, w_ref, b_ref):
        acc = b_ref[...] + w_ref[0:1, :] * pad_scr[5:5 + ch, :]
        for k in range(1, SSM_CONV):
            acc = acc + w_ref[k:k + 1, :] * pad_scr[5 + k:5 + k + ch, :]
        return _silu(acc)

    xs = conv(padx_scr, cwx_ref, cbx_ref)
    bcv = conv(padb_scr, cwb_ref, cbb_ref)
    padx_scr[0:8, :] = padx_scr[ch:ch + 8, :]
    padb_scr[0:8, :] = padb_scr[ch:ch + 8, :]

    gn = SSM_GROUPS * SSM_STATE
    dtr = dt_ref[0] + dtb_ref[...]
    dt = jnp.maximum(dtr, 0.0) + jnp.log(1.0 + jnp.exp(-jnp.abs(dtr)))
    a = dt * (-jnp.exp(alog_ref[...]))
    expand = exp_ref[...]
    row = lax.broadcasted_iota(jnp.int32, (ch, ch), 0)
    colm = lax.broadcasted_iota(jnp.int32, (ch, ch), 1)
    tril = row >= colm
    tri = jnp.where(tril, 1.0, 0.0).astype(BF16)

    def times_select(v, sel):
        return sum(jnp.dot(part, sel, preferred_element_type=F32) for part in _split3(v))

    dt_e = times_select(dt, expand)
    acs = sum(jnp.dot(tri, part, preferred_element_type=F32) for part in _split3(a))
    acs_e = times_select(acs, expand)
    acs_t = acs.T
    x_dt = xs * dt_e
    last = acs_e[ch - 1:ch, :]
    x_dec = (x_dt * jnp.exp(last - acs_e)).astype(BF16)
    lane = lax.broadcasted_iota(jnp.int32, (ch, LANE), 1)
    lo = lane < SSM_HEAD_DIM
    half_w = SSM_D_INNER // SSM_GROUPS
    heads_per_group = SSM_HEADS // SSM_GROUPS
    groups = range(SSM_GROUPS)
    heads = range(SSM_HEADS)
    bg = [bcv[:, g * SSM_STATE:(g + 1) * SSM_STATE] for g in groups]
    cg = [bcv[:, gn + g * SSM_STATE:gn + (g + 1) * SSM_STATE].astype(BF16) for g in groups]
    gmat = [lax.dot_general(cg[g], bg[g].astype(BF16), (((1,), (1,)), ((), ())),
                            preferred_element_type=F32) for g in groups]
    st_in = [st_scr[:, g * half_w:(g + 1) * half_w] for g in groups]
    yoff = [jnp.dot(cg[g], st_in[g].astype(BF16), preferred_element_type=F32) for g in groups]
    upd = [jnp.dot(bg[g].T.astype(BF16), x_dec[:, g * half_w:(g + 1) * half_w],
                   preferred_element_type=F32) for g in groups]
    seg = [jnp.exp(jnp.where(tril, acs[:, h:h + 1] - acs_t[h:h + 1, :], -jnp.inf)) for h in heads]
    mmat = [(gmat[h // heads_per_group] * seg[h]).astype(BF16) for h in heads]
    xh = []
    for h in heads:
        xp = x_dt[:, (h // 2) * LANE:(h // 2 + 1) * LANE]
        xh.append((jnp.where(lo, xp, 0.0) if h % 2 == 0 else jnp.where(lo, 0.0, xp)).astype(BF16))
    ydiag = [jnp.dot(mmat[2 * pr], xh[2 * pr], preferred_element_type=F32)
             + jnp.dot(mmat[2 * pr + 1], xh[2 * pr + 1], preferred_element_type=F32)
             for pr in range(SSM_HEADS // 2)]
    for g in groups:
        st_scr[:, g * half_w:(g + 1) * half_w] = (
            st_in[g] * jnp.exp(last[:, g * half_w:(g + 1) * half_w]) + upd[g])
    y = (jnp.concatenate(ydiag, axis=1) + jnp.concatenate(yoff, axis=1) * jnp.exp(acs_e)
         + dsk_ref[...] * xs)
    gz = y * _silu(z_ref[0].astype(F32))
    outs = []
    for g in range(SSM_GROUPS):
        gg = gz[:, g * half_w:(g + 1) * half_w]
        outs.append(gg * lax.rsqrt(jnp.mean(gg * gg, axis=-1, keepdims=True) + NORM_EPS))
    o_ref[0] = (jnp.concatenate(outs, axis=1) * nw_ref[...]).astype(o_ref.dtype)


def _ssd(proj3, tail3, off, conv_w, conv_b, dt_bias, a_log, d_skip, norm_w):
    b, s, _ = proj3.shape
    ch = SSM_CHUNK
    di = SSM_D_INNER
    pad16 = lambda v: jnp.zeros((1, LANE), F32).at[0, :SSM_HEADS].set(v)
    expand = np.zeros((LANE, di), np.float32)
    for h in range(SSM_HEADS):
        expand[h, h * SSM_HEAD_DIM:(h + 1) * SSM_HEAD_DIM] = 1.0
    full = lambda shape: pl.BlockSpec(shape, lambda i, j: (0,) * len(shape))
    return pl.pallas_call(
        _ssd_kernel,
        grid=(b, s // ch),
        in_specs=[pl.BlockSpec((1, ch, di), lambda i, j: (i, j, off["xs"] // di)),
                  pl.BlockSpec((1, ch, SSM_BC), lambda i, j: (i, j, off["bc"] // SSM_BC)),
                  pl.BlockSpec((1, ch, di), lambda i, j: (i, j, off["z"] // di)),
                  pl.BlockSpec((1, ch, LANE), lambda i, j: (i, j, 2)),
                  full((SSM_CONV, di)), full((SSM_CONV, SSM_BC)), full((1, di)), full((1, SSM_BC)),
                  full((1, LANE)), full((1, LANE)), full((1, di)), full((1, di)), full((LANE, di))],
        out_specs=pl.BlockSpec((1, ch, di), lambda i, j: (i, j, 0)),
        out_shape=jax.ShapeDtypeStruct((b, s, di), BF16),
        scratch_shapes=[pltpu.VMEM((ch + 8, di), F32), pltpu.VMEM((ch + 8, SSM_BC), F32),
                        pltpu.VMEM((SSM_STATE, di), F32)],
        compiler_params=_params(("parallel", "arbitrary"), 32 << 20),
        name="ssd_mixer",
    )(proj3, proj3, proj3, tail3, conv_w[:, :di], conv_w[:, di:], conv_b[:di].reshape(1, di),
      conv_b[di:].reshape(1, SSM_BC), pad16(dt_bias), pad16(a_log),
      jnp.repeat(d_skip, SSM_HEAD_DIM).reshape(1, di), norm_w.reshape(1, di), jnp.asarray(expand, BF16))


def _merge_kernel(o0_ref, o1_ref, o2_ref, o3_ref, w_ref, g0_ref, g1_ref, g2_ref, g3_ref, out_ref, wbf_scr):
    @pl.when(pl.program_id(1) == 0)
    def _():
        for r in range(N_BRANCH):
            wbf_scr[r] = w_ref[0, r].astype(BF16)

    acc = None
    for r, (o_ref, g_ref) in enumerate(((o0_ref, g0_ref), (o1_ref, g1_ref), (o2_ref, g2_ref),
                                        (o3_ref, g3_ref))):
        y = jnp.dot(o_ref[...], wbf_scr[r], preferred_element_type=F32)
        gate = 1.0 / (1.0 + jnp.exp(-g_ref[...].astype(F32)))
        acc = gate * y if acc is None else acc + gate * y
    out_ref[...] = acc.astype(out_ref.dtype)


def _merge(branches, w_branch, layer, proj, d, tm=1024, tn=512):
    t = proj.shape[0]
    nj = d // tn
    o_spec = pl.BlockSpec((tm, BRANCH_W), lambda j, i: (i, 0))
    g_specs = [pl.BlockSpec((tm, tn), functools.partial(lambda j, i, r: (i, r * nj + j), r=r))
               for r in range(N_BRANCH)]
    est = (2 * (4 * tm * BRANCH_W * 2 + 4 * BRANCH_W * tn * 4 + 4 * tm * tn * 2 + tm * tn * 2)
           + 4 * BRANCH_W * tn * 2 + 3 * tm * tn * 4 + (6 << 20))
    return pl.pallas_call(
        _merge_kernel,
        grid=(nj, t // tm),
        in_specs=[o_spec] * 4 + [pl.BlockSpec((1, N_BRANCH, BRANCH_W, tn), lambda j, i: (layer, 0, 0, j))]
        + g_specs,
        out_specs=pl.BlockSpec((tm, tn), lambda j, i: (i, j)),
        out_shape=jax.ShapeDtypeStruct((t, d), BF16),
        scratch_shapes=[pltpu.VMEM((N_BRANCH, BRANCH_W, tn), BF16)],
        compiler_params=_params(("parallel", "arbitrary"), est),
        name="branch_merge",
    )(*branches, w_branch, proj, proj, proj, proj)


def _out_proj_kernel(a_ref, w_ref, x_ref, mod_ref, o_ref, wbf_scr, *, g_idx):
    @pl.when(pl.program_id(1) == 0)
    def _():
        rows = 512
        for r in range(0, wbf_scr.shape[0], rows):
            wbf_scr[r:r + rows, :] = w_ref[0, r:r + rows, :].astype(BF16)

    y = jnp.dot(a_ref[...], wbf_scr[...], preferred_element_type=F32)
    o_ref[...] = x_ref[...] + mod_ref[0, g_idx:g_idx + 1, :] * y


def _out_proj(merged, w_out, layer, x2, mod, seq, g_idx, tm=1024, tn=512):
    t, d = x2.shape
    per_b = seq // tm
    est = 2 * (tm * d * 2 + d * tn * 4 + 2 * tm * tn * 4) + d * tn * 2 + tm * tn * 4 + (4 << 20)
    return pl.pallas_call(
        functools.partial(_out_proj_kernel, g_idx=g_idx),
        grid=(d // tn, t // tm),
        in_specs=[pl.BlockSpec((tm, d), lambda j, i: (i, 0)),
                  pl.BlockSpec((1, d, tn), lambda j, i: (layer, 0, j)),
                  pl.BlockSpec((tm, tn), lambda j, i: (i, j)),
                  pl.BlockSpec((1, 6, tn), lambda j, i: (i // per_b, 0, j))],
        out_specs=pl.BlockSpec((tm, tn), lambda j, i: (i, j)),
        out_shape=jax.ShapeDtypeStruct((t, d), F32),
        scratch_shapes=[pltpu.VMEM((d, tn), BF16)],
        compiler_params=_params(("parallel", "arbitrary"), est),
        name="out_proj",
    )(merged, w_out, x2, mod)


def _router_kernel(x_ref, w_ref, mod_ref, rwh_ref, rwl_ref, rb_ref, h_ref, ri_ref, rwt_ref, cnt_ref, carry_scr,
                   *, sh_idx, sc_idx):
    step = pl.program_id(0) * pl.num_programs(1) + pl.program_id(1)

    @pl.when(step == 0)
    def _():
        carry_scr[...] = jnp.zeros(carry_scr.shape, F32)

    x = x_ref[0]
    y = x * lax.rsqrt(jnp.mean(x * x, axis=-1, keepdims=True) + NORM_EPS) * w_ref[...]
    m = mod_ref[0]
    h = y * (1.0 + m[sc_idx:sc_idx + 1]) + m[sh_idx:sh_idx + 1]
    for g in range(h.shape[1] // (2 * LANE)):
        h_ref[0, :, g * LANE:(g + 1) * LANE] = _pack_bf16_pair(h[:, 2 * g * LANE:(2 * g + 1) * LANE],
                                                               h[:, (2 * g + 1) * LANE:(2 * g + 2) * LANE])
    h_hi = h.astype(BF16)
    h_lo = (h - h_hi.astype(F32)).astype(BF16)
    nt = (((1,), (1,)), ((), ()))
    logits = (lax.dot_general(h_hi, rwh_ref[...], nt, preferred_element_type=F32)
              + lax.dot_general(h_hi, rwl_ref[...], nt, preferred_element_type=F32)
              + lax.dot_general(h_lo, rwh_ref[...], nt, preferred_element_type=F32)) + rb_ref[...]
    tm = logits.shape[0]
    lane = lax.broadcasted_iota(jnp.int32, logits.shape, 1)
    big = jnp.int32(4 * LANE)
    neg = -jnp.inf

    def first_argmax(vals):
        mx = jnp.max(vals, axis=1, keepdims=True)
        idx = jnp.min(jnp.where(vals == mx, lane, big), axis=1, keepdims=True)
        return mx, idx

    lg = jnp.where(lane < N_GROUPS, logits, neg)
    gmax, gidx = first_argmax(lg)
    g_w = 1.0 / jnp.sum(jnp.exp(lg - gmax), axis=1, keepdims=True)
    lo = N_GROUPS + gidx * EXPERTS_PER_GROUP
    in_group = jnp.where(lane >= lo, jnp.where(lane < lo + EXPERTS_PER_GROUP, 1.0, 0.0), 0.0) > 0.0
    le = jnp.where(in_group, logits, neg)
    m1, i1 = first_argmax(le)
    le2 = jnp.where(lane == i1, neg, le)
    m2, i2 = first_argmax(le2)
    denom = jnp.sum(jnp.exp(le - m1), axis=1, keepdims=True)
    p1 = 1.0 / denom
    p2 = jnp.exp(m2 - m1) / denom
    w1 = p1 / (p1 + p2) * g_w
    w2 = p2 / (p1 + p2) * g_w
    e1 = i1 - N_GROUPS
    e2 = i2 - N_GROUPS

    oh1 = jnp.where(lane == e1, 1.0, 0.0)
    oh2 = jnp.where(lane == e2, 1.0, 0.0)
    both = oh1 + oh2
    r_i = lax.broadcasted_iota(jnp.int32, (tm, tm), 0)
    c_i = lax.broadcasted_iota(jnp.int32, (tm, tm), 1)
    strict = jnp.where(r_i > c_i, 1.0, 0.0).astype(BF16)
    before = jnp.dot(strict, both.astype(BF16), preferred_element_type=F32) + carry_scr[...]
    rank1 = jnp.sum(before * oh1, axis=1, keepdims=True).astype(jnp.int32)
    rank2 = jnp.sum(before * oh2, axis=1, keepdims=True).astype(jnp.int32)
    carry_scr[...] = carry_scr[...] + jnp.sum(both, axis=0, keepdims=True)
    cnt_ref[...] = carry_scr[...]
    ri_ref[0] = jnp.where(lane == 0, e1, jnp.where(lane == 1, e2, jnp.where(
        lane == 2, rank1, jnp.where(lane == 3, rank2, 0))))
    rwt_ref[0] = jnp.where(lane == 0, w1, jnp.where(lane == 1, w2, 0.0))


def _router(x, w, mod, wg, bg, wr, br, sh_idx, sc_idx, tm=256):
    b, s, d = x.shape
    rw = jnp.concatenate([wg.T, wr.T, jnp.zeros((LANE - N_GROUPS - N_EXPERTS, d), F32)], axis=0)
    rb = jnp.zeros((1, LANE), F32).at[0, :N_GROUPS].set(bg).at[0, N_GROUPS:N_GROUPS + N_EXPERTS].set(br)
    rw_hi = rw.astype(BF16)
    rw_lo = (rw - rw_hi.astype(F32)).astype(BF16)
    return pl.pallas_call(
        functools.partial(_router_kernel, sh_idx=sh_idx, sc_idx=sc_idx),
        grid=(b, s // tm),
        in_specs=[pl.BlockSpec((1, tm, d), lambda i, j: (i, j, 0)),
                  pl.BlockSpec((1, d), lambda i, j: (0, 0)),
                  pl.BlockSpec((1, 6, d), lambda i, j: (i, 0, 0)),
                  pl.BlockSpec((LANE, d), lambda i, j: (0, 0)),
                  pl.BlockSpec((LANE, d), lambda i, j: (0, 0)),
                  pl.BlockSpec((1, LANE), lambda i, j: (0, 0))],
        out_specs=[pl.BlockSpec((1, tm, d // 2), lambda i, j: (i, j, 0)),
                   pl.BlockSpec((1, tm, LANE), lambda i, j: (i, j, 0)),
                   pl.BlockSpec((1, tm, LANE), lambda i, j: (i, j, 0)),
                   pl.BlockSpec((1, LANE), lambda i, j: (0, 0))],
        out_shape=[jax.ShapeDtypeStruct((b, s, d // 2), jnp.uint32),
                   jax.ShapeDtypeStruct((b, s, LANE), jnp.int32),
                   jax.ShapeDtypeStruct((b, s, LANE), F32),
                   jax.ShapeDtypeStruct((1, LANE), F32)],
        scratch_shapes=[pltpu.VMEM((1, LANE), F32)],
        compiler_params=_params(("arbitrary", "arbitrary"), 8 * tm * d * 4 + d * LANE * 8 + (8 << 20)),
        name="moe_router",
    )(x, w.reshape(1, d), mod, rw_hi, rw_lo, rb)


def _slot_kernel(e1_ref, e2_ref, r1_ref, r2_ref, start_ref, tok_ref, d1_ref, d2_ref):
    def clear(i, carry):
        tok_ref[i] = 0
        return carry

    lax.fori_loop(0, tok_ref.shape[0], clear, 0, unroll=16)

    def place(t, carry):
        a = start_ref[e1_ref[t]] + r1_ref[t]
        b = start_ref[e2_ref[t]] + r2_ref[t]
        d1_ref[t] = a
        d2_ref[t] = b
        tok_ref[a] = t
        tok_ref[b] = t
        return carry

    lax.fori_loop(0, e1_ref.shape[0], place, 0, unroll=8)


def _slot_tokens(e1, e2, r1, r2, pad_start, cap):
    smem = pl.BlockSpec(memory_space=pltpu.SMEM)
    t = e1.shape[0]
    return pl.pallas_call(
        _slot_kernel,
        in_specs=[smem] * 5,
        out_specs=[smem] * 3,
        out_shape=[jax.ShapeDtypeStruct((cap,), jnp.int32), jax.ShapeDtypeStruct((t,), jnp.int32),
                   jax.ShapeDtypeStruct((t,), jnp.int32)],
        name="moe_slot_tokens",
    )(e1, e2, r1, r2, pad_start)


def _expert_kernel(tok_ref, be_ref, nu_ref, first_ref, nxt_ref, h_hbm, wg_hbm, wu_hbm, wd_hbm, o_ref,
                   x_scr, xb_scr, sg_scr, su_scr, sd_scr, wg_scr, wu_scr, wd_scr, sem_x, sem_w, *, layer):
    blk = pl.program_id(0)
    n_used = nu_ref[0]
    rows = x_scr.shape[1]
    slot = lax.rem(blk, 2)

    def row_copy(b, s, i):
        tok = tok_ref[b * rows + i]
        return pltpu.make_async_copy(h_hbm.at[pl.ds(tok, 1)], x_scr.at[s, pl.ds(i, 1)], sem_x.at[s])

    def start_rows(b, s):
        def go(i, carry):
            row_copy(b, s, i).start()
            return carry

        lax.fori_loop(0, rows, go, 0, unroll=8)

    def weight_copies(e):
        return (pltpu.make_async_copy(wg_hbm.at[layer, e], sg_scr, sem_w.at[0]),
                pltpu.make_async_copy(wu_hbm.at[layer, e], su_scr, sem_w.at[1]),
                pltpu.make_async_copy(wd_hbm.at[layer, e], sd_scr, sem_w.at[2]))

    @pl.when(blk < n_used)
    def _():
        @pl.when(blk == 0)
        def _():
            start_rows(0, 0)
            for cp in weight_copies(be_ref[0]):
                cp.start()

        @pl.when(blk + 1 < n_used)
        def _():
            start_rows(blk + 1, 1 - slot)

        @pl.when(first_ref[blk] == 1)
        def _():
            for cp in weight_copies(be_ref[blk]):
                cp.wait()
            step = 512
            for r in range(0, sg_scr.shape[0], step):
                wg_scr[r:r + step, :] = sg_scr[r:r + step, :].astype(BF16)
                wu_scr[r:r + step, :] = su_scr[r:r + step, :].astype(BF16)
            for r in range(0, sd_scr.shape[1], step * 4):
                wd_scr[:, r:r + step * 4] = sd_scr[:, r:r + step * 4].astype(BF16)

            @pl.when(nxt_ref[blk] >= 0)
            def _():
                for cp in weight_copies(nxt_ref[blk]):
                    cp.start()

        def wait(i, carry):
            row_copy(blk, slot, i).wait()
            return carry

        lax.fori_loop(0, rows, wait, 0, unroll=8)
        groups = xb_scr.shape[1] // (2 * LANE)
        for g in range(groups):
            lo, hi = _unpack_bf16_pair(x_scr[slot, :, g * LANE:(g + 1) * LANE])
            xb_scr[:, 2 * g * LANE:(2 * g + 1) * LANE] = lo.astype(BF16)
            xb_scr[:, (2 * g + 1) * LANE:(2 * g + 2) * LANE] = hi.astype(BF16)
        xb = xb_scr[...]
        gate = jnp.dot(xb, wg_scr[...], preferred_element_type=F32)
        up = jnp.dot(xb, wu_scr[...], preferred_element_type=F32)
        act = (_silu(gate) * up).astype(BF16)
        for g in range(groups):
            y = jnp.dot(act, wd_scr[:, 2 * g * LANE:(2 * g + 2) * LANE], preferred_element_type=F32)
            o_ref[:, g * LANE:(g + 1) * LANE] = _pack_bf16_pair(y[:, :LANE], y[:, LANE:])

    @pl.when(blk >= n_used)
    def _():
        o_ref[...] = jnp.zeros(o_ref.shape, o_ref.dtype)


def _experts(slot_tok, block_expert, n_used, first, nxt, h2, wg_stack, wu_stack, wd_stack, layer):
    t, dw = h2.shape
    d = 2 * dw
    n_blocks = block_expert.shape[0]
    hid = wg_stack.shape[3]
    est = 3 * d * hid * 6 + EXP_BLOCK * d * (2 * 2 + 2 + 2 * 2 + 8) + (6 << 20)
    any_spec = pl.BlockSpec(memory_space=pl.ANY)
    return pl.pallas_call(
        functools.partial(_expert_kernel, layer=layer),
        grid_spec=pltpu.PrefetchScalarGridSpec(
            num_scalar_prefetch=5,
            grid=(n_blocks,),
            in_specs=[any_spec, any_spec, any_spec, any_spec],
            out_specs=pl.BlockSpec((EXP_BLOCK, dw), lambda i, *_: (i, 0)),
            scratch_shapes=[pltpu.VMEM((2, EXP_BLOCK, dw), jnp.uint32), pltpu.VMEM((EXP_BLOCK, d), BF16),
                            pltpu.VMEM((d, hid), F32), pltpu.VMEM((d, hid), F32), pltpu.VMEM((hid, d), F32),
                            pltpu.VMEM((d, hid), BF16), pltpu.VMEM((d, hid), BF16), pltpu.VMEM((hid, d), BF16),
                            pltpu.SemaphoreType.DMA((2,)), pltpu.SemaphoreType.DMA((3,))]),
        out_shape=jax.ShapeDtypeStruct((n_blocks * EXP_BLOCK, dw), jnp.uint32),
        compiler_params=_params(("arbitrary",), est),
        name="moe_experts",
    )(slot_tok, block_expert, n_used, first, nxt, h2, wg_stack, wu_stack, wd_stack)


def _combine_kernel(d1_ref, d2_ref, ys_hbm, x_ref, rw_ref, mod_ref, fw_ref, o_ref, a_scr, b_scr, sem,
                    *, g_idx, final):
    i = pl.program_id(0)
    rows = a_scr.shape[1]
    slot = lax.rem(i, 2)

    def copies(step, s, r):
        t = step * rows + r
        return (pltpu.make_async_copy(ys_hbm.at[pl.ds(d1_ref[t], 1)], a_scr.at[s, pl.ds(r, 1)], sem.at[0, s]),
                pltpu.make_async_copy(ys_hbm.at[pl.ds(d2_ref[t], 1)], b_scr.at[s, pl.ds(r, 1)], sem.at[1, s]))

    def start_rows(step, s):
        def go(r, carry):
            ca, cb = copies(step, s, r)
            ca.start()
            cb.start()
            return carry

        lax.fori_loop(0, rows, go, 0, unroll=8)

    @pl.when(i == 0)
    def _():
        start_rows(0, 0)

    def wait_rows(s):
        def wait(r, carry):
            ca, cb = copies(i, s, r)
            ca.wait()
            cb.wait()
            return carry

        lax.fori_loop(0, rows, wait, 0, unroll=8)

    wait_rows(slot)
    last = pl.num_programs(0) - 1
    nxt = jnp.minimum(i + 1, last)
    for r in range(rows):
        ca, cb = copies(nxt, 1 - slot, r)
        ca.start()
        cb.start()
    rw = rw_ref[...]
    w1 = rw[:, 0:1]
    w2 = rw[:, 1:2]
    d = x_ref.shape[1]
    ssq = jnp.zeros((rows, 1), F32)
    for g in range(d // (2 * LANE)):
        halves_a = _unpack_bf16_pair(a_scr[slot, :, g * LANE:(g + 1) * LANE])
        halves_b = _unpack_bf16_pair(b_scr[slot, :, g * LANE:(g + 1) * LANE])
        for half in range(2):
            c0 = (2 * g + half) * LANE
            moe = w1 * halves_a[half] + w2 * halves_b[half]
            xn = x_ref[:, c0:c0 + LANE] + mod_ref[0, g_idx:g_idx + 1, c0:c0 + LANE] * moe
            o_ref[:, c0:c0 + LANE] = xn
            if final:
                ssq = ssq + jnp.sum(xn * xn, axis=-1, keepdims=True)
    if final:
        o_ref[...] = o_ref[...] * lax.rsqrt(ssq * (1.0 / d) + NORM_EPS) * fw_ref[...]

    @pl.when(i == last)
    def _():
        wait_rows(1 - slot)


def _combine(dest1, dest2, ys, x2, route_w, mod, final_w, seq, g_idx, final, tm=128):
    t, d = x2.shape
    per_b = seq // tm
    est = 2 * (2 * tm * d * 4 + tm * LANE * 4) + 4 * tm * d * 4 + (6 << 20)
    return pl.pallas_call(
        functools.partial(_combine_kernel, g_idx=g_idx, final=final),
        grid_spec=pltpu.PrefetchScalarGridSpec(
            num_scalar_prefetch=2,
            grid=(t // tm,),
            in_specs=[pl.BlockSpec(memory_space=pl.ANY),
                      pl.BlockSpec((tm, d), lambda i, d1, d2: (i, 0)),
                      pl.BlockSpec((tm, LANE), lambda i, d1, d2: (i, 0)),
                      pl.BlockSpec((1, 6, d), lambda i, d1, d2: (i // per_b, 0, 0)),
                      pl.BlockSpec((1, d), lambda i, d1, d2: (0, 0))],
            out_specs=pl.BlockSpec((tm, d), lambda i, d1, d2: (i, 0)),
            scratch_shapes=[pltpu.VMEM((2, tm, d // 2), jnp.uint32), pltpu.VMEM((2, tm, d // 2), jnp.uint32),
                            pltpu.SemaphoreType.DMA((2, 2))]),
        out_shape=jax.ShapeDtypeStruct((t, d), F32),
        compiler_params=_params(("arbitrary",), est),
        name="moe_combine",
    )(dest1, dest2, ys, x2, route_w, mod, final_w.reshape(1, d))


def _small_in_weights(wt_stack, layer):
    d = wt_stack.shape[2]
    bounds = np.cumsum((0,) + IN_SIZES_HEAD)
    seg = [wt_stack[layer, bounds[i]:bounds[i + 1], :] for i in range(3, len(IN_SIZES_HEAD))]
    dt, q_lat, kv_lat, k_rope, sq, sk, sv = seg
    z64 = jnp.zeros((SWA_HEAD_DIM, d), wt_stack.dtype)

    def spread(w):
        h0, h1 = w[:SWA_HEAD_DIM], w[SWA_HEAD_DIM:]
        return jnp.concatenate([h0, z64, z64, h0, h1, z64, z64, h1], axis=0)

    half = MLA_ROPE // 2
    rot = jnp.concatenate([-k_rope[half:], k_rope[:half]], axis=0)
    small = jnp.concatenate([sq, spread(sk), spread(sv), kv_lat,
                             jnp.zeros((OFF_S["qlat"] - OFF_S["kvlat"] - MLA_KV_LORA, d), wt_stack.dtype),
                             q_lat], axis=0)
    tail = jnp.concatenate([k_rope, z64, rot, z64, dt,
                            jnp.zeros((LANE - SSM_HEADS, d), wt_stack.dtype)], axis=0)
    return small[None], tail[None]


def _extend_wq(wq_b):
    k = wq_b.shape[0]
    w = wq_b.reshape(k, MLA_HEADS, MLA_NOPE + MLA_ROPE)
    z = jnp.zeros((k, MLA_HEADS, LANE - MLA_ROPE), wq_b.dtype)
    rope = w[..., MLA_NOPE:]
    half = MLA_ROPE // 2
    rot = jnp.concatenate([-rope[..., half:], rope[..., :half]], axis=-1)
    return jnp.concatenate([w[..., :MLA_NOPE], rope, z, rot, z], axis=-1).reshape(k, -1).astype(BF16)


def _mixer(x, mod, pos_col, pos_row, cos_t, sin_t, norm_w, w_in_stack, layer, conv_w, conv_b, dt_bias, a_log,
           d_skip, ssm_norm, mla_q_norm, mla_wq_b, mla_kv_norm, mla_wkv_b, swa_sinks, w_branch, w_out):
    b, s, d = x.shape
    t = b * s
    swa_slopes, moba_slopes = _alibi_slopes()
    h = _norm_mod(x, norm_w, mod, sh_idx=0, sc_idx=1)
    h2 = h.reshape(t, d)
    w_small, w_tail = _small_in_weights(w_in_stack, layer)
    proj_a = _wproj(h2, w_in_stack, layer, 0, A_COLS, "in_proj_head")
    proj_g = _wproj(h2, w_in_stack, layer, GATE_COL0, N_BRANCH * d, "in_proj_gates")
    proj_s = _wproj(h2, w_small, 0, 0, S_COLS, "in_proj_small", tn=768)
    tail = _wproj(h2, w_tail, 0, 0, 3 * LANE, "in_proj_tail", tn=3 * LANE, out_dtype=F32)
    proj_a3 = proj_a.reshape(b, s, A_COLS)
    proj_s3 = proj_s.reshape(b, s, S_COLS)
    tail3 = tail.reshape(b, s, 3 * LANE)

    o_moba = _flash(proj_a3, proj_a3, proj_a3, heads=MOBA_HEADS, dq=MOBA_HEAD_DIM, dv=MOBA_HEAD_DIM,
                    q_off=OFF_A["mq"], k_off=OFF_A["mk"], v_off=OFF_A["mv"],
                    moba=True, q_scale=float(MOBA_HEAD_DIM ** -0.5) * LOG2E,
                    slopes=jnp.asarray(moba_slopes * np.float32(LOG2E)),
                    pos_col=pos_col, pos_row=pos_row, name="moba_attention")
    o_ssm = _ssd(proj_a3, tail3, OFF_A, conv_w, conv_b, dt_bias, a_log, d_skip, ssm_norm)
    q_m, k_m, v_m = _mla_project(proj_s, tail, OFF_S, mla_q_norm, _extend_wq(mla_wq_b), mla_kv_norm,
                                 mla_wkv_b.astype(BF16), cos_t, sin_t)
    o_mla = _flash(q_m.reshape(b, s, -1), k_m.reshape(b, s, -1), v_m.reshape(b, s, -1),
                   heads=MLA_HEADS, dq=MLA_QK, dv=MLA_V, q_off=0, k_off=0, v_off=0, name="mla_attention")
    o_swa = _swa(proj_s3, OFF_S, swa_sinks, pos_col, pos_row, swa_slopes)

    branches = [o.reshape(t, BRANCH_W) for o in (o_moba, o_ssm, o_mla, o_swa)]
    merged = _merge(branches, w_branch, layer, proj_g, d)
    return _out_proj(merged, w_out, layer, x.reshape(t, d), mod, s, g_idx=2)


def _moe(x2, seq, mod, norm_w, wg, bg, wr, br, w_gate, w_up, w_down, layer, final_w, final):
    t, d = x2.shape
    b = t // seq
    h, route_i, route_w, counts = _router(x2.reshape(b, seq, d), norm_w, mod, wg, bg, wr, br,
                                          sh_idx=3, sc_idx=4)
    route_i = route_i.reshape(t, LANE)
    counts = counts[0, :N_EXPERTS].astype(jnp.int32)
    padded = (counts + EXP_BLOCK - 1) // EXP_BLOCK * EXP_BLOCK
    pad_end = jnp.cumsum(padded)
    pad_start = pad_end - padded
    n_blocks = -(-(t * 2) // EXP_BLOCK) + N_EXPERTS
    block_row0 = jnp.arange(n_blocks, dtype=jnp.int32) * EXP_BLOCK
    block_expert = jnp.minimum(jnp.sum((pad_end[None, :] <= block_row0[:, None]).astype(jnp.int32), axis=1),
                               N_EXPERTS - 1)
    n_used = (pad_end[-1:] // EXP_BLOCK).astype(jnp.int32)
    first = (block_expert != jnp.concatenate([jnp.full((1,), -1, jnp.int32), block_expert[:-1]])).astype(jnp.int32)
    e_ids = jnp.arange(N_EXPERTS, dtype=jnp.int32)
    later = jnp.where((counts > 0)[None, :] & (e_ids[None, :] > e_ids[:, None]), e_ids[None, :], N_EXPERTS)
    nxt_e = jnp.min(later, axis=1)
    nxt = jnp.where(nxt_e == N_EXPERTS, -1, nxt_e)[block_expert].astype(jnp.int32)
    slot_tok, dest1, dest2 = _slot_tokens(route_i[:, 0], route_i[:, 1], route_i[:, 2], route_i[:, 3],
                                          pad_start.astype(jnp.int32), n_blocks * EXP_BLOCK)
    ys = _experts(slot_tok, block_expert, n_used, first, nxt, h.reshape(t, d // 2), w_gate, w_up, w_down, layer)
    return _combine(dest1, dest2, ys, x2, route_w.reshape(t, LANE), mod, final_w, seq, g_idx=5, final=final)


def kernel(x, c, positions, ada_w, ada_b, norm_mix, norm_ffn, w_in, conv_w, conv_b, dt_bias, a_log, d_skip,
           ssm_norm, mla_q_norm, mla_wq_b, mla_kv_norm, mla_wkv_b, swa_sinks, w_branch, w_out,
           router_group_w, router_group_b, router_w, router_b, exp_w_gate, exp_w_up, exp_w_down, final_norm):
    b, s, d = x.shape
    depth = ada_w.shape[0]
    mods = _ada_mod(c, ada_w, ada_b)
    pos_f = positions.astype(F32)
    pos_col = pos_f.reshape(b, s, 1)
    pos_row = pos_f.reshape(b, 1, s)
    cos_t, sin_t = _rope_tables(pos_f.reshape(b * s, 1))
    w_in_t = jnp.swapaxes(w_in, 1, 2)
    for l in range(depth):
        x2 = _mixer(x, mods[l], pos_col, pos_row, cos_t, sin_t, norm_mix[l], w_in_t, l, conv_w[l], conv_b[l],
                    dt_bias[l], a_log[l], d_skip[l], ssm_norm[l], mla_q_norm[l], mla_wq_b[l],
                    mla_kv_norm[l], mla_wkv_b[l], swa_sinks[l], w_branch, w_out)
        x2 = _moe(x2, s, mods[l], norm_ffn[l], router_group_w[l], router_group_b[l], router_w[l],
                  router_b[l], exp_w_gate, exp_w_up, exp_w_down, l, final_norm,
                  final=(l == depth - 1))
        x = x2.reshape(b, s, d)
    return x
```

```python
import functools
import math

import numpy as np
import jax
import jax.numpy as jnp
from jax import lax
from jax.experimental import pallas as pl
from jax.experimental.pallas import tpu as pltpu

F32 = jnp.float32
BF16 = jnp.bfloat16
HIGHEST = lax.Precision.HIGHEST

MOBA_HEADS = 8
MOBA_HEAD_DIM = 128
MOBA_BLOCK = 256
MOBA_TOPK = 3
SSM_D_INNER = 1024
SSM_HEAD_DIM = 64
SSM_HEADS = SSM_D_INNER // SSM_HEAD_DIM
SSM_GROUPS = 2
SSM_STATE = 128
SSM_CONV = 4
SSM_CHUNK = 128
SSM_BC = 2 * SSM_GROUPS * SSM_STATE
MLA_HEADS = 8
MLA_Q_LORA = 768
MLA_KV_LORA = 512
MLA_NOPE = 128
MLA_ROPE = 64
MLA_V = 128
ROPE_THETA = 10000.0
SWA_HEADS = 16
SWA_KV_HEADS = 2
SWA_HEAD_DIM = 64
SWA_WINDOW = 128
N_BRANCH = 4
BRANCH_W = 1024
N_GROUPS = 4
EXPERTS_PER_GROUP = 8
N_EXPERTS = N_GROUPS * EXPERTS_PER_GROUP
EXPERT_HIDDEN = 512
EXP_BLOCK = 256
NORM_EPS = 1e-6
N_ALIBI = MOBA_HEADS + SWA_HEADS

LOG2E = math.log2(math.e)
MASKED = -1e30
LANE = 128
VMEM_CAP = 60 * 1024 * 1024
ATTN_TILE = 512
FLASH_HEADS = 4
ONES_ROWS = 16
MLA_QK = 256


def _alibi_slopes():
    i = np.arange(1, N_ALIBI + 1, dtype=np.float64)
    s = np.exp2(-8.0 * i / N_ALIBI).astype(np.float32)
    return s[:SWA_HEADS], s[SWA_HEADS:]


IN_SIZES_HEAD = (3 * MOBA_HEADS * MOBA_HEAD_DIM, SSM_D_INNER, SSM_D_INNER + SSM_BC, SSM_HEADS, MLA_Q_LORA,
                 MLA_KV_LORA, MLA_ROPE, SWA_HEADS * SWA_HEAD_DIM, SWA_KV_HEADS * SWA_HEAD_DIM,
                 SWA_KV_HEADS * SWA_HEAD_DIM)
A_COLS = IN_SIZES_HEAD[0] + IN_SIZES_HEAD[1] + IN_SIZES_HEAD[2]
GATE_COL0 = sum(IN_SIZES_HEAD)
OFF_A = dict(mq=0, mk=1024, mv=2048, z=3072, xs=4096, bc=5120)
OFF_S = dict(sq=0, skx=1024, svx=1536, kvlat=2048, qlat=3072)
S_COLS = 3840


def _params(sem, est_bytes):
    limit = int(min(VMEM_CAP, max(est_bytes, 16 * 1024 * 1024)))
    return pltpu.CompilerParams(dimension_semantics=sem, vmem_limit_bytes=limit)


def _silu(v):
    return v * (1.0 / (1.0 + jnp.exp(-v)))


def _pack_bf16_pair(lo, hi):
    lo_bits = pltpu.bitcast(lo.astype(BF16).astype(F32), jnp.uint32)
    hi_bits = pltpu.bitcast(hi.astype(BF16).astype(F32), jnp.uint32)
    return hi_bits | lax.shift_right_logical(lo_bits, jnp.uint32(16))


def _split3(v):
    hi = v.astype(BF16)
    rest = v - hi.astype(F32)
    mid = rest.astype(BF16)
    lo = (rest - mid.astype(F32)).astype(BF16)
    return hi, mid, lo


def _unpack_bf16_pair(word):
    lo = pltpu.bitcast(lax.shift_left(word, jnp.uint32(16)), F32)
    hi = pltpu.bitcast(word & jnp.uint32(0xFFFF0000), F32)
    return lo, hi


def _ada_kernel(c_ref, w_ref, b_ref, o_ref):
    k = pl.program_id(1)

    @pl.when(k == 0)
    def _():
        o_ref[0] = jnp.broadcast_to(b_ref[0], o_ref.shape[1:])

    o_ref[0] += jnp.dot(c_ref[...].astype(BF16), w_ref[0].astype(BF16), preferred_element_type=F32)


def _ada_mod(c, ada_w, ada_b):
    depth, d, n = ada_w.shape
    b = c.shape[0]
    c8 = jnp.zeros((8, d), F32).at[:b].set(c)
    tk = LANE
    out = pl.pallas_call(
        _ada_kernel,
        grid=(depth, d // tk),
        in_specs=[pl.BlockSpec((8, tk), lambda l, k: (0, k)),
                  pl.BlockSpec((1, tk, n), lambda l, k: (l, k, 0)),
                  pl.BlockSpec((1, 1, n), lambda l, k: (l, 0, 0))],
        out_specs=pl.BlockSpec((1, 8, n), lambda l, k: (l, 0, 0)),
        out_shape=jax.ShapeDtypeStruct((depth, 8, n), F32),
        compiler_params=_params(("parallel", "arbitrary"), 3 * tk * n * 4 + 32 * n * 4 + (4 << 20)),
        name="ada_mod",
    )(c8, ada_w, ada_b.reshape(depth, 1, n))
    return out[:, :b].reshape(depth, b, 6, d)


def _norm_mod_kernel(x_ref, w_ref, mod_ref, o_ref, *, sh_idx, sc_idx):
    x = x_ref[0]
    y = x * lax.rsqrt(jnp.mean(x * x, axis=-1, keepdims=True) + NORM_EPS) * w_ref[...]
    m = mod_ref[0]
    o_ref[0] = (y * (1.0 + m[sc_idx:sc_idx + 1]) + m[sh_idx:sh_idx + 1]).astype(o_ref.dtype)


def _norm_mod(x, w, mod, sh_idx, sc_idx, tm=256):
    b, s, d = x.shape
    return pl.pallas_call(
        functools.partial(_norm_mod_kernel, sh_idx=sh_idx, sc_idx=sc_idx),
        grid=(b, s // tm),
        in_specs=[pl.BlockSpec((1, tm, d), lambda i, j: (i, j, 0)),
                  pl.BlockSpec((1, d), lambda i, j: (0, 0)),
                  pl.BlockSpec((1, 6, d), lambda i, j: (i, 0, 0))],
        out_specs=pl.BlockSpec((1, tm, d), lambda i, j: (i, j, 0)),
        out_shape=jax.ShapeDtypeStruct((b, s, d), BF16),
        compiler_params=_params(("parallel", "parallel"), 6 * tm * d * 4 + (4 << 20)),
        name="norm_mod",
    )(x, w.reshape(1, d), mod)


def _wproj_kernel(a_ref, w_ref, o_ref, wbf_scr):
    tn = wbf_scr.shape[0]
    rows = 64

    @pl.when(pl.program_id(1) == 0)
    def _():
        for r in range(0, tn, rows):
            wbf_scr[r:r + rows, :] = w_ref[0, r:r + rows, :].astype(BF16)

    o_ref[...] = lax.dot_general(a_ref[...], wbf_scr[...], (((1,), (1,)), ((), ())),
                                 preferred_element_type=F32).astype(o_ref.dtype)


def _wproj(a, wt_stack, layer, col0, n_cols, name, tm=1024, tn=512, out_dtype=BF16):
    m, k = a.shape
    assert n_cols % tn == 0 and m % tm == 0 and col0 % 8 == 0
    if col0 % tn == 0:
        w_spec = pl.BlockSpec((1, tn, k), lambda j, i: (layer, col0 // tn + j, 0))
    else:
        w_spec = pl.BlockSpec((pl.Element(1), pl.Element(tn), pl.Element(k)),
                              lambda j, i: (layer, pl.multiple_of(col0 + j * tn, 8), 0))
    est = 2 * (tm * k * 2 + tn * k * 4 + tm * tn * 2) + k * tn * 2 + tm * tn * 4 + (6 << 20)
    return pl.pallas_call(
        _wproj_kernel,
        grid=(n_cols // tn, m // tm),
        in_specs=[pl.BlockSpec((tm, k), lambda j, i: (i, 0)), w_spec],
        out_specs=pl.BlockSpec((tm, tn), lambda j, i: (i, j)),
        out_shape=jax.ShapeDtypeStruct((m, n_cols), out_dtype),
        scratch_shapes=[pltpu.VMEM((tn, k), BF16)],
        compiler_params=_params(("parallel", "arbitrary"), est),
        name=name,
    )(a, wt_stack)


def _rope_kernel(pos_ref, freq_ref, cos_ref, sin_ref):
    ang = pos_ref[...] * freq_ref[...]
    lane = lax.broadcasted_iota(jnp.int32, ang.shape, 1)
    live = lane < MLA_ROPE
    cos_ref[...] = jnp.where(live, jnp.cos(ang), 0.0)
    sin_ref[...] = jnp.where(live, jnp.sin(ang), 0.0)


def _rope_tables(pos_col):
    t = pos_col.shape[0]
    half = MLA_ROPE // 2
    inv = ROPE_THETA ** (-np.arange(half, dtype=np.float32) / half)
    freq = np.zeros((1, LANE), np.float32)
    freq[0, :half] = inv
    freq[0, half:2 * half] = inv
    tm = 512
    return pl.pallas_call(
        _rope_kernel,
        grid=(t // tm,),
        in_specs=[pl.BlockSpec((tm, 1), lambda i: (i, 0)),
                  pl.BlockSpec((1, LANE), lambda i: (0, 0))],
        out_specs=[pl.BlockSpec((tm, LANE), lambda i: (i, 0))] * 2,
        out_shape=[jax.ShapeDtypeStruct((t, LANE), F32)] * 2,
        compiler_params=_params(("parallel",), 16 << 20),
        name="rope_tables",
    )(pos_col, jnp.asarray(freq))


def _mla_q_kernel(x_ref, nw_ref, w_ref, cos_ref, sin_ref, o_ref, *, scale):
    x = x_ref[...].astype(F32)
    y = x * lax.rsqrt(jnp.mean(x * x, axis=-1, keepdims=True) + NORM_EPS) * nw_ref[...]
    r = jnp.dot(y.astype(BF16), w_ref[...], preferred_element_type=F32)
    cos = cos_ref[...]
    sin = sin_ref[...]
    for h in range(MLA_HEADS):
        base = h * 3 * LANE
        nope = r[:, base:base + LANE]
        rope = r[:, base + LANE:base + 2 * LANE]
        rot = r[:, base + 2 * LANE:base + 3 * LANE]
        o_ref[:, h * MLA_QK:h * MLA_QK + LANE] = (nope * scale).astype(o_ref.dtype)
        o_ref[:, h * MLA_QK + LANE:(h + 1) * MLA_QK] = (
            (rope * cos + rot * sin) * scale).astype(o_ref.dtype)


def _mla_kv_kernel(x_ref, nw_ref, w_ref, kr_ref, krot_ref, cos_ref, sin_ref, k_ref, v_ref):
    x = x_ref[...].astype(F32)
    y = x * lax.rsqrt(jnp.mean(x * x, axis=-1, keepdims=True) + NORM_EPS) * nw_ref[...]
    r = jnp.dot(y.astype(BF16), w_ref[...], preferred_element_type=F32)
    kr = (kr_ref[...] * cos_ref[...] + krot_ref[...] * sin_ref[...]).astype(k_ref.dtype)
    for h in range(MLA_HEADS):
        base = h * (MLA_NOPE + MLA_V)
        k_ref[:, h * MLA_QK:h * MLA_QK + LANE] = r[:, base:base + MLA_NOPE].astype(k_ref.dtype)
        k_ref[:, h * MLA_QK + LANE:(h + 1) * MLA_QK] = kr
        v_ref[:, h * MLA_V:(h + 1) * MLA_V] = r[:, base + MLA_NOPE:base + MLA_NOPE + MLA_V].astype(
            v_ref.dtype)


def _mla_project(proj, tail, off, q_norm, wq_ext, kv_norm, wkv, cos_t, sin_t, tm=512):
    t = proj.shape[0]
    scale = float((MLA_NOPE + MLA_ROPE) ** -0.5) * LOG2E
    q = pl.pallas_call(
        functools.partial(_mla_q_kernel, scale=scale),
        grid=(t // tm,),
        in_specs=[pl.BlockSpec((tm, MLA_Q_LORA), lambda i: (i, off["qlat"] // MLA_Q_LORA)),
                  pl.BlockSpec((1, MLA_Q_LORA), lambda i: (0, 0)),
                  pl.BlockSpec(wq_ext.shape, lambda i: (0, 0)),
                  pl.BlockSpec((tm, LANE), lambda i: (i, 0)),
                  pl.BlockSpec((tm, LANE), lambda i: (i, 0))],
        out_specs=pl.BlockSpec((tm, MLA_HEADS * MLA_QK), lambda i: (i, 0)),
        out_shape=jax.ShapeDtypeStruct((t, MLA_HEADS * MLA_QK), BF16),
        compiler_params=_params(("parallel",), 40 << 20),
        name="mla_q_proj",
    )(proj, q_norm.reshape(1, -1), wq_ext, cos_t, sin_t)
    k, v = pl.pallas_call(
        _mla_kv_kernel,
        grid=(t // tm,),
        in_specs=[pl.BlockSpec((tm, MLA_KV_LORA), lambda i: (i, off["kvlat"] // MLA_KV_LORA)),
                  pl.BlockSpec((1, MLA_KV_LORA), lambda i: (0, 0)),
                  pl.BlockSpec(wkv.shape, lambda i: (0, 0)),
                  pl.BlockSpec((tm, LANE), lambda i: (i, 0)),
                  pl.BlockSpec((tm, LANE), lambda i: (i, 1)),
                  pl.BlockSpec((tm, LANE), lambda i: (i, 0)),
                  pl.BlockSpec((tm, LANE), lambda i: (i, 0))],
        out_specs=[pl.BlockSpec((tm, MLA_HEADS * MLA_QK), lambda i: (i, 0)),
                   pl.BlockSpec((tm, MLA_HEADS * MLA_V), lambda i: (i, 0))],
        out_shape=[jax.ShapeDtypeStruct((t, MLA_HEADS * MLA_QK), BF16),
                   jax.ShapeDtypeStruct((t, MLA_HEADS * MLA_V), BF16)],
        compiler_params=_params(("parallel",), 40 << 20),
        name="mla_kv_proj",
    )(proj, kv_norm.reshape(1, -1), wkv, tail, tail, cos_t, sin_t)
    return q, k, v


def _flash_kernel(*refs, moba, q_scale, dq, dv):
    if moba:
        slope_ref, q_ref, k_ref, v_ref, posk_ref, posq_ref, o_ref, vt_scr, kmean_scr, sel_scr = refs
    else:
        q_ref, k_ref, v_ref, o_ref, vt_scr = refs
    tile = ATTN_TILE
    seq = k_ref.shape[1]
    sub = tile // MOBA_BLOCK
    nblk = seq // MOBA_BLOCK
    qi = pl.program_id(2)
    heads = range(FLASH_HEADS)

    @pl.when(qi == 0)
    def _():
        for j in heads:
            for c in range(seq // tile):
                vt_scr[j, c, 0:dv, :] = (
                    v_ref[0, c * tile:(c + 1) * tile, j * dv:(j + 1) * dv].astype(F32).T.astype(BF16))
                vt_scr[j, c, dv:dv + ONES_ROWS, :] = jnp.ones((ONES_ROWS, tile), BF16)
            if moba:
                for n in range(nblk):
                    kblk = k_ref[0, n * MOBA_BLOCK:(n + 1) * MOBA_BLOCK, j * dq:(j + 1) * dq].astype(F32)
                    kmean_scr[j, n:n + 1, :] = jnp.mean(kblk, axis=0, keepdims=True)

    q = []
    for j in heads:
        q_raw = q_ref[0, :, j * dq:(j + 1) * dq]
        q.append(q_raw if q_scale is None else (q_raw.astype(F32) * q_scale).astype(BF16))
        if moba:
            gate = lax.dot_general(kmean_scr[j], q_raw.astype(F32), (((1,), (1,)), ((), ())),
                                   precision=HIGHEST, preferred_element_type=F32)
            n_iota = lax.broadcasted_iota(jnp.int32, gate.shape, 0)
            q_blk = qi * sub + lax.broadcasted_iota(jnp.int32, gate.shape, 1) // MOBA_BLOCK
            beaten = jnp.zeros(gate.shape, F32)
            for m in range(nblk):
                gm = gate[m:m + 1, :]
                wins = jnp.where(gm > gate, 1.0, jnp.where(gm == gate, jnp.where(m < n_iota, 1.0, 0.0), 0.0))
                beaten = beaten + jnp.where(m < q_blk, wins, 0.0)
            sel_scr[j] = jnp.where(beaten < MOBA_TOPK, jnp.where(n_iota < q_blk, 0.0, MASKED), MASKED)
    if moba:
        slope = [slope_ref[pl.program_id(1) * FLASH_HEADS + j] for j in heads]
        pq = [posq_ref[0] * slope[j] for j in heads]

    def scores(j, c):
        start = pl.multiple_of(c * tile, tile)
        kb = k_ref[0, pl.ds(start, tile), j * dq:(j + 1) * dq]
        st = lax.dot_general(kb, q[j], (((1,), (1,)), ((), ())), preferred_element_type=F32)
        if moba:
            pk = posk_ref[0, pl.ds(start, tile), :] * slope[j]
            st = st - jnp.abs(pq[j] - pk)
        return st

    def block_bias(j, c):
        rows = [jnp.broadcast_to(sel_scr[j, pl.ds(c * sub + r, 1), :], (MOBA_BLOCK, tile)) for r in range(sub)]
        return jnp.concatenate(rows, axis=0)

    key_i = lax.broadcasted_iota(jnp.int32, (tile, tile), 0)
    qry_i = lax.broadcasted_iota(jnp.int32, (tile, tile), 1)
    causal_bias = jnp.where(key_i <= qry_i, 0.0, MASKED)
    init = []
    for j in heads:
        st = scores(j, qi)
        if moba:
            same_blk = (key_i // MOBA_BLOCK) == (qry_i // MOBA_BLOCK)
            st = st + jnp.where(same_blk, causal_bias, block_bias(j, qi))
        else:
            st = st + causal_bias
        m0 = jnp.max(st, axis=0, keepdims=True)
        p = jnp.exp2(st - m0)
        init.append((m0, jnp.dot(vt_scr[j, qi], p.astype(BF16), preferred_element_type=F32)))

    def body(c, carry):
        s_c = [scores(j, c) for j in heads]
        if moba:
            s_c = [s_c[j] + block_bias(j, c) for j in heads]
        m_new = [jnp.maximum(carry[j][0], jnp.max(s_c[j], axis=0, keepdims=True)) for j in heads]
        p_c = [jnp.exp2(s_c[j] - m_new[j]).astype(BF16) for j in heads]
        out = []
        for j in heads:
            m_prev, acc = carry[j]
            alpha = jnp.exp2(m_prev - m_new[j])
            acc_new = alpha * acc + jnp.dot(vt_scr[j, c], p_c[j], preferred_element_type=F32)
            out.append((m_new[j], acc_new))
        return tuple(out)

    final = lax.fori_loop(0, qi, body, tuple(init))
    for j in heads:
        _, acc = final[j]
        o_ref[0, :, j * dv:(j + 1) * dv] = (acc[0:dv] * (1.0 / acc[dv:dv + 1])).T.astype(o_ref.dtype)


def _flash(q, k, v, *, heads, dq, dv, q_off, k_off, v_off, moba=False, q_scale=None,
           slopes=None, pos_col=None, pos_row=None, name="flash"):
    b, s, _ = q.shape
    tile = ATTN_TILE
    hp = FLASH_HEADS
    assert heads % hp == 0 and q_off % (hp * dq) == 0 and k_off % (hp * dq) == 0 and v_off % (hp * dv) == 0
    qb, kb, vb = q_off // (hp * dq), k_off // (hp * dq), v_off // (hp * dv)
    in_specs = [pl.BlockSpec((1, tile, hp * dq), lambda bi, h, i: (bi, i, qb + h)),
                pl.BlockSpec((1, s, hp * dq), lambda bi, h, i: (bi, 0, kb + h)),
                pl.BlockSpec((1, s, hp * dv), lambda bi, h, i: (bi, 0, vb + h))]
    args = [q, k, v]
    scratch = [pltpu.VMEM((hp, s // tile, dv + ONES_ROWS, tile), BF16)]
    if moba:
        in_specs = [pl.BlockSpec(memory_space=pltpu.SMEM)] + in_specs + [
            pl.BlockSpec((1, s, 1), lambda bi, h, i: (bi, 0, 0)),
            pl.BlockSpec((1, 1, tile), lambda bi, h, i: (bi, 0, i))]
        args = [slopes] + args + [pos_col, pos_row]
        scratch += [pltpu.VMEM((hp, s // MOBA_BLOCK, dq), F32), pltpu.VMEM((hp, s // MOBA_BLOCK, tile), F32)]
    est = hp * (2 * (s * dq * 2 + s * dv * 2) + s * dv * 2) + s * LANE * 4 * 2 + (20 << 20)
    return pl.pallas_call(
        functools.partial(_flash_kernel, moba=moba, q_scale=q_scale, dq=dq, dv=dv),
        grid=(b, heads // hp, s // tile),
        in_specs=in_specs,
        out_specs=pl.BlockSpec((1, tile, hp * dv), lambda bi, h, i: (bi, i, h)),
        out_shape=jax.ShapeDtypeStruct((b, s, heads * dv), BF16),
        scratch_shapes=scratch,
        compiler_params=_params(("parallel", "parallel", "arbitrary"), est),
        name=name,
    )(*args)


def _swa_kernel(sink_ref, q_ref, kp_ref, kc_ref, vp_ref, vc_ref, pq_ref, pkp_ref, pkc_ref, o_ref,
                *, slopes):
    w = SWA_WINDOW
    n = pl.program_id(1)
    kx = jnp.concatenate([kp_ref[0], kc_ref[0]], axis=0)
    vx = jnp.concatenate([vp_ref[0], vc_ref[0]], axis=0)
    pk = jnp.concatenate([pkp_ref[0], pkc_ref[0]], axis=1)
    dist = jnp.abs(pq_ref[0] - pk)
    qi = lax.broadcasted_iota(jnp.int32, dist.shape, 0)
    kk = lax.broadcasted_iota(jnp.int32, dist.shape, 1)
    allowed = jnp.where(kk > qi, jnp.where(kk <= qi + w, 0.0, MASKED), MASKED)
    mask_bias = jnp.where(kk >= w, allowed, jnp.where(n > 0, allowed, MASKED))
    scale = float(SWA_HEAD_DIM ** -0.5)
    rep = SWA_HEADS // SWA_KV_HEADS
    heads = range(SWA_HEADS)
    col = [(2 * (h // rep) + h % 2) * LANE for h in heads]
    s = [lax.dot_general(q_ref[0, :, (h // 2) * LANE:(h // 2 + 1) * LANE], kx[:, col[h]:col[h] + LANE],
                         (((1,), (1,)), ((), ())), preferred_element_type=F32) for h in heads]
    s = [s[h] * scale + (mask_bias - float(slopes[h]) * dist) for h in heads]
    m = [jnp.maximum(jnp.max(s[h], axis=1, keepdims=True), sink_ref[h]) for h in heads]
    p = [jnp.exp(s[h] - m[h]) for h in heads]
    inv = [1.0 / (jnp.sum(p[h], axis=1, keepdims=True) + jnp.exp(sink_ref[h] - m[h])) for h in heads]
    p = [(p[h] * inv[h]).astype(BF16) for h in heads]
    for pair in range(SWA_HEADS // 2):
        h0, h1 = 2 * pair, 2 * pair + 1
        acc = (jnp.dot(p[h0], vx[:, col[h0]:col[h0] + LANE], preferred_element_type=F32)
               + jnp.dot(p[h1], vx[:, col[h1]:col[h1] + LANE], preferred_element_type=F32))
        o_ref[0, :, pair * LANE:(pair + 1) * LANE] = acc.astype(o_ref.dtype)


def _swa(proj3, off, sinks, pos_col, pos_row, slopes):
    b, s, _ = proj3.shape
    w = SWA_WINDOW
    qw = SWA_HEADS * SWA_HEAD_DIM
    kw = 4 * LANE
    prev = lambda j: jnp.maximum(j - 1, 0)
    return pl.pallas_call(
        functools.partial(_swa_kernel, slopes=tuple(float(v) for v in slopes)),
        grid=(b, s // w),
        in_specs=[pl.BlockSpec(memory_space=pltpu.SMEM),
                  pl.BlockSpec((1, w, qw), lambda i, j: (i, j, off["sq"] // qw)),
                  pl.BlockSpec((1, w, kw), lambda i, j: (i, prev(j), off["skx"] // kw)),
                  pl.BlockSpec((1, w, kw), lambda i, j: (i, j, off["skx"] // kw)),
                  pl.BlockSpec((1, w, kw), lambda i, j: (i, prev(j), off["svx"] // kw)),
                  pl.BlockSpec((1, w, kw), lambda i, j: (i, j, off["svx"] // kw)),
                  pl.BlockSpec((1, w, 1), lambda i, j: (i, j, 0)),
                  pl.BlockSpec((1, 1, w), lambda i, j: (i, 0, prev(j))),
                  pl.BlockSpec((1, 1, w), lambda i, j: (i, 0, j))],
        out_specs=pl.BlockSpec((1, w, qw), lambda i, j: (i, j, 0)),
        out_shape=jax.ShapeDtypeStruct((b, s, qw), BF16),
        compiler_params=_params(("parallel", "parallel"), 24 << 20),
        name="swa_attention",
    )(sinks, proj3, proj3, proj3, proj3, proj3, pos_col, pos_row, pos_row)


def _ssd_kernel(xs_ref, bc_ref, z_ref, dt_ref, cwx_ref, cwb_ref, cbx_ref, cbb_ref, dtb_ref,
                alog_ref, dsk_ref, nw_ref, exp_ref, o_ref, padx_scr, padb_scr, st_scr):
    ch = SSM_CHUNK
    c = pl.program_id(1)

    @pl.when(c == 0)
    def _():
        padx_scr[0:8, :] = jnp.zeros((8, padx_scr.shape[1]), F32)
        padb_scr[0:8, :] = jnp.zeros((8, padb_scr.shape[1]), F32)
        st_scr[...] = jnp.zeros(st_scr.shape, F32)

    padx_scr[8:8 + ch, :] = xs_ref[0].astype(F32)
    padb_scr[8:8 + ch, :] = bc_ref[0].astype(F32)

    def conv(pad_scr, w_ref, b_ref):
        acc = b_ref[...] + w_ref[0:1, :] * pad_scr[5:5 + ch, :]
        for k in range(1, SSM_CONV):
            acc = acc + w_ref[k:k + 1, :] * pad_scr[5 + k:5 + k + ch, :]
        return _silu(acc)

    xs = conv(padx_scr, cwx_ref, cbx_ref)
    bcv = conv(padb_scr, cwb_ref, cbb_ref)
    padx_scr[0:8, :] = padx_scr[ch:ch + 8, :]
    padb_scr[0:8, :] = padb_scr[ch:ch + 8, :]

    gn = SSM_GROUPS * SSM_STATE
    dtr = dt_ref[0] + dtb_ref[...]
    dt = jnp.maximum(dtr, 0.0) + jnp.log(1.0 + jnp.exp(-jnp.abs(dtr)))
    a = dt * (-jnp.exp(alog_ref[...]))
    expand = exp_ref[...]
    row = lax.broadcasted_iota(jnp.int32, (ch, ch), 0)
    colm = lax.broadcasted_iota(jnp.int32, (ch, ch), 1)
    tril = row >= colm
    tri = jnp.where(tril, 1.0, 0.0).astype(BF16)

    def times_select(v, sel):
        return sum(jnp.dot(part, sel, preferred_element_type=F32) for part in _split3(v))

    dt_e = times_select(dt, expand)
    acs = sum(jnp.dot(tri, part, preferred_element_type=F32) for part in _split3(a))
    acs_e = times_select(acs, expand)
    acs_t = acs.T
    x_dt = xs * dt_e
    last = acs_e[ch - 1:ch, :]
    x_dec = (x_dt * jnp.exp(last - acs_e)).astype(BF16)
    lane = lax.broadcasted_iota(jnp.int32, (ch, LANE), 1)
    lo = lane < SSM_HEAD_DIM
    half_w = SSM_D_INNER // SSM_GROUPS
    heads_per_group = SSM_HEADS // SSM_GROUPS
    groups = range(SSM_GROUPS)
    heads = range(SSM_HEADS)
    bg = [bcv[:, g * SSM_STATE:(g + 1) * SSM_STATE] for g in groups]
    cg = [bcv[:, gn + g * SSM_STATE:gn + (g + 1) * SSM_STATE].astype(BF16) for g in groups]
    gmat = [lax.dot_general(cg[g], bg[g].astype(BF16), (((1,), (1,)), ((), ())),
                            preferred_element_type=F32) for g in groups]
    st_in = [st_scr[:, g * half_w:(g + 1) * half_w] for g in groups]
    yoff = [jnp.dot(cg[g], st_in[g].astype(BF16), preferred_element_type=F32) for g in groups]
    upd = [jnp.dot(bg[g].T.astype(BF16), x_dec[:, g * half_w:(g + 1) * half_w],
                   preferred_element_type=F32) for g in groups]
    seg = [jnp.exp(jnp.where(tril, acs[:, h:h + 1] - acs_t[h:h + 1, :], -jnp.inf)) for h in heads]
    mmat = [(gmat[h // heads_per_group] * seg[h]).astype(BF16) for h in heads]
    xh = []
    for h in heads:
        xp = x_dt[:, (h // 2) * LANE:(h // 2 + 1) * LANE]
        xh.append((jnp.where(lo, xp, 0.0) if h % 2 == 0 else jnp.where(lo, 0.0, xp)).astype(BF16))
    ydiag = [jnp.dot(mmat[2 * pr], xh[2 * pr], preferred_element_type=F32)
             + jnp.dot(mmat[2 * pr + 1], xh[2 * pr + 1], preferred_element_type=F32)
             for pr in range(SSM_HEADS // 2)]
    for g in groups:
        st_scr[:, g * half_w:(g + 1) * half_w] = (
            st_in[g] * jnp.exp(last[:, g * half_w:(g + 1) * half_w]) + upd[g])
    y = (jnp.concatenate(ydiag, axis=1) + jnp.concatenate(yoff, axis=1) * jnp.exp(acs_e)
         + dsk_ref[...] * xs)
    gz = y * _silu(z_ref[0].astype(F32))
    outs = []
    for g in range(SSM_GROUPS):
        gg = gz[:, g * half_w:(g + 1) * half_w]
        outs.append(gg * lax.rsqrt(jnp.mean(gg * gg, axis=-1, keepdims=True) + NORM_EPS))
    o_ref[0] = (jnp.concatenate(outs, axis=1) * nw_ref[...]).astype(o_ref.dtype)


def _ssd(proj3, tail3, off, conv_w, conv_b, dt_bias, a_log, d_skip, norm_w):
    b, s, _ = proj3.shape
    ch = SSM_CHUNK
    di = SSM_D_INNER
    pad16 = lambda v: jnp.zeros((1, LANE), F32).at[0, :SSM_HEADS].set(v)
    expand = np.zeros((LANE, di), np.float32)
    for h in range(SSM_HEADS):
        expand[h, h * SSM_HEAD_DIM:(h + 1) * SSM_HEAD_DIM] = 1.0
    full = lambda shape: pl.BlockSpec(shape, lambda i, j: (0,) * len(shape))
    return pl.pallas_call(
        _ssd_kernel,
        grid=(b, s // ch),
        in_specs=[pl.BlockSpec((1, ch, di), lambda i, j: (i, j, off["xs"] // di)),
                  pl.BlockSpec((1, ch, SSM_BC), lambda i, j: (i, j, off["bc"] // SSM_BC)),
                  pl.BlockSpec((1, ch, di), lambda i, j: (i, j, off["z"] // di)),
                  pl.BlockSpec((1, ch, LANE), lambda i, j: (i, j, 2)),
                  full((SSM_CONV, di)), full((SSM_CONV, SSM_BC)), full((1, di)), full((1, SSM_BC)),
                  full((1, LANE)), full((1, LANE)), full((1, di)), full((1, di)), full((LANE, di))],
        out_specs=pl.BlockSpec((1, ch, di), lambda i, j: (i, j, 0)),
        out_shape=jax.ShapeDtypeStruct((b, s, di), BF16),
        scratch_shapes=[pltpu.VMEM((ch + 8, di), F32), pltpu.VMEM((ch + 8, SSM_BC), F32),
                        pltpu.VMEM((SSM_STATE, di), F32)],
        compiler_params=_params(("parallel", "arbitrary"), 32 << 20),
        name="ssd_mixer",
    )(proj3, proj3, proj3, tail3, conv_w[:, :di], conv_w[:, di:], conv_b[:di].reshape(1, di),
      conv_b[di:].reshape(1, SSM_BC), pad16(dt_bias), pad16(a_log),
      jnp.repeat(d_skip, SSM_HEAD_DIM).reshape(1, di), norm_w.reshape(1, di), jnp.asarray(expand, BF16))


def _merge_kernel(o0_ref, o1_ref, o2_ref, o3_ref, w_ref, g0_ref, g1_ref, g2_ref, g3_ref, out_ref, wbf_scr):
    @pl.when(pl.program_id(1) == 0)
    def _():
        for r in range(N_BRANCH):
            wbf_scr[r] = w_ref[0, r].astype(BF16)

    acc = None
    for r, (o_ref, g_ref) in enumerate(((o0_ref, g0_ref), (o1_ref, g1_ref), (o2_ref, g2_ref),
                                        (o3_ref, g3_ref))):
        y = jnp.dot(o_ref[...], wbf_scr[r], preferred_element_type=F32)
        gate = 1.0 / (1.0 + jnp.exp(-g_ref[...].astype(F32)))
        acc = gate * y if acc is None else acc + gate * y
    out_ref[...] = acc.astype(out_ref.dtype)


def _merge(branches, w_branch, layer, proj, d, tm=1024, tn=512):
    t = proj.shape[0]
    nj = d // tn
    o_spec = pl.BlockSpec((tm, BRANCH_W), lambda j, i: (i, 0))
    g_specs = [pl.BlockSpec((tm, tn), functools.partial(lambda j, i, r: (i, r * nj + j), r=r))
               for r in range(N_BRANCH)]
    est = (2 * (4 * tm * BRANCH_W * 2 + 4 * BRANCH_W * tn * 4 + 4 * tm * tn * 2 + tm * tn * 2)
           + 4 * BRANCH_W * tn * 2 + 3 * tm * tn * 4 + (6 << 20))
    return pl.pallas_call(
        _merge_kernel,
        grid=(nj, t // tm),
        in_specs=[o_spec] * 4 + [pl.BlockSpec((1, N_BRANCH, BRANCH_W, tn), lambda j, i: (layer, 0, 0, j))]
        + g_specs,
        out_specs=pl.BlockSpec((tm, tn), lambda j, i: (i, j)),
        out_shape=jax.ShapeDtypeStruct((t, d), BF16),
        scratch_shapes=[pltpu.VMEM((N_BRANCH, BRANCH_W, tn), BF16)],
        compiler_params=_params(("parallel", "arbitrary"), est),
        name="branch_merge",
    )(*branches, w_branch, proj, proj, proj, proj)


def _out_proj_kernel(a_ref, w_ref, x_ref, mod_ref, o_ref, wbf_scr, *, g_idx):
    @pl.when(pl.program_id(1) == 0)
    def _():
        rows = 512
        for r in range(0, wbf_scr.shape[0], rows):
            wbf_scr[r:r + rows, :] = w_ref[0, r:r + rows, :].astype(BF16)

    y = jnp.dot(a_ref[...], wbf_scr[...], preferred_element_type=F32)
    o_ref[...] = x_ref[...] + mod_ref[0, g_idx:g_idx + 1, :] * y


def _out_proj(merged, w_out, layer, x2, mod, seq, g_idx, tm=1024, tn=512):
    t, d = x2.shape
    per_b = seq // tm
    est = 2 * (tm * d * 2 + d * tn * 4 + 2 * tm * tn * 4) + d * tn * 2 + tm * tn * 4 + (4 << 20)
    return pl.pallas_call(
        functools.partial(_out_proj_kernel, g_idx=g_idx),
        grid=(d // tn, t // tm),
        in_specs=[pl.BlockSpec((tm, d), lambda j, i: (i, 0)),
                  pl.BlockSpec((1, d, tn), lambda j, i: (layer, 0, j)),
                  pl.BlockSpec((tm, tn), lambda j, i: (i, j)),
                  pl.BlockSpec((1, 6, tn), lambda j, i: (i // per_b, 0, j))],
        out_specs=pl.BlockSpec((tm, tn), lambda j, i: (i, j)),
        out_shape=jax.ShapeDtypeStruct((t, d), F32),
        scratch_shapes=[pltpu.VMEM((d, tn), BF16)],
        compiler_params=_params(("parallel", "arbitrary"), est),
        name="out_proj",
    )(merged, w_out, x2, mod)


def _router_kernel(x_ref, w_ref, mod_ref, rwh_ref, rwl_ref, rb_ref, h_ref, ri_ref, rwt_ref, cnt_ref, carry_scr,
                   *, sh_idx, sc_idx):
    step = pl.program_id(0) * pl.num_programs(1) + pl.program_id(1)

    @pl.when(step == 0)
    def _():
        carry_scr[...] = jnp.zeros(carry_scr.shape, F32)

    x = x_ref[0]
    y = x * lax.rsqrt(jnp.mean(x * x, axis=-1, keepdims=True) + NORM_EPS) * w_ref[...]
    m = mod_ref[0]
    h = y * (1.0 + m[sc_idx:sc_idx + 1]) + m[sh_idx:sh_idx + 1]
    for g in range(h.shape[1] // (2 * LANE)):
        h_ref[0, :, g * LANE:(g + 1) * LANE] = _pack_bf16_pair(h[:, 2 * g * LANE:(2 * g + 1) * LANE],
                                                               h[:, (2 * g + 1) * LANE:(2 * g + 2) * LANE])
    h_hi = h.astype(BF16)
    h_lo = (h - h_hi.astype(F32)).astype(BF16)
    nt = (((1,), (1,)), ((), ()))
    logits = (lax.dot_general(h_hi, rwh_ref[...], nt, preferred_element_type=F32)
              + lax.dot_general(h_hi, rwl_ref[...], nt, preferred_element_type=F32)
              + lax.dot_general(h_lo, rwh_ref[...], nt, preferred_element_type=F32)) + rb_ref[...]
    tm = logits.shape[0]
    lane = lax.broadcasted_iota(jnp.int32, logits.shape, 1)
    big = jnp.int32(4 * LANE)
    neg = -jnp.inf

    def first_argmax(vals):
        mx = jnp.max(vals, axis=1, keepdims=True)
        idx = jnp.min(jnp.where(vals == mx, lane, big), axis=1, keepdims=True)
        return mx, idx

    lg = jnp.where(lane < N_GROUPS, logits, neg)
    gmax, gidx = first_argmax(lg)
    g_w = 1.0 / jnp.sum(jnp.exp(lg - gmax), axis=1, keepdims=True)
    lo = N_GROUPS + gidx * EXPERTS_PER_GROUP
    in_group = jnp.where(lane >= lo, jnp.where(lane < lo + EXPERTS_PER_GROUP, 1.0, 0.0), 0.0) > 0.0
    le = jnp.where(in_group, logits, neg)
    m1, i1 = first_argmax(le)
    le2 = jnp.where(lane == i1, neg, le)
    m2, i2 = first_argmax(le2)
    denom = jnp.sum(jnp.exp(le - m1), axis=1, keepdims=True)
    p1 = 1.0 / denom
    p2 = jnp.exp(m2 - m1) / denom
    w1 = p1 / (p1 + p2) * g_w
    w2 = p2 / (p1 + p2) * g_w
    e1 = i1 - N_GROUPS
    e2 = i2 - N_GROUPS

    oh1 = jnp.where(lane == e1, 1.0, 0.0)
    oh2 = jnp.where(lane == e2, 1.0, 0.0)
    both = oh1 + oh2
    r_i = lax.broadcasted_iota(jnp.int32, (tm, tm), 0)
    c_i = lax.broadcasted_iota(jnp.int32, (tm, tm), 1)
    strict = jnp.where(r_i > c_i, 1.0, 0.0).astype(BF16)
    before = jnp.dot(strict, both.astype(BF16), preferred_element_type=F32) + carry_scr[...]
    rank1 = jnp.sum(before * oh1, axis=1, keepdims=True).astype(jnp.int32)
    rank2 = jnp.sum(before * oh2, axis=1, keepdims=True).astype(jnp.int32)
    carry_scr[...] = carry_scr[...] + jnp.sum(both, axis=0, keepdims=True)
    cnt_ref[...] = carry_scr[...]
    ri_ref[0] = jnp.where(lane == 0, e1, jnp.where(lane == 1, e2, jnp.where(
        lane == 2, rank1, jnp.where(lane == 3, rank2, 0))))
    rwt_ref[0] = jnp.where(lane == 0, w1, jnp.where(lane == 1, w2, 0.0))


def _router(x, w, mod, wg, bg, wr, br, sh_idx, sc_idx, tm=256):
    b, s, d = x.shape
    rw = jnp.concatenate([wg.T, wr.T, jnp.zeros((LANE - N_GROUPS - N_EXPERTS, d), F32)], axis=0)
    rb = jnp.zeros((1, LANE), F32).at[0, :N_GROUPS].set(bg).at[0, N_GROUPS:N_GROUPS + N_EXPERTS].set(br)
    rw_hi = rw.astype(BF16)
    rw_lo = (rw - rw_hi.astype(F32)).astype(BF16)
    return pl.pallas_call(
        functools.partial(_router_kernel, sh_idx=sh_idx, sc_idx=sc_idx),
        grid=(b, s // tm),
        in_specs=[pl.BlockSpec((1, tm, d), lambda i, j: (i, j, 0)),
                  pl.BlockSpec((1, d), lambda i, j: (0, 0)),
                  pl.BlockSpec((1, 6, d), lambda i, j: (i, 0, 0)),
                  pl.BlockSpec((LANE, d), lambda i, j: (0, 0)),
                  pl.BlockSpec((LANE, d), lambda i, j: (0, 0)),
                  pl.BlockSpec((1, LANE), lambda i, j: (0, 0))],
        out_specs=[pl.BlockSpec((1, tm, d // 2), lambda i, j: (i, j, 0)),
                   pl.BlockSpec((1, tm, LANE), lambda i, j: (i, j, 0)),
                   pl.BlockSpec((1, tm, LANE), lambda i, j: (i, j, 0)),
                   pl.BlockSpec((1, LANE), lambda i, j: (0, 0))],
        out_shape=[jax.ShapeDtypeStruct((b, s, d // 2), jnp.uint32),
                   jax.ShapeDtypeStruct((b, s, LANE), jnp.int32),
                   jax.ShapeDtypeStruct((b, s, LANE), F32),
                   jax.ShapeDtypeStruct((1, LANE), F32)],
        scratch_shapes=[pltpu.VMEM((1, LANE), F32)],
        compiler_params=_params(("arbitrary", "arbitrary"), 8 * tm * d * 4 + d * LANE * 8 + (8 << 20)),
        name="moe_router",
    )(x, w.reshape(1, d), mod, rw_hi, rw_lo, rb)


def _slot_kernel(d1_ref, d2_ref, o_ref):
    def clear(i, carry):
        o_ref[i] = 0
        return carry

    lax.fori_loop(0, o_ref.shape[0], clear, 0, unroll=16)

    def place(t, carry):
        o_ref[d1_ref[t]] = t
        o_ref[d2_ref[t]] = t
        return carry

    lax.fori_loop(0, d1_ref.shape[0], place, 0, unroll=8)


def _slot_tokens(dest1, dest2, cap):
    smem = pl.BlockSpec(memory_space=pltpu.SMEM)
    return pl.pallas_call(
        _slot_kernel,
        in_specs=[smem, smem],
        out_specs=smem,
        out_shape=jax.ShapeDtypeStruct((cap,), jnp.int32),
        name="moe_slot_tokens",
    )(dest1, dest2)


def _expert_kernel(tok_ref, be_ref, nu_ref, first_ref, nxt_ref, h_hbm, wg_hbm, wu_hbm, wd_hbm, o_ref,
                   x_scr, xb_scr, sg_scr, su_scr, sd_scr, wg_scr, wu_scr, wd_scr, sem_x, sem_w, *, layer):
    blk = pl.program_id(0)
    n_used = nu_ref[0]
    rows = x_scr.shape[1]
    slot = lax.rem(blk, 2)

    def row_copy(b, s, i):
        tok = tok_ref[b * rows + i]
        return pltpu.make_async_copy(h_hbm.at[pl.ds(tok, 1)], x_scr.at[s, pl.ds(i, 1)], sem_x.at[s])

    def start_rows(b, s):
        def go(i, carry):
            row_copy(b, s, i).start()
            return carry

        lax.fori_loop(0, rows, go, 0, unroll=8)

    def weight_copies(e):
        return (pltpu.make_async_copy(wg_hbm.at[layer, e], sg_scr, sem_w.at[0]),
                pltpu.make_async_copy(wu_hbm.at[layer, e], su_scr, sem_w.at[1]),
                pltpu.make_async_copy(wd_hbm.at[layer, e], sd_scr, sem_w.at[2]))

    @pl.when(blk < n_used)
    def _():
        @pl.when(blk == 0)
        def _():
            start_rows(0, 0)
            for cp in weight_copies(be_ref[0]):
                cp.start()

        @pl.when(blk + 1 < n_used)
        def _():
            start_rows(blk + 1, 1 - slot)

        @pl.when(first_ref[blk] == 1)
        def _():
            for cp in weight_copies(be_ref[blk]):
                cp.wait()
            step = 512
            for r in range(0, sg_scr.shape[0], step):
                wg_scr[r:r + step, :] = sg_scr[r:r + step, :].astype(BF16)
                wu_scr[r:r + step, :] = su_scr[r:r + step, :].astype(BF16)
            for r in range(0, sd_scr.shape[1], step * 4):
                wd_scr[:, r:r + step * 4] = sd_scr[:, r:r + step * 4].astype(BF16)

            @pl.when(nxt_ref[blk] >= 0)
            def _():
                for cp in weight_copies(nxt_ref[blk]):
                    cp.start()

        def wait(i, carry):
            row_copy(blk, slot, i).wait()
            return carry

        lax.fori_loop(0, rows, wait, 0, unroll=8)
        groups = xb_scr.shape[1] // (2 * LANE)
        for g in range(groups):
            lo, hi = _unpack_bf16_pair(x_scr[slot, :, g * LANE:(g + 1) * LANE])
            xb_scr[:, 2 * g * LANE:(2 * g + 1) * LANE] = lo.astype(BF16)
            xb_scr[:, (2 * g + 1) * LANE:(2 * g + 2) * LANE] = hi.astype(BF16)
        xb = xb_scr[...]
        gate = jnp.dot(xb, wg_scr[...], preferred_element_type=F32)
        up = jnp.dot(xb, wu_scr[...], preferred_element_type=F32)
        act = (_silu(gate) * up).astype(BF16)
        for g in range(groups):
            y = jnp.dot(act, wd_scr[:, 2 * g * LANE:(2 * g + 2) * LANE], preferred_element_type=F32)
            o_ref[:, g * LANE:(g + 1) * LANE] = _pack_bf16_pair(y[:, :LANE], y[:, LANE:])

    @pl.when(blk >= n_used)
    def _():
        o_ref[...] = jnp.zeros(o_ref.shape, o_ref.dtype)


def _experts(slot_tok, block_expert, n_used, first, nxt, h2, wg_stack, wu_stack, wd_stack, layer):
    t, dw = h2.shape
    d = 2 * dw
    n_blocks = block_expert.shape[0]
    hid = wg_stack.shape[3]
    est = 3 * d * hid * 6 + EXP_BLOCK * d * (2 * 2 + 2 + 2 * 2 + 8) + (6 << 20)
    any_spec = pl.BlockSpec(memory_space=pl.ANY)
    return pl.pallas_call(
        functools.partial(_expert_kernel, layer=layer),
        grid_spec=pltpu.PrefetchScalarGridSpec(
            num_scalar_prefetch=5,
            grid=(n_blocks,),
            in_specs=[any_spec, any_spec, any_spec, any_spec],
            out_specs=pl.BlockSpec((EXP_BLOCK, dw), lambda i, *_: (i, 0)),
            scratch_shapes=[pltpu.VMEM((2, EXP_BLOCK, dw), jnp.uint32), pltpu.VMEM((EXP_BLOCK, d), BF16),
                            pltpu.VMEM((d, hid), F32), pltpu.VMEM((d, hid), F32), pltpu.VMEM((hid, d), F32),
                            pltpu.VMEM((d, hid), BF16), pltpu.VMEM((d, hid), BF16), pltpu.VMEM((hid, d), BF16),
                            pltpu.SemaphoreType.DMA((2,)), pltpu.SemaphoreType.DMA((3,))]),
        out_shape=jax.ShapeDtypeStruct((n_blocks * EXP_BLOCK, dw), jnp.uint32),
        compiler_params=_params(("arbitrary",), est),
        name="moe_experts",
    )(slot_tok, block_expert, n_used, first, nxt, h2, wg_stack, wu_stack, wd_stack)


def _combine_kernel(d1_ref, d2_ref, ys_hbm, x_ref, rw_ref, mod_ref, fw_ref, o_ref, a_scr, b_scr, sem,
                    *, g_idx, final):
    i = pl.program_id(0)
    rows = a_scr.shape[1]
    slot = lax.rem(i, 2)

    def copies(step, s, r):
        t = step * rows + r
        return (pltpu.make_async_copy(ys_hbm.at[pl.ds(d1_ref[t], 1)], a_scr.at[s, pl.ds(r, 1)], sem.at[0, s]),
                pltpu.make_async_copy(ys_hbm.at[pl.ds(d2_ref[t], 1)], b_scr.at[s, pl.ds(r, 1)], sem.at[1, s]))

    def start_rows(step, s):
        def go(r, carry):
            ca, cb = copies(step, s, r)
            ca.start()
            cb.start()
            return carry

        lax.fori_loop(0, rows, go, 0, unroll=8)

    @pl.when(i == 0)
    def _():
        start_rows(0, 0)

    def wait_rows(s):
        def wait(r, carry):
            ca, cb = copies(i, s, r)
            ca.wait()
            cb.wait()
            return carry

        lax.fori_loop(0, rows, wait, 0, unroll=8)

    wait_rows(slot)
    last = pl.num_programs(0) - 1
    nxt = jnp.minimum(i + 1, last)
    for r in range(rows):
        ca, cb = copies(nxt, 1 - slot, r)
        ca.start()
        cb.start()
    rw = rw_ref[...]
    w1 = rw[:, 0:1]
    w2 = rw[:, 1:2]
    d = x_ref.shape[1]
    ssq = jnp.zeros((rows, 1), F32)
    for g in range(d // (2 * LANE)):
        halves_a = _unpack_bf16_pair(a_scr[slot, :, g * LANE:(g + 1) * LANE])
        halves_b = _unpack_bf16_pair(b_scr[slot, :, g * LANE:(g + 1) * LANE])
        for half in range(2):
            c0 = (2 * g + half) * LANE
            moe = w1 * halves_a[half] + w2 * halves_b[half]
            xn = x_ref[:, c0:c0 + LANE] + mod_ref[0, g_idx:g_idx + 1, c0:c0 + LANE] * moe
            o_ref[:, c0:c0 + LANE] = xn
            if final:
                ssq = ssq + jnp.sum(xn * xn, axis=-1, keepdims=True)
    if final:
        o_ref[...] = o_ref[...] * lax.rsqrt(ssq * (1.0 / d) + NORM_EPS) * fw_ref[...]

    @pl.when(i == last)
    def _():
        wait_rows(1 - slot)


def _combine(dest1, dest2, ys, x2, route_w, mod, final_w, seq, g_idx, final, tm=128):
    t, d = x2.shape
    per_b = seq // tm
    est = 2 * (2 * tm * d * 4 + tm * LANE * 4) + 4 * tm * d * 4 + (6 << 20)
    return pl.pallas_call(
        functools.partial(_combine_kernel, g_idx=g_idx, final=final),
        grid_spec=pltpu.PrefetchScalarGridSpec(
            num_scalar_prefetch=2,
            grid=(t // tm,),
            in_specs=[pl.BlockSpec(memory_space=pl.ANY),
                      pl.BlockSpec((tm, d), lambda i, d1, d2: (i, 0)),
                      pl.BlockSpec((tm, LANE), lambda i, d1, d2: (i, 0)),
                      pl.BlockSpec((1, 6, d), lambda i, d1, d2: (i // per_b, 0, 0)),
                      pl.BlockSpec((1, d), lambda i, d1, d2: (0, 0))],
            out_specs=pl.BlockSpec((tm, d), lambda i, d1, d2: (i, 0)),
            scratch_shapes=[pltpu.VMEM((2, tm, d // 2), jnp.uint32), pltpu.VMEM((2, tm, d // 2), jnp.uint32),
                            pltpu.SemaphoreType.DMA((2, 2))]),
        out_shape=jax.ShapeDtypeStruct((t, d), F32),
        compiler_params=_params(("arbitrary",), est),
        name="moe_combine",
    )(dest1, dest2, ys, x2, route_w, mod, final_w.reshape(1, d))


def _small_in_weights(wt_stack, layer):
    d = wt_stack.shape[2]
    bounds = np.cumsum((0,) + IN_SIZES_HEAD)
    seg = [wt_stack[layer, bounds[i]:bounds[i + 1], :] for i in range(3, len(IN_SIZES_HEAD))]
    dt, q_lat, kv_lat, k_rope, sq, sk, sv = seg
    z64 = jnp.zeros((SWA_HEAD_DIM, d), wt_stack.dtype)

    def spread(w):
        h0, h1 = w[:SWA_HEAD_DIM], w[SWA_HEAD_DIM:]
        return jnp.concatenate([h0, z64, z64, h0, h1, z64, z64, h1], axis=0)

    half = MLA_ROPE // 2
    rot = jnp.concatenate([-k_rope[half:], k_rope[:half]], axis=0)
    small = jnp.concatenate([sq, spread(sk), spread(sv), kv_lat,
                             jnp.zeros((OFF_S["qlat"] - OFF_S["kvlat"] - MLA_KV_LORA, d), wt_stack.dtype),
                             q_lat], axis=0)
    tail = jnp.concatenate([k_rope, z64, rot, z64, dt,
                            jnp.zeros((LANE - SSM_HEADS, d), wt_stack.dtype)], axis=0)
    return small[None], tail[None]


def _extend_wq(wq_b):
    k = wq_b.shape[0]
    w = wq_b.reshape(k, MLA_HEADS, MLA_NOPE + MLA_ROPE)
    z = jnp.zeros((k, MLA_HEADS, LANE - MLA_ROPE), wq_b.dtype)
    rope = w[..., MLA_NOPE:]
    half = MLA_ROPE // 2
    rot = jnp.concatenate([-rope[..., half:], rope[..., :half]], axis=-1)
    return jnp.concatenate([w[..., :MLA_NOPE], rope, z, rot, z], axis=-1).reshape(k, -1).astype(BF16)


def _mixer(x, mod, pos_col, pos_row, cos_t, sin_t, norm_w, w_in_stack, layer, conv_w, conv_b, dt_bias, a_log,
           d_skip, ssm_norm, mla_q_norm, mla_wq_b, mla_kv_norm, mla_wkv_b, swa_sinks, w_branch, w_out):
    b, s, d = x.shape
    t = b * s
    swa_slopes, moba_slopes = _alibi_slopes()
    h = _norm_mod(x, norm_w, mod, sh_idx=0, sc_idx=1)
    h2 = h.reshape(t, d)
    w_small, w_tail = _small_in_weights(w_in_stack, layer)
    proj_a = _wproj(h2, w_in_stack, layer, 0, A_COLS, "in_proj_head")
    proj_g = _wproj(h2, w_in_stack, layer, GATE_COL0, N_BRANCH * d, "in_proj_gates", tm=512, tn=1024)
    proj_s = _wproj(h2, w_small, 0, 0, S_COLS, "in_proj_small", tn=768)
    tail = _wproj(h2, w_tail, 0, 0, 3 * LANE, "in_proj_tail", tn=3 * LANE, out_dtype=F32)
    proj_a3 = proj_a.reshape(b, s, A_COLS)
    proj_s3 = proj_s.reshape(b, s, S_COLS)
    tail3 = tail.reshape(b, s, 3 * LANE)

    o_moba = _flash(proj_a3, proj_a3, proj_a3, heads=MOBA_HEADS, dq=MOBA_HEAD_DIM, dv=MOBA_HEAD_DIM,
                    q_off=OFF_A["mq"], k_off=OFF_A["mk"], v_off=OFF_A["mv"],
                    moba=True, q_scale=float(MOBA_HEAD_DIM ** -0.5) * LOG2E,
                    slopes=jnp.asarray(moba_slopes * np.float32(LOG2E)),
                    pos_col=pos_col, pos_row=pos_row, name="moba_attention")
    o_ssm = _ssd(proj_a3, tail3, OFF_A, conv_w, conv_b, dt_bias, a_log, d_skip, ssm_norm)
    q_m, k_m, v_m = _mla_project(proj_s, tail, OFF_S, mla_q_norm, _extend_wq(mla_wq_b), mla_kv_norm,
                                 mla_wkv_b.astype(BF16), cos_t, sin_t)
    o_mla = _flash(q_m.reshape(b, s, -1), k_m.reshape(b, s, -1), v_m.reshape(b, s, -1),
                   heads=MLA_HEADS, dq=MLA_QK, dv=MLA_V, q_off=0, k_off=0, v_off=0, name="mla_attention")
    o_swa = _swa(proj_s3, OFF_S, swa_sinks, pos_col, pos_row, swa_slopes)

    branches = [o.reshape(t, BRANCH_W) for o in (o_moba, o_ssm, o_mla, o_swa)]
    merged = _merge(branches, w_branch, layer, proj_g, d)
    return _out_proj(merged, w_out, layer, x.reshape(t, d), mod, s, g_idx=2)


def _moe(x2, seq, mod, norm_w, wg, bg, wr, br, w_gate, w_up, w_down, layer, final_w, final):
    t, d = x2.shape
    b = t // seq
    h, route_i, route_w, counts = _router(x2.reshape(b, seq, d), norm_w, mod, wg, bg, wr, br,
                                          sh_idx=3, sc_idx=4)
    route_i = route_i.reshape(t, LANE)
    counts = counts[0, :N_EXPERTS].astype(jnp.int32)
    padded = (counts + EXP_BLOCK - 1) // EXP_BLOCK * EXP_BLOCK
    pad_end = jnp.cumsum(padded)
    pad_start = pad_end - padded
    n_blocks = -(-(t * 2) // EXP_BLOCK) + N_EXPERTS
    dest1 = pad_start[route_i[:, 0]] + route_i[:, 2]
    dest2 = pad_start[route_i[:, 1]] + route_i[:, 3]
    block_row0 = jnp.arange(n_blocks, dtype=jnp.int32) * EXP_BLOCK
    block_expert = jnp.minimum(jnp.sum((pad_end[None, :] <= block_row0[:, None]).astype(jnp.int32), axis=1),
                               N_EXPERTS - 1)
    n_used = (pad_end[-1:] // EXP_BLOCK).astype(jnp.int32)
    first = (block_expert != jnp.concatenate([jnp.full((1,), -1, jnp.int32), block_expert[:-1]])).astype(jnp.int32)
    e_ids = jnp.arange(N_EXPERTS, dtype=jnp.int32)
    later = jnp.where((counts > 0)[None, :] & (e_ids[None, :] > e_ids[:, None]), e_ids[None, :], N_EXPERTS)
    nxt_e = jnp.min(later, axis=1)
    nxt = jnp.where(nxt_e == N_EXPERTS, -1, nxt_e)[block_expert].astype(jnp.int32)
    slot_tok = _slot_tokens(dest1, dest2, n_blocks * EXP_BLOCK)
    ys = _experts(slot_tok, block_expert, n_used, first, nxt, h.reshape(t, d // 2), w_gate, w_up, w_down, layer)
    return _combine(dest1, dest2, ys, x2, route_w.reshape(t, LANE), mod, final_w, seq, g_idx=5, final=final)


def kernel(x, c, positions, ada_w, ada_b, norm_mix, norm_ffn, w_in, conv_w, conv_b, dt_bias, a_log, d_skip,
           ssm_norm, mla_q_norm, mla_wq_b, mla_kv_norm, mla_wkv_b, swa_sinks, w_branch, w_out,
           router_group_w, router_group_b, router_w, router_b, exp_w_gate, exp_w_up, exp_w_down, final_norm):
    b, s, d = x.shape
    depth = ada_w.shape[0]
    mods = _ada_mod(c, ada_w, ada_b)
    pos_f = positions.astype(F32)
    pos_col = pos_f.reshape(b, s, 1)
    pos_row = pos_f.reshape(b, 1, s)
    cos_t, sin_t = _rope_tables(pos_f.reshape(b * s, 1))
    w_in_t = jnp.swapaxes(w_in, 1, 2)
    for l in range(depth):
        x2 = _mixer(x, mods[l], pos_col, pos_row, cos_t, sin_t, norm_mix[l], w_in_t, l, conv_w[l], conv_b[l],
                    dt_bias[l], a_log[l], d_skip[l], ssm_norm[l], mla_q_norm[l], mla_wq_b[l],
                    mla_kv_norm[l], mla_wkv_b[l], swa_sinks[l], w_branch, w_out)
        x2 = _moe(x2, s, mods[l], norm_ffn[l], router_group_w[l], router_group_b[l], router_w[l],
                  router_b[l], exp_w_gate, exp_w_up, exp_w_down, l, final_norm,
                  final=(l == depth - 1))
        x = x2.reshape(b, s, d)
    return x
```
